```python
import math
import jax
import jax.numpy as jnp
from jax import lax
import numpy as np


D_MODEL = 1024
BATCH = 4
SEQ = 4096
DEPTH = 2

GRID_W = 64
CTX_LEN = 256

N_MOD = 6
NORM_EPS = 1e-6
NEG_INF = -1e30

ATT_HEADS = 8
ATT_KV_HEADS = 2
ATT_GROUP = ATT_HEADS // ATT_KV_HEADS
HEAD_DIM = 64
ATT_WIDTH = ATT_HEADS * HEAD_DIM
KV_WIDTH = ATT_KV_HEADS * HEAD_DIM
WINDOW = 128
ATT_BLOCK = 128
ROPE_BASE = 10000.0
ROPE_PAIRS_PER_AXIS = HEAD_DIM // 4

HY_WIDTH = 512
HY_ORDER = 2
HY_N_PROJ = HY_ORDER + 1
HY_SHORT = 3
HY_BANDS = 16
HY_EMB = 1 + 2 * HY_BANDS
HY_FILTER_HIDDEN = 64
HY_DECAY_TARGET = 1e-2
HY_FAST_RATE = -math.log(HY_DECAY_TARGET) / 0.3
HY_SLOW_RATE = -math.log(HY_DECAY_TARGET) / 1.5

S5_WIDTH = 512
S5_GROUP = 16
S5_GROUPS = S5_WIDTH // S5_GROUP
S5_STATE = 64
S5_DT_MIN = 1e-3
S5_DT_MAX = 1e-1

N_BRANCH = 3

PEER_HEADS = 8
PEER_KEYS = 128
PEER_EXPERTS = PEER_KEYS * PEER_KEYS
PEER_HALF = 128
PEER_QDIM = 2 * PEER_HALF
PEER_TOPK = 16
PEER_BLOCK = 128

COL_Q = 0
COL_K = COL_Q + ATT_WIDTH
COL_V = COL_K + KV_WIDTH
COL_S5 = COL_V + KV_WIDTH
COL_HY = COL_S5 + S5_WIDTH
COL_GATE = COL_HY + HY_N_PROJ * HY_WIDTH
IN_WIDTH = COL_GATE + N_BRANCH * D_MODEL

kernel_name = 'hybrid_diffusion_trunk'


def rms_norm(x, g):
    xf = x.astype(jnp.float32)
    y = xf * lax.rsqrt(jnp.mean(xf * xf, axis=-1, keepdims=True) + NORM_EPS)
    return (y * g.astype(jnp.float32)).astype(x.dtype)


def modulate(h, shift, scale):
    return h * (1 + scale) + shift


def axial_rope_tables(L):
    rows_n = L // GRID_W
    row = jnp.repeat(jnp.arange(rows_n), GRID_W).astype(jnp.float32)
    col = jnp.tile(jnp.arange(GRID_W), rows_n).astype(jnp.float32)
    inv = jnp.power(ROPE_BASE, -jnp.arange(ROPE_PAIRS_PER_AXIS, dtype=jnp.float32) / ROPE_PAIRS_PER_AXIS)
    ang = jnp.concatenate([row[:, None] * inv, col[:, None] * inv], axis=-1)
    return jnp.cos(ang), jnp.sin(ang)


def apply_rope(t, cos, sin):
    tf = t.astype(jnp.float32)
    t1, t2 = jnp.split(tf, 2, axis=-1)
    c = cos[None, :, None, :]
    s = sin[None, :, None, :]
    return jnp.concatenate([t1 * c - t2 * s, t1 * s + t2 * c], axis=-1).astype(t.dtype)


def latent_window_attention(q, k, v, kc, vc, sink):
    B, L = q.shape[:2]
    C = kc.shape[1]
    nb = L // ATT_BLOCK
    scale = HEAD_DIM ** -0.5
    qb = q.reshape(B, nb, ATT_BLOCK, ATT_KV_HEADS, ATT_GROUP, HEAD_DIM)
    pad = ((0, 0), (ATT_BLOCK, ATT_BLOCK), (0, 0), (0, 0))

    def band(t):
        tp = jnp.pad(t, pad).reshape(B, nb + 2, ATT_BLOCK, ATT_KV_HEADS, HEAD_DIM)
        return jnp.concatenate([tp[:, :-2], tp[:, 1:-1], tp[:, 2:]], axis=2)

    kb = band(k)
    vb = band(v)
    s_loc = jnp.einsum('bnqhgd,bnkhd->bnhgqk', qb, kb).astype(jnp.float32) * scale
    s_ctx = jnp.einsum('bnqhgd,bchd->bnhgqc', qb, kc).astype(jnp.float32) * scale
    blk = jnp.arange(nb)[:, None, None] * ATT_BLOCK
    qpos = blk + jnp.arange(ATT_BLOCK)[None, :, None]
    kpos = blk - ATT_BLOCK + jnp.arange(3 * ATT_BLOCK)[None, None, :]
    valid = (jnp.abs(kpos - qpos) <= WINDOW) & (kpos >= 0) & (kpos < L)
    s_loc = jnp.where(valid[None, :, None, None], s_loc, NEG_INF)
    s_sink = jnp.broadcast_to(sink.astype(jnp.float32).reshape(1, 1, ATT_KV_HEADS, ATT_GROUP, 1, 1),
                              s_ctx.shape[:-1] + (1,))
    p = jax.nn.softmax(jnp.concatenate([s_sink, s_ctx, s_loc], axis=-1), axis=-1).astype(v.dtype)
    p_ctx = p[..., 1:1 + C]
    p_loc = p[..., 1 + C:]
    o = (jnp.einsum('bnhgqc,bchd->bnqhgd', p_ctx, vc)
         + jnp.einsum('bnhgqk,bnkhd->bnqhgd', p_loc, vb))
    return o.reshape(B, L, ATT_WIDTH)


def context_attention(qc, kc, vc, sink):
    B, C = qc.shape[:2]
    qg = qc.reshape(B, C, ATT_KV_HEADS, ATT_GROUP, HEAD_DIM)
    s = jnp.einsum('bqhgd,bkhd->bhgqk', qg, kc).astype(jnp.float32) * (HEAD_DIM ** -0.5)
    s_sink = jnp.broadcast_to(sink.astype(jnp.float32).reshape(1, ATT_KV_HEADS, ATT_GROUP, 1, 1),
                              s.shape[:-1] + (1,))
    p = jax.nn.softmax(jnp.concatenate([s_sink, s], axis=-1), axis=-1)[..., 1:].astype(vc.dtype)
    o = jnp.einsum('bhgqk,bkhd->bqhgd', p, vc)
    return o.reshape(B, C, ATT_WIDTH)


def short_conv(z, w, b):
    L = z.shape[1]
    half = HY_SHORT // 2
    zp = jnp.pad(z, ((0, 0), (half, half), (0, 0)))
    out = b
    for j in range(HY_SHORT):
        out = out + zp[:, j:j + L] * w[j]
    return out


def hyena_filters(L, lp):
    f32 = jnp.float32
    t = jnp.arange(L, dtype=f32)
    tn = t / L
    bands = jnp.arange(1, HY_BANDS + 1, dtype=f32)
    ang = 2.0 * math.pi * tn[:, None] * bands[None, :]
    z = jnp.concatenate([tn[:, None], jnp.cos(ang), jnp.sin(ang)], axis=-1)
    hdn = jnp.sin(lp['hy_freq1'].astype(f32) * (z @ lp['hy_w1'].astype(f32) + lp['hy_b1'].astype(f32)))
    hdn = jnp.sin(lp['hy_freq2'].astype(f32) * (hdn @ lp['hy_w2'].astype(f32) + lp['hy_b2'].astype(f32)))
    filt = hdn @ lp['hy_w3'].astype(f32)
    rate = jnp.linspace(HY_FAST_RATE, HY_SLOW_RATE, HY_WIDTH, dtype=f32)
    tw = jnp.linspace(0.0, 1.0, L, dtype=f32)
    window = jnp.exp(-tw[:, None] * rate[None, :])
    return filt[:, :HY_WIDTH] * window, filt[:, HY_WIDTH:] * window


def bidirectional_fftconv(u, h_fwd, h_bwd):
    L = u.shape[1]
    n = 2 * L
    kern = jnp.concatenate([h_fwd, jnp.zeros_like(h_fwd[:1]), h_bwd[1:][::-1]], axis=0)
    kf = jnp.fft.rfft(kern, n=n, axis=0)
    uf = jnp.fft.rfft(u.astype(jnp.float32), n=n, axis=1)
    return jnp.fft.irfft(uf * kf[None], n=n, axis=1)[:, :L]


def hyena_mixer(z, lp):
    L = z.shape[1]
    z = short_conv(z, lp['hy_short_w'], lp['hy_short_b'])
    x0, x1, v = jnp.split(z, HY_N_PROJ, axis=-1)
    h_fwd, h_bwd = hyena_filters(L, lp)
    v = v * x1
    y = bidirectional_fftconv(v, h_fwd, h_bwd).astype(z.dtype) + v * lp['hy_bias']
    return y * x0


def s5_discretize(lp, d):
    f32 = jnp.float32
    lam = lax.complex(jnp.minimum(lp['s5_a_re'][d].astype(f32), -1e-4), lp['s5_a_im'][d].astype(f32))
    dt = jnp.exp(lp['s5_log_dt'][d].astype(f32))[:, None]
    abar = jnp.exp(lam * dt)
    b = lax.complex(lp['s5_b_re'][d].astype(f32), lp['s5_b_im'][d].astype(f32))
    bbar = ((abar - 1.0) / lam)[..., None] * b
    cmat = lax.complex(lp['s5_c_re'][d].astype(f32), lp['s5_c_im'][d].astype(f32))
    return abar, bbar, cmat


def s5_drive(u, bbar):
    B, L, _ = u.shape
    ug = u.astype(jnp.float32).reshape(B, L, S5_GROUPS, S5_GROUP).astype(jnp.complex64)
    return jnp.einsum('blgh,gph->blgp', ug, bbar)


def s5_scan(bu, abar, h0, reverse):
    if h0 is not None:
        edge = -1 if reverse else 0
        bu = bu.at[:, edge].add(abar * h0)
    a = jnp.broadcast_to(abar, bu.shape)

    def combine(e1, e2):
        a1, b1 = e1
        a2, b2 = e2
        return a1 * a2, a2 * b1 + b2

    _, h = lax.associative_scan(combine, (a, bu), reverse=reverse, axis=1)
    return h


def s5_readout(h_fwd, h_bwd, c_fwd, c_bwd, u, lp):
    B, L, _ = u.shape
    y = (jnp.einsum('blgp,ghp->blgh', h_fwd, c_fwd)
         + jnp.einsum('blgp,ghp->blgh', h_bwd, c_bwd)).real.reshape(B, L, S5_WIDTH)
    y = (y + lp['s5_d'].astype(jnp.float32) * u.astype(jnp.float32)).astype(u.dtype)
    y = jax.nn.gelu(y)
    return y * jax.nn.sigmoid(y @ lp['s5_glu_w'] + lp['s5_glu_b'])


def s5_mixer(u, uc, lp, need_ctx):
    af, bf, cf = s5_discretize(lp, 0)
    ab, bb, cb = s5_discretize(lp, 1)
    hcf = s5_scan(s5_drive(uc, bf), af, None, False)
    hcb = s5_scan(s5_drive(uc, bb), ab, None, True)
    hf = s5_scan(s5_drive(u, bf), af, hcf[:, -1], False)
    hb = s5_scan(s5_drive(u, bb), ab, hcb[:, 0], True)
    y = s5_readout(hf, hb, cf, cb, u, lp)
    yc = s5_readout(hcf, hcb, cf, cb, uc, lp) if need_ctx else None
    return y, yc


def branch_merge(att, hy, s5, gate_logits, lp):
    g = jax.nn.sigmoid(gate_logits.astype(jnp.float32)).astype(att.dtype)
    ga, gh, gs = jnp.split(g, N_BRANCH, axis=-1)
    m = ga * (att @ lp['br_attn_w']) + gh * (hy @ lp['br_hyena_w']) + gs * (s5 @ lp['br_s5_w'])
    return m @ lp['out_w']


def token_mixer(h, hc, rope_cos, rope_sin, lp, need_ctx):
    B, L, _ = h.shape
    C = hc.shape[1]
    w_in = lp['in_w']
    proj = h @ w_in
    q = apply_rope(proj[..., COL_Q:COL_K].reshape(B, L, ATT_HEADS, HEAD_DIM), rope_cos, rope_sin)
    k = apply_rope(proj[..., COL_K:COL_V].reshape(B, L, ATT_KV_HEADS, HEAD_DIM), rope_cos, rope_sin)
    v = proj[..., COL_V:COL_S5].reshape(B, L, ATT_KV_HEADS, HEAD_DIM)
    u_s5 = proj[..., COL_S5:COL_HY]
    z_hy = proj[..., COL_HY:COL_GATE]
    gates = proj[..., COL_GATE:] + lp['gate_b']
    lo = COL_Q if need_ctx else COL_K
    hi = IN_WIDTH if need_ctx else COL_HY
    projc = hc @ w_in[:, lo:hi]
    kc = projc[..., COL_K - lo:COL_V - lo].reshape(B, C, ATT_KV_HEADS, HEAD_DIM)
    vc = projc[..., COL_V - lo:COL_S5 - lo].reshape(B, C, ATT_KV_HEADS, HEAD_DIM)
    uc_s5 = projc[..., COL_S5 - lo:COL_HY - lo]
    att = latent_window_attention(q, k, v, kc, vc, lp['attn_sink'])
    hy = hyena_mixer(z_hy, lp)
    s5, s5c = s5_mixer(u_s5, uc_s5, lp, need_ctx)
    out = branch_merge(att, hy, s5, gates, lp)
    if not need_ctx:
        return out, None
    qc = projc[..., COL_Q - lo:COL_K - lo].reshape(B, C, ATT_HEADS, HEAD_DIM)
    att_c = context_attention(qc, kc, vc, lp['attn_sink'])
    hy_c = hyena_mixer(projc[..., COL_HY - lo:COL_GATE - lo], lp)
    gates_c = projc[..., COL_GATE - lo:] + lp['gate_b']
    out_c = branch_merge(att_c, hy_c, s5c, gates_c, lp)
    return out, out_c


def peer_ffn(x, wq, sub_keys, u_tab, v_tab):
    B, L, D = x.shape
    nb = L // PEER_BLOCK
    xb = jnp.swapaxes(x.reshape(B, nb, PEER_BLOCK, D), 0, 1)
    kk = PEER_TOPK * PEER_TOPK

    def block(xc):
        q = (xc @ wq).reshape(B, PEER_BLOCK, PEER_HEADS, 2, PEER_HALF)
        s = jnp.einsum('bthsk,hsnk->bthsn', q, sub_keys).astype(jnp.float32)
        s_top, i_top = lax.top_k(s, PEER_TOPK)
        cand = (s_top[..., 0, :, None] + s_top[..., 1, None, :]).reshape(B, PEER_BLOCK, PEER_HEADS, kk)
        cand_id = (i_top[..., 0, :, None] * PEER_KEYS + i_top[..., 1, None, :]).reshape(B, PEER_BLOCK, PEER_HEADS, kk)
        best, pos = lax.top_k(cand, PEER_TOPK)
        expert = jnp.take_along_axis(cand_id, pos, axis=-1)
        gate = jax.nn.softmax(best, axis=-1)
        act = jax.nn.gelu(jnp.einsum('btd,bthkd->bthk', xc, u_tab[expert]).astype(jnp.float32))
        w = (gate * act).astype(xc.dtype)
        return jnp.einsum('bthk,bthkd->btd', w, v_tab[expert])

    return jnp.swapaxes(lax.map(block, xb), 0, 1).reshape(B, L, D)


def setup_inputs(seed: int = 0) -> dict:
    key = jax.random.key(seed)
    keys = iter(jax.random.split(key, 64))
    f32 = jnp.float32

    def nrm(shape, std):
        return std * jax.random.normal(next(keys), shape, f32)

    def near_one(shape):
        return 1.0 + nrm(shape, 0.02)

    L = DEPTH
    hy_cols = HY_N_PROJ * HY_WIDTH
    a_im0 = math.pi * jnp.arange(S5_STATE, dtype=f32)
    return {
        'x': nrm((BATCH, SEQ, D_MODEL), 1.0),
        'c': nrm((BATCH, D_MODEL), 1.0),
        'ctx': nrm((BATCH, CTX_LEN, D_MODEL), 1.0),
        'c_ctx': nrm((D_MODEL,), 1.0),
        'mod_w': nrm((L, D_MODEL, N_MOD * D_MODEL), 0.5 * D_MODEL ** -0.5),
        'mod_b': nrm((L, N_MOD * D_MODEL), 0.02),
        'norm1_g': near_one((L, D_MODEL)),
        'norm2_g': near_one((L, D_MODEL)),
        'in_w': nrm((L, D_MODEL, IN_WIDTH), D_MODEL ** -0.5),
        'gate_b': nrm((L, N_BRANCH * D_MODEL), 0.02),
        'attn_sink': nrm((L, ATT_HEADS), 0.5),
        'hy_short_w': nrm((L, HY_SHORT, hy_cols), HY_SHORT ** -0.5),
        'hy_short_b': nrm((L, hy_cols), 0.02),
        'hy_w1': nrm((L, HY_EMB, HY_FILTER_HIDDEN), HY_EMB ** -0.5),
        'hy_b1': nrm((L, HY_FILTER_HIDDEN), 0.1),
        'hy_freq1': 1.0 + nrm((L, HY_FILTER_HIDDEN), 0.1),
        'hy_w2': nrm((L, HY_FILTER_HIDDEN, HY_FILTER_HIDDEN), HY_FILTER_HIDDEN ** -0.5),
        'hy_b2': nrm((L, HY_FILTER_HIDDEN), 0.1),
        'hy_freq2': 1.0 + nrm((L, HY_FILTER_HIDDEN), 0.1),
        'hy_w3': nrm((L, HY_FILTER_HIDDEN, 2 * HY_WIDTH), 0.01),
        'hy_bias': nrm((L, HY_WIDTH), 0.5),
        's5_a_re': -0.5 + nrm((L, 2, S5_GROUPS, S5_STATE), 0.01),
        's5_a_im': a_im0 + nrm((L, 2, S5_GROUPS, S5_STATE), 0.01),
        's5_log_dt': jax.random.uniform(next(keys), (L, 2, S5_GROUPS), f32, math.log(S5_DT_MIN), math.log(S5_DT_MAX)),
        's5_b_re': nrm((L, 2, S5_GROUPS, S5_STATE, S5_GROUP), (2 * S5_GROUP) ** -0.5),
        's5_b_im': nrm((L, 2, S5_GROUPS, S5_STATE, S5_GROUP), (2 * S5_GROUP) ** -0.5),
        's5_c_re': nrm((L, 2, S5_GROUPS, S5_GROUP, S5_STATE), S5_STATE ** -0.5),
        's5_c_im': nrm((L, 2, S5_GROUPS, S5_GROUP, S5_STATE), S5_STATE ** -0.5),
        's5_d': nrm((L, S5_WIDTH), 1.0),
        's5_glu_w': nrm((L, S5_WIDTH, S5_WIDTH), S5_WIDTH ** -0.5),
        's5_glu_b': nrm((L, S5_WIDTH), 0.02),
        'br_attn_w': nrm((L, ATT_WIDTH, D_MODEL), ATT_WIDTH ** -0.5),
        'br_hyena_w': nrm((L, HY_WIDTH, D_MODEL), HY_WIDTH ** -0.5),
        'br_s5_w': nrm((L, S5_WIDTH, D_MODEL), S5_WIDTH ** -0.5),
        'out_w': nrm((L, D_MODEL, D_MODEL), D_MODEL ** -0.5),
        'peer_wq': nrm((L, D_MODEL, PEER_HEADS * PEER_QDIM), D_MODEL ** -0.5),
        'peer_keys': nrm((L, PEER_HEADS, 2, PEER_KEYS, PEER_HALF), PEER_HALF ** -0.5),
        'peer_u': nrm((L, PEER_EXPERTS, D_MODEL), D_MODEL ** -0.5),
        'peer_v': nrm((L, PEER_EXPERTS, D_MODEL), PEER_HEADS ** -0.5),
        'final_g': near_one((D_MODEL,)),
    }


def reference(x, c, ctx, c_ctx, mod_w, mod_b, norm1_g, norm2_g, in_w, gate_b, attn_sink,
              hy_short_w, hy_short_b, hy_w1, hy_b1, hy_freq1, hy_w2, hy_b2, hy_freq2, hy_w3, hy_bias,
              s5_a_re, s5_a_im, s5_log_dt, s5_b_re, s5_b_im, s5_c_re, s5_c_im, s5_d, s5_glu_w, s5_glu_b,
              br_attn_w, br_hyena_w, br_s5_w, out_w, peer_wq, peer_keys, peer_u, peer_v, final_g):
    L = x.shape[1]
    rope_cos, rope_sin = axial_rope_tables(L)
    xc = ctx
    silu_c = jax.nn.silu(c)
    silu_cc = jax.nn.silu(c_ctx)
    for layer in range(DEPTH):
        need_ctx = layer < DEPTH - 1
        lp = dict(in_w=in_w[layer], gate_b=gate_b[layer], attn_sink=attn_sink[layer],
                  hy_short_w=hy_short_w[layer], hy_short_b=hy_short_b[layer],
                  hy_w1=hy_w1[layer], hy_b1=hy_b1[layer], hy_freq1=hy_freq1[layer],
                  hy_w2=hy_w2[layer], hy_b2=hy_b2[layer], hy_freq2=hy_freq2[layer],
                  hy_w3=hy_w3[layer], hy_bias=hy_bias[layer],
                  s5_a_re=s5_a_re[layer], s5_a_im=s5_a_im[layer], s5_log_dt=s5_log_dt[layer],
                  s5_b_re=s5_b_re[layer], s5_b_im=s5_b_im[layer],
                  s5_c_re=s5_c_re[layer], s5_c_im=s5_c_im[layer], s5_d=s5_d[layer],
                  s5_glu_w=s5_glu_w[layer], s5_glu_b=s5_glu_b[layer],
                  br_attn_w=br_attn_w[layer], br_hyena_w=br_hyena_w[layer], br_s5_w=br_s5_w[layer],
                  out_w=out_w[layer])
        mod = (silu_c @ mod_w[layer] + mod_b[layer])[:, None, :]
        sh1, sc1, g1, sh2, sc2, g2 = jnp.split(mod, N_MOD, axis=-1)
        n_c = (N_MOD if need_ctx else 2) * D_MODEL
        mc = jnp.split(silu_cc @ mod_w[layer][:, :n_c] + mod_b[layer][:n_c], n_c // D_MODEL)
        h = modulate(rms_norm(x, norm1_g[layer]), sh1, sc1)
        hc = modulate(rms_norm(xc, norm1_g[layer]), mc[0], mc[1])
        y, yc = token_mixer(h, hc, rope_cos, rope_sin, lp, need_ctx)
        x = x + g1 * y
        h = modulate(rms_norm(x, norm2_g[layer]), sh2, sc2)
        x = x + g2 * peer_ffn(h, peer_wq[layer], peer_keys[layer], peer_u[layer], peer_v[layer])
        if need_ctx:
            xc = xc + mc[2] * yc
            hc = modulate(rms_norm(xc, norm2_g[layer]), mc[3], mc[4])
            xc = xc + mc[5] * peer_ffn(hc, peer_wq[layer], peer_keys[layer], peer_u[layer], peer_v[layer])
    return rms_norm(x, final_g)
```

```python
import functools
import math

import jax
import jax.numpy as jnp
import numpy as np
from jax import lax
from jax.experimental import pallas as pl
from jax.experimental.pallas import tpu as pltpu

F32 = jnp.float32
BF16 = jnp.bfloat16

NORM_EPS = 1e-6
NEG_INF = -1e30
N_MOD = 6

GRID_W = 64
ATT_HEADS = 8
ATT_KV_HEADS = 2
ATT_GROUP = ATT_HEADS // ATT_KV_HEADS
HEAD_DIM = 64
ATT_WIDTH = ATT_HEADS * HEAD_DIM
KV_WIDTH = ATT_KV_HEADS * HEAD_DIM
WINDOW = 128
ATT_BLOCK = 128
ROPE_BASE = 10000.0
ROPE_PAIRS_PER_AXIS = HEAD_DIM // 4

HY_WIDTH = 512
HY_N_PROJ = 3
HY_SHORT = 3
HY_BANDS = 16
HY_DECAY_TARGET = 1e-2
HY_FAST_RATE = -math.log(HY_DECAY_TARGET) / 0.3
HY_SLOW_RATE = -math.log(HY_DECAY_TARGET) / 1.5

S5_WIDTH = 512
S5_GROUP = 16
S5_GROUPS = S5_WIDTH // S5_GROUP
S5_STATE = 64

N_BRANCH = 3

PEER_HEADS = 8
PEER_KEYS = 128
PEER_HALF = 128
PEER_TOPK = 16

COL_Q = 0
COL_K = COL_Q + ATT_WIDTH
COL_V = COL_K + KV_WIDTH
COL_S5 = COL_V + KV_WIDTH
COL_HY = COL_S5 + S5_WIDTH
COL_GATE = COL_HY + HY_N_PROJ * HY_WIDTH

VMEM_LIMIT_V7X = 56 * 1024 * 1024


def _cparams(*sem):
    return pltpu.CompilerParams(dimension_semantics=sem, vmem_limit_bytes=VMEM_LIMIT_V7X)


def _gelu_tanh(x):
    return 0.5 * x * (1.0 + jnp.tanh(math.sqrt(2.0 / math.pi) * (x + 0.044715 * (x * x * x))))


def _mod_kernel(s_ref, w_ref, b_ref, o_ref):
    s = s_ref[...]
    s = (s * jax.nn.sigmoid(s)).astype(BF16)
    o_ref[0] = jnp.dot(s, w_ref[0].astype(BF16), preferred_element_type=F32) + b_ref[0]


def modulation(cvec, mod_w, mod_b):
    depth, d, n = mod_w.shape
    tn = 1024
    return pl.pallas_call(
        _mod_kernel,
        grid=(depth, n // tn),
        in_specs=[pl.BlockSpec((8, d), lambda l, j: (0, 0)),
                  pl.BlockSpec((1, d, tn), lambda l, j: (l, 0, j)),
                  pl.BlockSpec((1, 1, tn), lambda l, j: (l, 0, j))],
        out_specs=pl.BlockSpec((1, 8, tn), lambda l, j: (l, 0, j)),
        out_shape=jax.ShapeDtypeStruct((depth, 8, n), F32),
        compiler_params=_cparams("parallel", "parallel"),
        name="adaln_mod",
    )(cvec, mod_w, mod_b.reshape(depth, 1, n))


def _norm_mod_kernel(x_ref, g_ref, sh_ref, sc_ref, o_ref):
    x = x_ref[0]
    y = x * lax.rsqrt(jnp.mean(x * x, axis=-1, keepdims=True) + NORM_EPS)
    y = y * g_ref[...]
    o_ref[0] = (y * (1.0 + sc_ref[0]) + sh_ref[0]).astype(o_ref.dtype)


def norm_modulate(x, g, shift, scale, out_dtype=BF16):
    b, l, d = x.shape
    tl = min(l, 512)
    per_batch = shift.shape[0] == b and b > 1
    mod_map = (lambda i, j: (i, 0, 0)) if per_batch else (lambda i, j: (0, 0, 0))
    return pl.pallas_call(
        _norm_mod_kernel,
        grid=(b, l // tl),
        in_specs=[pl.BlockSpec((1, tl, d), lambda i, j: (i, j, 0)),
                  pl.BlockSpec((1, d), lambda i, j: (0, 0)),
                  pl.BlockSpec((1, 1, d), mod_map),
                  pl.BlockSpec((1, 1, d), mod_map)],
        out_specs=pl.BlockSpec((1, tl, d), lambda i, j: (i, j, 0)),
        out_shape=jax.ShapeDtypeStruct((b, l, d), out_dtype),
        compiler_params=_cparams("parallel", "parallel"),
        name="norm_modulate",
    )(x, g.reshape(1, d), shift.reshape(-1, 1, d), scale.reshape(-1, 1, d))


def _final_norm_kernel(x_ref, g_ref, o_ref):
    x = x_ref[...]
    y = x * lax.rsqrt(jnp.mean(x * x, axis=-1, keepdims=True) + NORM_EPS)
    o_ref[...] = y * g_ref[...]


def final_norm(x, g):
    b, l, d = x.shape
    x2 = x.reshape(b * l, d)
    tm = 512
    out = pl.pallas_call(
        _final_norm_kernel,
        grid=(b * l // tm,),
        in_specs=[pl.BlockSpec((tm, d), lambda i: (i, 0)), pl.BlockSpec((1, d), lambda i: (0, 0))],
        out_specs=pl.BlockSpec((tm, d), lambda i: (i, 0)),
        out_shape=jax.ShapeDtypeStruct((b * l, d), F32),
        compiler_params=_cparams("parallel"),
        name="final_norm",
    )(x2, g.reshape(1, d))
    return out.reshape(b, l, d)


def _inproj_kernel(h_ref, wqkv_ref, ws5_ref, why_ref, wg_ref, gb_ref, oqkv_ref, os5_ref, ohy_ref, og_ref):
    h = h_ref[...]
    oqkv_ref[...] = jnp.dot(h, wqkv_ref[...], preferred_element_type=F32)
    os5_ref[...] = jnp.dot(h, ws5_ref[...], preferred_element_type=F32)
    ohy_ref[...] = jnp.dot(h, why_ref[...], preferred_element_type=F32)
    og_ref[...] = jnp.dot(h, wg_ref[...], preferred_element_type=F32) + gb_ref[...]


def in_projection(h, w_parts, gate_b):
    m, d = h.shape
    tm = 256
    widths = [w.shape[1] for w in w_parts]
    w_specs = [pl.BlockSpec((d, n), lambda i: (0, 0)) for n in widths]
    return pl.pallas_call(
        _inproj_kernel,
        grid=(m // tm,),
        in_specs=[pl.BlockSpec((tm, d), lambda i: (i, 0))] + w_specs
                 + [pl.BlockSpec((1, widths[3]), lambda i: (0, 0))],
        out_specs=[pl.BlockSpec((tm, n), lambda i: (i, 0)) for n in widths],
        out_shape=[jax.ShapeDtypeStruct((m, n), F32) for n in widths],
        compiler_params=_cparams("parallel"),
        name="in_projection",
    )(h, *w_parts, gate_b.reshape(1, -1))


def _merge_kernel(att_ref, hy_ref, s5_ref, gl_ref, wa_ref, wh_ref, ws_ref, wo_ref, x_ref, g1_ref, o_ref):
    d = wo_ref.shape[0]
    g = jax.nn.sigmoid(gl_ref[0])
    m = g[:, :d] * jnp.dot(att_ref[0], wa_ref[...], preferred_element_type=F32)
    m = m + g[:, d:2 * d] * jnp.dot(hy_ref[0], wh_ref[...], preferred_element_type=F32)
    m = m + g[:, 2 * d:] * jnp.dot(s5_ref[0], ws_ref[...], preferred_element_type=F32)
    y = jnp.dot(m.astype(BF16), wo_ref[...], preferred_element_type=F32)
    o_ref[0] = x_ref[0] + g1_ref[0] * y


def branch_merge(att, hy, s5, gate_logits, wa, wh, ws, wo, x, g1):
    b, l, d = x.shape
    tl = min(l, 256)
    wd = att.shape[-1]
    per_batch = g1.shape[0] == b and b > 1
    g_map = (lambda i, j: (i, 0, 0)) if per_batch else (lambda i, j: (0, 0, 0))
    row = lambda n: pl.BlockSpec((1, tl, n), lambda i, j: (i, j, 0))
    full = lambda a: pl.BlockSpec(a.shape, lambda i, j: (0, 0))
    return pl.pallas_call(
        _merge_kernel,
        grid=(b, l // tl),
        in_specs=[row(wd), row(wd), row(wd), row(N_BRANCH * d), full(wa), full(wh), full(ws), full(wo),
                  row(d), pl.BlockSpec((1, 1, d), g_map)],
        out_specs=row(d),
        out_shape=jax.ShapeDtypeStruct((b, l, d), F32),
        compiler_params=_cparams("parallel", "parallel"),
        name="branch_merge",
    )(att, hy, s5, gate_logits, wa, wh, ws, wo, x, g1.reshape(-1, 1, d))


def _topk_rows(s, k):
    rows = []
    for _ in range(k):
        m = jnp.max(s, axis=0, keepdims=True)
        rows.append(m)
        s = jnp.where(s >= m, NEG_INF, s)
    return rows


def _peer_select_kernel(h_ref, wqt_ref, keys_ref, s1_ref, e1_ref, s2_ref, e2_ref, tau_ref, qt_scr, cand_scr):
    nh = s1_ref.shape[0]
    nk = PEER_KEYS
    qt_scr[...] = lax.dot_general(wqt_ref[...], h_ref[...], (((1,), (1,)), ((), ())),
                                  preferred_element_type=F32).astype(BF16)
    pairs = [(p, q) for p in range(PEER_TOPK) for q in range(PEER_TOPK) if (p + 1) * (q + 1) <= PEER_TOPK]

    def head(hh, carry):
        r0 = pl.multiple_of(hh * 2 * nk, 2 * nk)
        s1 = jnp.dot(keys_ref[2 * hh], qt_scr[pl.ds(r0, nk), :], preferred_element_type=F32)
        s2 = jnp.dot(keys_ref[2 * hh + 1], qt_scr[pl.ds(r0 + nk, nk), :], preferred_element_type=F32)
        a = _topk_rows(s1, PEER_TOPK)
        b = _topk_rows(s2, PEER_TOPK)
        cand_scr[...] = jnp.full(cand_scr.shape, NEG_INF, F32)
        for r, (p, q) in enumerate(pairs):
            cand_scr[r:r + 1, :] = a[p] + b[q]
        best = _topk_rows(cand_scr[...], PEER_TOPK)
        top = a[0] + b[0]
        z = best[0] - top
        z = jnp.exp(z)
        for r in range(1, PEER_TOPK):
            z = z + jnp.exp(best[r] - top)
        tau = best[PEER_TOPK - 1]
        s1_ref[hh] = s1
        tau_ref[hh] = tau
        e1_ref[hh] = jnp.exp(s1 - a[0])
        s2_ref[hh] = s2
        e2_ref[hh] = jnp.exp(s2 - b[0]) / z
        return carry

    lax.fori_loop(0, nh, head, 0)


def peer_select(h2, wqt, keys, tt):
    m, d = h2.shape
    nh = keys.shape[0] // 2
    out = jax.ShapeDtypeStruct((nh, PEER_KEYS, m), F32)
    ospec = pl.BlockSpec((nh, PEER_KEYS, tt), lambda i: (0, 0, i))
    tau_out = jax.ShapeDtypeStruct((nh, 1, m), F32)
    tau_spec = pl.BlockSpec((nh, 1, tt), lambda i: (0, 0, i))
    return pl.pallas_call(
        _peer_select_kernel,
        grid=(m // tt,),
        in_specs=[pl.BlockSpec((tt, d), lambda i: (i, 0)),
                  pl.BlockSpec(wqt.shape, lambda i: (0, 0)),
                  pl.BlockSpec(keys.shape, lambda i: (0, 0, 0))],
        out_specs=[ospec] * 4 + [tau_spec],
        out_shape=[out] * 4 + [tau_out],
        scratch_shapes=[pltpu.VMEM((wqt.shape[0], tt), BF16), pltpu.VMEM((56, tt), F32)],
        compiler_params=_cparams("parallel"),
        name="peer_select",
    )(h2, wqt, keys)


def _peer_expert_kernel(h_ref, u_ref, vt_ref, s1_ref, e1_ref, s2_ref, e2_ref, tau_ref, x_ref, g2_ref, o_ref,
                        acc_scr, w_scr):
    k = pl.program_id(1)
    nh = s1_ref.shape[0]
    nk = PEER_KEYS
    rows_i = u_ref.shape[0] // nk

    @pl.when(k == 0)
    def _():
        acc_scr[...] = jnp.zeros_like(acc_scr)

    def build(ii, carry):
        i = k * rows_i + ii
        w = None
        for hh in range(nh):
            s1 = s1_ref[hh, pl.ds(i, 1), :]
            e1 = e1_ref[hh, pl.ds(i, 1), :]
            t = jnp.where(s1 + s2_ref[hh] >= tau_ref[hh], e2_ref[hh] * e1, 0.0)
            w = t if w is None else w + t
        w_scr[pl.ds(pl.multiple_of(ii * nk, nk), nk), :] = w
        return carry

    lax.fori_loop(0, rows_i, build, 0)
    act = lax.dot_general(u_ref[...], h_ref[...], (((1,), (1,)), ((), ())), preferred_element_type=F32)
    wa = (w_scr[...] * _gelu_tanh(act)).astype(BF16)
    acc_scr[...] += jnp.dot(vt_ref[...], wa, preferred_element_type=F32)

    @pl.when(k == pl.num_programs(1) - 1)
    def _():
        o_ref[...] = x_ref[...] + g2_ref[0] * acc_scr[...].T


def peer_experts(h2, u_tab, vt_tab, sel, x, g2, tokens_per_batch, tt, ec):
    m, d = h2.shape
    e = u_tab.shape[0]
    nh = sel[0].shape[0]
    sspec = pl.BlockSpec((nh, PEER_KEYS, tt), lambda i, k: (0, 0, i))
    tiles_per_batch = tokens_per_batch // tt
    if g2.shape[0] > 1:
        g_map = lambda i, k: (i // tiles_per_batch, 0, 0)
    else:
        g_map = lambda i, k: (0, 0, 0)
    return pl.pallas_call(
        _peer_expert_kernel,
        grid=(m // tt, e // ec),
        in_specs=[pl.BlockSpec((tt, d), lambda i, k: (i, 0)),
                  pl.BlockSpec((ec, d), lambda i, k: (k, 0)),
                  pl.BlockSpec((d, ec), lambda i, k: (0, k)),
                  sspec, sspec, sspec, sspec,
                  pl.BlockSpec((nh, 1, tt), lambda i, k: (0, 0, i)),
                  pl.BlockSpec((tt, d), lambda i, k: (i, 0)),
                  pl.BlockSpec((1, 1, d), g_map)],
        out_specs=pl.BlockSpec((tt, d), lambda i, k: (i, 0)),
        out_shape=jax.ShapeDtypeStruct((m, d), F32),
        scratch_shapes=[pltpu.VMEM((d, tt), F32), pltpu.VMEM((ec, tt), F32)],
        compiler_params=_cparams("parallel", "arbitrary"),
        name="peer_experts",
    )(h2, u_tab, vt_tab, *sel, x, g2.reshape(-1, 1, d))


def peer_block(x, g_norm, shift, scale, gate, wqt, keys, u_tab, vt_tab):
    b, l, d = x.shape
    h2 = norm_modulate(x, g_norm, shift, scale).reshape(b * l, d)
    tt = min(l, 512)
    sel = peer_select(h2, wqt, keys, tt)
    out = peer_experts(h2, u_tab, vt_tab, sel, x.reshape(b * l, d), gate, l, tt, 1024)
    return out.reshape(b, l, d)


def axial_rope_tables(L):
    rows_n = L // GRID_W
    row = jnp.repeat(jnp.arange(rows_n), GRID_W).astype(F32)
    col = jnp.tile(jnp.arange(GRID_W), rows_n).astype(F32)
    inv = jnp.power(ROPE_BASE, -jnp.arange(ROPE_PAIRS_PER_AXIS, dtype=F32) / ROPE_PAIRS_PER_AXIS)
    ang = jnp.concatenate([row[:, None] * inv, col[:, None] * inv], axis=-1)
    return jnp.cos(ang), jnp.sin(ang)


def apply_rope(t, cos, sin):
    t1, t2 = jnp.split(t, 2, axis=-1)
    c = cos[None, :, None, :]
    s = sin[None, :, None, :]
    return jnp.concatenate([t1 * c - t2 * s, t1 * s + t2 * c], axis=-1)


def latent_window_attention(q, k, v, kc, vc, sink):
    B, L = q.shape[:2]
    C = kc.shape[1]
    nb = L // ATT_BLOCK
    scale = HEAD_DIM ** -0.5
    qb = q.reshape(B, nb, ATT_BLOCK, ATT_KV_HEADS, ATT_GROUP, HEAD_DIM)
    pad = ((0, 0), (ATT_BLOCK, ATT_BLOCK), (0, 0), (0, 0))

    def band(t):
        tp = jnp.pad(t, pad).reshape(B, nb + 2, ATT_BLOCK, ATT_KV_HEADS, HEAD_DIM)
        return jnp.concatenate([tp[:, :-2], tp[:, 1:-1], tp[:, 2:]], axis=2)

    kb = band(k)
    vb = band(v)
    s_loc = jnp.einsum('bnqhgd,bnkhd->bnhgqk', qb, kb).astype(F32) * scale
    s_ctx = jnp.einsum('bnqhgd,bchd->bnhgqc', qb, kc).astype(F32) * scale
    blk = jnp.arange(nb)[:, None, None] * ATT_BLOCK
    qpos = blk + jnp.arange(ATT_BLOCK)[None, :, None]
    kpos = blk - ATT_BLOCK + jnp.arange(3 * ATT_BLOCK)[None, None, :]
    valid = (jnp.abs(kpos - qpos) <= WINDOW) & (kpos >= 0) & (kpos < L)
    s_loc = jnp.where(valid[None, :, None, None], s_loc, NEG_INF)
    s_sink = jnp.broadcast_to(sink.astype(F32).reshape(1, 1, ATT_KV_HEADS, ATT_GROUP, 1, 1),
                              s_ctx.shape[:-1] + (1,))
    p = jax.nn.softmax(jnp.concatenate([s_sink, s_ctx, s_loc], axis=-1), axis=-1)
    p_ctx = p[..., 1:1 + C]
    p_loc = p[..., 1 + C:]
    o = (jnp.einsum('bnhgqc,bchd->bnqhgd', p_ctx, vc)
         + jnp.einsum('bnhgqk,bnkhd->bnqhgd', p_loc, vb))
    return o.reshape(B, L, ATT_WIDTH)


def context_attention(qc, kc, vc, sink):
    B, C = qc.shape[:2]
    qg = qc.reshape(B, C, ATT_KV_HEADS, ATT_GROUP, HEAD_DIM)
    s = jnp.einsum('bqhgd,bkhd->bhgqk', qg, kc).astype(F32) * (HEAD_DIM ** -0.5)
    s_sink = jnp.broadcast_to(sink.astype(F32).reshape(1, ATT_KV_HEADS, ATT_GROUP, 1, 1), s.shape[:-1] + (1,))
    p = jax.nn.softmax(jnp.concatenate([s_sink, s], axis=-1), axis=-1)[..., 1:]
    o = jnp.einsum('bhgqk,bkhd->bqhgd', p, vc)
    return o.reshape(B, C, ATT_WIDTH)


def short_conv(z, w, b):
    L = z.shape[1]
    half = HY_SHORT // 2
    zp = jnp.pad(z, ((0, 0), (half, half), (0, 0)))
    out = b
    for j in range(HY_SHORT):
        out = out + zp[:, j:j + L] * w[j]
    return out


def hyena_filters(L, lp):
    t = jnp.arange(L, dtype=F32)
    tn = t / L
    bands = jnp.arange(1, HY_BANDS + 1, dtype=F32)
    ang = 2.0 * math.pi * tn[:, None] * bands[None, :]
    z = jnp.concatenate([tn[:, None], jnp.cos(ang), jnp.sin(ang)], axis=-1)
    hdn = jnp.sin(lp['hy_freq1'] * (z @ lp['hy_w1'] + lp['hy_b1']))
    hdn = jnp.sin(lp['hy_freq2'] * (hdn @ lp['hy_w2'] + lp['hy_b2']))
    filt = hdn @ lp['hy_w3']
    rate = jnp.linspace(HY_FAST_RATE, HY_SLOW_RATE, HY_WIDTH, dtype=F32)
    tw = jnp.linspace(0.0, 1.0, L, dtype=F32)
    window = jnp.exp(-tw[:, None] * rate[None, :])
    return filt[:, :HY_WIDTH] * window, filt[:, HY_WIDTH:] * window


def bidirectional_fftconv(u, h_fwd, h_bwd):
    L = u.shape[1]
    n = 2 * L
    kern = jnp.concatenate([h_fwd, jnp.zeros_like(h_fwd[:1]), h_bwd[1:][::-1]], axis=0)
    kf = jnp.fft.rfft(kern, n=n, axis=0)
    uf = jnp.fft.rfft(u, n=n, axis=1)
    return jnp.fft.irfft(uf * kf[None], n=n, axis=1)[:, :L]


def hyena_mixer(z, lp):
    L = z.shape[1]
    z = short_conv(z, lp['hy_short_w'], lp['hy_short_b'])
    x0, x1, v = jnp.split(z, HY_N_PROJ, axis=-1)
    h_fwd, h_bwd = hyena_filters(L, lp)
    v = v * x1
    y = bidirectional_fftconv(v, h_fwd, h_bwd) + v * lp['hy_bias']
    return y * x0


def s5_discretize(lp, d):
    lam = lax.complex(jnp.minimum(lp['s5_a_re'][d], -1e-4), lp['s5_a_im'][d])
    dt = jnp.exp(lp['s5_log_dt'][d])[:, None]
    abar = jnp.exp(lam * dt)
    b = lax.complex(lp['s5_b_re'][d], lp['s5_b_im'][d])
    bbar = ((abar - 1.0) / lam)[..., None] * b
    cmat = lax.complex(lp['s5_c_re'][d], lp['s5_c_im'][d])
    return abar, bbar, cmat


def s5_drive(u, bbar):
    B, L, _ = u.shape
    ug = u.reshape(B, L, S5_GROUPS, S5_GROUP).astype(jnp.complex64)
    return jnp.einsum('blgh,gph->blgp', ug, bbar)


def s5_scan(bu, abar, h0, reverse):
    if h0 is not None:
        edge = -1 if reverse else 0
        bu = bu.at[:, edge].add(abar * h0)
    a = jnp.broadcast_to(abar, bu.shape)

    def combine(e1, e2):
        a1, b1 = e1
        a2, b2 = e2
        return a1 * a2, a2 * b1 + b2

    _, h = lax.associative_scan(combine, (a, bu), reverse=reverse, axis=1)
    return h


def s5_readout(h_fwd, h_bwd, c_fwd, c_bwd, u, lp):
    B, L, _ = u.shape
    y = (jnp.einsum('blgp,ghp->blgh', h_fwd, c_fwd)
         + jnp.einsum('blgp,ghp->blgh', h_bwd, c_bwd)).real.reshape(B, L, S5_WIDTH)
    y = y + lp['s5_d'] * u
    y = jax.nn.gelu(y)
    return y * jax.nn.sigmoid(y @ lp['s5_glu_w'] + lp['s5_glu_b'])


def s5_mixer(u, uc, lp, need_ctx):
    af, bf, cf = s5_discretize(lp, 0)
    ab, bb, cb = s5_discretize(lp, 1)
    hcf = s5_scan(s5_drive(uc, bf), af, None, False)
    hcb = s5_scan(s5_drive(uc, bb), ab, None, True)
    hf = s5_scan(s5_drive(u, bf), af, hcf[:, -1], False)
    hb = s5_scan(s5_drive(u, bb), ab, hcb[:, 0], True)
    y = s5_readout(hf, hb, cf, cb, u, lp)
    yc = s5_readout(hcf, hcb, cf, cb, uc, lp) if need_ctx else None
    return y, yc


def kernel(x, c, ctx, c_ctx, mod_w, mod_b, norm1_g, norm2_g, in_w, gate_b, attn_sink, hy_short_w, hy_short_b,
           hy_w1, hy_b1, hy_freq1, hy_w2, hy_b2, hy_freq2, hy_w3, hy_bias, s5_a_re, s5_a_im, s5_log_dt,
           s5_b_re, s5_b_im, s5_c_re, s5_c_im, s5_d, s5_glu_w, s5_glu_b, br_attn_w, br_hyena_w, br_s5_w,
           out_w, peer_wq, peer_keys, peer_u, peer_v, final_g):
    B, L, D = x.shape
    C = ctx.shape[1]
    depth = mod_w.shape[0]
    rope_cos, rope_sin = axial_rope_tables(L)

    cvec = jnp.zeros((8, D), F32).at[:B].set(c).at[B].set(c_ctx)
    mod_all = modulation(cvec, mod_w, mod_b)

    xc = ctx
    for layer in range(depth):
        need_ctx = layer < depth - 1
        lp = dict(attn_sink=attn_sink[layer],
                  hy_short_w=hy_short_w[layer], hy_short_b=hy_short_b[layer],
                  hy_w1=hy_w1[layer], hy_b1=hy_b1[layer], hy_freq1=hy_freq1[layer],
                  hy_w2=hy_w2[layer], hy_b2=hy_b2[layer], hy_freq2=hy_freq2[layer],
                  hy_w3=hy_w3[layer], hy_bias=hy_bias[layer],
                  s5_a_re=s5_a_re[layer], s5_a_im=s5_a_im[layer], s5_log_dt=s5_log_dt[layer],
                  s5_b_re=s5_b_re[layer], s5_b_im=s5_b_im[layer],
                  s5_c_re=s5_c_re[layer], s5_c_im=s5_c_im[layer], s5_d=s5_d[layer],
                  s5_glu_w=s5_glu_w[layer], s5_glu_b=s5_glu_b[layer])
        mod = mod_all[layer]
        ml = [mod[:B, i * D:(i + 1) * D] for i in range(N_MOD)]
        mc = [mod[B:B + 1, i * D:(i + 1) * D] for i in range(N_MOD)]

        w_in = in_w[layer].astype(BF16)
        w_parts = [w_in[:, COL_Q:COL_S5], w_in[:, COL_S5:COL_HY], w_in[:, COL_HY:COL_GATE], w_in[:, COL_GATE:]]
        wa = br_attn_w[layer].astype(BF16)
        wh = br_hyena_w[layer].astype(BF16)
        ws = br_s5_w[layer].astype(BF16)
        wo = out_w[layer].astype(BF16)
        wqt = peer_wq[layer].T.astype(BF16)
        keys = peer_keys[layer].reshape(2 * PEER_HEADS, PEER_KEYS, PEER_HALF).astype(BF16)
        u_tab = peer_u[layer].astype(BF16)
        vt_tab = peer_v[layer].T.astype(BF16)

        h = norm_modulate(x, norm1_g[layer], ml[0], ml[1])
        hc = norm_modulate(xc, norm1_g[layer], mc[0], mc[1])
        qkv, u_s5, z_hy, gates = in_projection(h.reshape(B * L, D), w_parts, gate_b[layer])
        qkv_c, uc_s5, zc_hy, gates_c = in_projection(hc.reshape(B * C, D), w_parts, gate_b[layer])
        qkv = qkv.reshape(B, L, -1)
        qkv_c = qkv_c.reshape(B, C, -1)
        u_s5 = u_s5.reshape(B, L, -1)
        uc_s5 = uc_s5.reshape(B, C, -1)

        q = apply_rope(qkv[..., COL_Q:COL_K].reshape(B, L, ATT_HEADS, HEAD_DIM), rope_cos, rope_sin)
        k = apply_rope(qkv[..., COL_K:COL_V].reshape(B, L, ATT_KV_HEADS, HEAD_DIM), rope_cos, rope_sin)
        v = qkv[..., COL_V:COL_S5].reshape(B, L, ATT_KV_HEADS, HEAD_DIM)
        kc = qkv_c[..., COL_K:COL_V].reshape(B, C, ATT_KV_HEADS, HEAD_DIM)
        vc = qkv_c[..., COL_V:COL_S5].reshape(B, C, ATT_KV_HEADS, HEAD_DIM)
        att = latent_window_attention(q, k, v, kc, vc, lp['attn_sink'])
        hy = hyena_mixer(z_hy.reshape(B, L, -1), lp)
        s5, s5c = s5_mixer(u_s5, uc_s5, lp, need_ctx)

        x = branch_merge(att.astype(BF16), hy.astype(BF16), s5.astype(BF16), gates.reshape(B, L, -1),
                         wa, wh, ws, wo, x, ml[2])
        x = peer_block(x, norm2_g[layer], ml[3], ml[4], ml[5], wqt, keys, u_tab, vt_tab)

        if need_ctx:
            qc = qkv_c[..., COL_Q:COL_K].reshape(B, C, ATT_HEADS, HEAD_DIM)
            att_c = context_attention(qc, kc, vc, lp['attn_sink'])
            hy_c = hyena_mixer(zc_hy.reshape(B, C, -1), lp)
            xc = branch_merge(att_c.astype(BF16), hy_c.astype(BF16), s5c.astype(BF16),
                              gates_c.reshape(B, C, -1), wa, wh, ws, wo, xc, mc[2])
            xc = peer_block(xc, norm2_g[layer], mc[3], mc[4], mc[5], wqt, keys, u_tab, vt_tab)
    return final_norm(x, final_g)
```

```python
import functools
import math

import jax
import jax.numpy as jnp
import numpy as np
from jax import lax
from jax.experimental import pallas as pl
from jax.experimental.pallas import tpu as pltpu

F32 = jnp.float32
BF16 = jnp.bfloat16

NORM_EPS = 1e-6
NEG_INF = -1e30
N_MOD = 6

GRID_W = 64
ATT_HEADS = 8
ATT_KV_HEADS = 2
ATT_GROUP = ATT_HEADS // ATT_KV_HEADS
HEAD_DIM = 64
ATT_WIDTH = ATT_HEADS * HEAD_DIM
KV_WIDTH = ATT_KV_HEADS * HEAD_DIM
WINDOW = 128
ATT_BLOCK = 128
ROPE_BASE = 10000.0
ROPE_PAIRS_PER_AXIS = HEAD_DIM // 4

HY_WIDTH = 512
HY_N_PROJ = 3
HY_SHORT = 3
HY_BANDS = 16
HY_DECAY_TARGET = 1e-2
HY_FAST_RATE = -math.log(HY_DECAY_TARGET) / 0.3
HY_SLOW_RATE = -math.log(HY_DECAY_TARGET) / 1.5

S5_WIDTH = 512
S5_GROUP = 16
S5_GROUPS = S5_WIDTH // S5_GROUP
S5_STATE = 64

N_BRANCH = 3

PEER_HEADS = 8
PEER_KEYS = 128
PEER_HALF = 128
PEER_TOPK = 16

COL_Q = 0
COL_K = COL_Q + ATT_WIDTH
COL_V = COL_K + KV_WIDTH
COL_S5 = COL_V + KV_WIDTH
COL_HY = COL_S5 + S5_WIDTH
COL_GATE = COL_HY + HY_N_PROJ * HY_WIDTH

VMEM_LIMIT_V7X = 56 * 1024 * 1024


def _cparams(*sem):
    return pltpu.CompilerParams(dimension_semantics=sem, vmem_limit_bytes=VMEM_LIMIT_V7X)


def _gelu_tanh(x):
    return 0.5 * x * (1.0 + jnp.tanh(math.sqrt(2.0 / math.pi) * (x + 0.044715 * (x * x * x))))


def _mod_kernel(s_ref, w_ref, b_ref, o_ref):
    s = s_ref[...]
    s = (s * jax.nn.sigmoid(s)).astype(BF16)
    o_ref[0] = jnp.dot(s, w_ref[0].astype(BF16), preferred_element_type=F32) + b_ref[0]


def modulation(cvec, mod_w, mod_b):
    depth, d, n = mod_w.shape
    tn = 1024
    return pl.pallas_call(
        _mod_kernel,
        grid=(depth, n // tn),
        in_specs=[pl.BlockSpec((8, d), lambda l, j: (0, 0)),
                  pl.BlockSpec((1, d, tn), lambda l, j: (l, 0, j)),
                  pl.BlockSpec((1, 1, tn), lambda l, j: (l, 0, j))],
        out_specs=pl.BlockSpec((1, 8, tn), lambda l, j: (l, 0, j)),
        out_shape=jax.ShapeDtypeStruct((depth, 8, n), F32),
        compiler_params=_cparams("parallel", "parallel"),
        name="adaln_mod",
    )(cvec, mod_w, mod_b.reshape(depth, 1, n))


def _norm_mod_kernel(x_ref, g_ref, sh_ref, sc_ref, o_ref):
    x = x_ref[0]
    y = x * lax.rsqrt(jnp.mean(x * x, axis=-1, keepdims=True) + NORM_EPS)
    y = y * g_ref[...]
    o_ref[0] = (y * (1.0 + sc_ref[0]) + sh_ref[0]).astype(o_ref.dtype)


def norm_modulate(x, g, shift, scale, out_dtype=BF16):
    b, l, d = x.shape
    tl = min(l, 512)
    per_batch = shift.shape[0] == b and b > 1
    mod_map = (lambda i, j: (i, 0, 0)) if per_batch else (lambda i, j: (0, 0, 0))
    return pl.pallas_call(
        _norm_mod_kernel,
        grid=(b, l // tl),
        in_specs=[pl.BlockSpec((1, tl, d), lambda i, j: (i, j, 0)),
                  pl.BlockSpec((1, d), lambda i, j: (0, 0)),
                  pl.BlockSpec((1, 1, d), mod_map),
                  pl.BlockSpec((1, 1, d), mod_map)],
        out_specs=pl.BlockSpec((1, tl, d), lambda i, j: (i, j, 0)),
        out_shape=jax.ShapeDtypeStruct((b, l, d), out_dtype),
        compiler_params=_cparams("parallel", "parallel"),
        name="norm_modulate",
    )(x, g.reshape(1, d), shift.reshape(-1, 1, d), scale.reshape(-1, 1, d))


def _final_norm_kernel(x_ref, g_ref, o_ref):
    x = x_ref[...]
    y = x * lax.rsqrt(jnp.mean(x * x, axis=-1, keepdims=True) + NORM_EPS)
    o_ref[...] = y * g_ref[...]


def final_norm(x, g):
    b, l, d = x.shape
    x2 = x.reshape(b * l, d)
    tm = 512
    out = pl.pallas_call(
        _final_norm_kernel,
        grid=(b * l // tm,),
        in_specs=[pl.BlockSpec((tm, d), lambda i: (i, 0)), pl.BlockSpec((1, d), lambda i: (0, 0))],
        out_specs=pl.BlockSpec((tm, d), lambda i: (i, 0)),
        out_shape=jax.ShapeDtypeStruct((b * l, d), F32),
        compiler_params=_cparams("parallel"),
        name="final_norm",
    )(x2, g.reshape(1, d))
    return out.reshape(b, l, d)


def _inproj_kernel(h_ref, wqkv_ref, ws5_ref, why_ref, wg_ref, gb_ref, oqkv_ref, os5_ref, ohy_ref, og_ref):
    h = h_ref[...]
    oqkv_ref[...] = jnp.dot(h, wqkv_ref[...], preferred_element_type=F32)
    os5_ref[...] = jnp.dot(h, ws5_ref[...], preferred_element_type=F32)
    ohy_ref[...] = jnp.dot(h, why_ref[...], preferred_element_type=F32)
    og_ref[...] = jnp.dot(h, wg_ref[...], preferred_element_type=F32) + gb_ref[...]


def in_projection(h, w_parts, gate_b):
    m, d = h.shape
    tm = 256
    widths = [w.shape[1] for w in w_parts]
    w_specs = [pl.BlockSpec((d, n), lambda i: (0, 0)) for n in widths]
    return pl.pallas_call(
        _inproj_kernel,
        grid=(m // tm,),
        in_specs=[pl.BlockSpec((tm, d), lambda i: (i, 0))] + w_specs
                 + [pl.BlockSpec((1, widths[3]), lambda i: (0, 0))],
        out_specs=[pl.BlockSpec((tm, n), lambda i: (i, 0)) for n in widths],
        out_shape=[jax.ShapeDtypeStruct((m, n), F32) for n in widths],
        compiler_params=_cparams("parallel"),
        name="in_projection",
    )(h, *w_parts, gate_b.reshape(1, -1))


def _merge_kernel(att_ref, hy_ref, s5_ref, gl_ref, wa_ref, wh_ref, ws_ref, wo_ref, x_ref, g1_ref, o_ref):
    d = wo_ref.shape[0]
    g = jax.nn.sigmoid(gl_ref[0])
    m = g[:, :d] * jnp.dot(att_ref[0], wa_ref[...], preferred_element_type=F32)
    m = m + g[:, d:2 * d] * jnp.dot(hy_ref[0], wh_ref[...], preferred_element_type=F32)
    m = m + g[:, 2 * d:] * jnp.dot(s5_ref[0], ws_ref[...], preferred_element_type=F32)
    y = jnp.dot(m.astype(BF16), wo_ref[...], preferred_element_type=F32)
    o_ref[0] = x_ref[0] + g1_ref[0] * y


def branch_merge(att, hy, s5, gate_logits, wa, wh, ws, wo, x, g1):
    b, l, d = x.shape
    tl = min(l, 256)
    wd = att.shape[-1]
    per_batch = g1.shape[0] == b and b > 1
    g_map = (lambda i, j: (i, 0, 0)) if per_batch else (lambda i, j: (0, 0, 0))
    row = lambda n: pl.BlockSpec((1, tl, n), lambda i, j: (i, j, 0))
    full = lambda a: pl.BlockSpec(a.shape, lambda i, j: (0, 0))
    return pl.pallas_call(
        _merge_kernel,
        grid=(b, l // tl),
        in_specs=[row(wd), row(wd), row(wd), row(N_BRANCH * d), full(wa), full(wh), full(ws), full(wo),
                  row(d), pl.BlockSpec((1, 1, d), g_map)],
        out_specs=row(d),
        out_shape=jax.ShapeDtypeStruct((b, l, d), F32),
        compiler_params=_cparams("parallel", "parallel"),
        name="branch_merge",
    )(att, hy, s5, gate_logits, wa, wh, ws, wo, x, g1.reshape(-1, 1, d))


def _topk_rows(s, k):
    rows = []
    for _ in range(k):
        m = jnp.max(s, axis=0, keepdims=True)
        rows.append(m)
        s = jnp.where(s >= m, NEG_INF, s)
    return rows


def _peer_select_kernel(h_ref, wqt_ref, keys_ref, s1_ref, e1_ref, s2_ref, e2_ref, tau_ref, qt_scr, cand_scr):
    nh = s1_ref.shape[0]
    nk = PEER_KEYS
    qt_scr[...] = lax.dot_general(wqt_ref[...], h_ref[...], (((1,), (1,)), ((), ())),
                                  preferred_element_type=F32).astype(BF16)
    pairs = [(p, q) for p in range(PEER_TOPK) for q in range(PEER_TOPK) if (p + 1) * (q + 1) <= PEER_TOPK]

    def head(hh, carry):
        r0 = pl.multiple_of(hh * 2 * nk, 2 * nk)
        s1 = jnp.dot(keys_ref[2 * hh], qt_scr[pl.ds(r0, nk), :], preferred_element_type=F32)
        s2 = jnp.dot(keys_ref[2 * hh + 1], qt_scr[pl.ds(r0 + nk, nk), :], preferred_element_type=F32)
        a = _topk_rows(s1, PEER_TOPK)
        b = _topk_rows(s2, PEER_TOPK)
        cand_scr[...] = jnp.full(cand_scr.shape, NEG_INF, F32)
        for r, (p, q) in enumerate(pairs):
            cand_scr[r:r + 1, :] = a[p] + b[q]
        best = _topk_rows(cand_scr[...], PEER_TOPK)
        top = a[0] + b[0]
        z = best[0] - top
        z = jnp.exp(z)
        for r in range(1, PEER_TOPK):
            z = z + jnp.exp(best[r] - top)
        tau = best[PEER_TOPK - 1]
        s1_ref[hh] = s1
        tau_ref[hh] = tau
        e1_ref[hh] = jnp.exp(s1 - a[0])
        s2_ref[hh] = s2
        e2_ref[hh] = jnp.exp(s2 - b[0]) / z
        return carry

    lax.fori_loop(0, nh, head, 0)


def peer_select(h2, wqt, keys, tt):
    m, d = h2.shape
    nh = keys.shape[0] // 2
    out = jax.ShapeDtypeStruct((nh, PEER_KEYS, m), F32)
    ospec = pl.BlockSpec((nh, PEER_KEYS, tt), lambda i: (0, 0, i))
    tau_out = jax.ShapeDtypeStruct((nh, 1, m), F32)
    tau_spec = pl.BlockSpec((nh, 1, tt), lambda i: (0, 0, i))
    return pl.pallas_call(
        _peer_select_kernel,
        grid=(m // tt,),
        in_specs=[pl.BlockSpec((tt, d), lambda i: (i, 0)),
                  pl.BlockSpec(wqt.shape, lambda i: (0, 0)),
                  pl.BlockSpec(keys.shape, lambda i: (0, 0, 0))],
        out_specs=[ospec] * 4 + [tau_spec],
        out_shape=[out] * 4 + [tau_out],
        scratch_shapes=[pltpu.VMEM((wqt.shape[0], tt), BF16), pltpu.VMEM((56, tt), F32)],
        compiler_params=_cparams("parallel"),
        name="peer_select",
    )(h2, wqt, keys)


def _peer_expert_kernel(h_ref, u_ref, vt_ref, s1_ref, e1_ref, s2_ref, e2_ref, tau_ref, x_ref, g2_ref, o_ref,
                        acc_scr, w_scr):
    k = pl.program_id(1)
    nh = s1_ref.shape[0]
    nk = PEER_KEYS
    rows_i = u_ref.shape[0] // nk

    @pl.when(k == 0)
    def _():
        acc_scr[...] = jnp.zeros_like(acc_scr)

    def build(ii, carry):
        i = k * rows_i + ii
        w = None
        for hh in range(nh):
            s1 = s1_ref[hh, pl.ds(i, 1), :]
            e1 = e1_ref[hh, pl.ds(i, 1), :]
            t = jnp.where(s1 + s2_ref[hh] >= tau_ref[hh], e2_ref[hh] * e1, 0.0)
            w = t if w is None else w + t
        w_scr[pl.ds(pl.multiple_of(ii * nk, nk), nk), :] = w
        return carry

    lax.fori_loop(0, rows_i, build, 0)
    act = lax.dot_general(u_ref[...], h_ref[...], (((1,), (1,)), ((), ())), preferred_element_type=F32)
    wa = (w_scr[...] * _gelu_tanh(act)).astype(BF16)
    acc_scr[...] += jnp.dot(vt_ref[...], wa, preferred_element_type=F32)

    @pl.when(k == pl.num_programs(1) - 1)
    def _():
        o_ref[...] = x_ref[...] + g2_ref[0] * acc_scr[...].T


def peer_experts(h2, u_tab, vt_tab, sel, x, g2, tokens_per_batch, tt, ec):
    m, d = h2.shape
    e = u_tab.shape[0]
    nh = sel[0].shape[0]
    sspec = pl.BlockSpec((nh, PEER_KEYS, tt), lambda i, k: (0, 0, i))
    tiles_per_batch = tokens_per_batch // tt
    if g2.shape[0] > 1:
        g_map = lambda i, k: (i // tiles_per_batch, 0, 0)
    else:
        g_map = lambda i, k: (0, 0, 0)
    return pl.pallas_call(
        _peer_expert_kernel,
        grid=(m // tt, e // ec),
        in_specs=[pl.BlockSpec((tt, d), lambda i, k: (i, 0)),
                  pl.BlockSpec((ec, d), lambda i, k: (k, 0)),
                  pl.BlockSpec((d, ec), lambda i, k: (0, k)),
                  sspec, sspec, sspec, sspec,
                  pl.BlockSpec((nh, 1, tt), lambda i, k: (0, 0, i)),
                  pl.BlockSpec((tt, d), lambda i, k: (i, 0)),
                  pl.BlockSpec((1, 1, d), g_map)],
        out_specs=pl.BlockSpec((tt, d), lambda i, k: (i, 0)),
        out_shape=jax.ShapeDtypeStruct((m, d), F32),
        scratch_shapes=[pltpu.VMEM((d, tt), F32), pltpu.VMEM((ec, tt), F32)],
        compiler_params=_cparams("parallel", "arbitrary"),
        name="peer_experts",
    )(h2, u_tab, vt_tab, *sel, x, g2.reshape(-1, 1, d))


def peer_block(x, g_norm, shift, scale, gate, wqt, keys, u_tab, vt_tab):
    b, l, d = x.shape
    h2 = norm_modulate(x, g_norm, shift, scale).reshape(b * l, d)
    tt = min(l, 512)
    sel = peer_select(h2, wqt, keys, tt)
    out = peer_experts(h2, u_tab, vt_tab, sel, x.reshape(b * l, d), gate, l, tt, 1024)
    return out.reshape(b, l, d)


def rope_tables(l):
    rows_n = l // GRID_W
    row = jnp.repeat(jnp.arange(rows_n), GRID_W).astype(F32)
    col = jnp.tile(jnp.arange(GRID_W), rows_n).astype(F32)
    inv = jnp.power(ROPE_BASE, -jnp.arange(ROPE_PAIRS_PER_AXIS, dtype=F32) / ROPE_PAIRS_PER_AXIS)
    ang = jnp.concatenate([row[:, None] * inv, col[:, None] * inv], axis=-1)
    c, s = jnp.cos(ang), jnp.sin(ang)
    return jnp.concatenate([c, c, c, c], axis=-1), jnp.concatenate([-s, s, -s, s], axis=-1)


def _rope_pair(x, cos2, sin2):
    half = HEAD_DIM // 2
    lane = lax.broadcasted_iota(jnp.int32, x.shape, 1)
    swapped = jnp.where(lane % HEAD_DIM < half, pltpu.roll(x, 2 * HEAD_DIM - half, axis=1),
                        pltpu.roll(x, half, axis=1))
    return x * cos2 + swapped * sin2


def _qkv_heads_kernel(qkv_ref, cos_ref, sin_ref, q_ref, k_ref, v_ref, *, rope):
    scale = HEAD_DIM ** -0.5
    x = qkv_ref[0]
    pair = 2 * HEAD_DIM
    for j in range((ATT_WIDTH + KV_WIDTH) // pair):
        blk = x[:, j * pair:(j + 1) * pair]
        if rope:
            blk = _rope_pair(blk, cos_ref[...], sin_ref[...])
        for t in range(2):
            head = blk[:, t * HEAD_DIM:(t + 1) * HEAD_DIM]
            h = 2 * j + t
            if h < ATT_HEADS:
                q_ref[0, h] = (head * scale).astype(q_ref.dtype)
            else:
                k_ref[0, h - ATT_HEADS] = head.astype(k_ref.dtype)
    for t in range(ATT_KV_HEADS):
        v_ref[0, t] = x[:, COL_V + t * HEAD_DIM:COL_V + (t + 1) * HEAD_DIM].astype(v_ref.dtype)


def qkv_heads(qkv, cos2, sin2, rope):
    b, l, w = qkv.shape
    tl = min(l, 256)
    tab = pl.BlockSpec((tl, 2 * HEAD_DIM), lambda i, j: (j, 0))
    hspec = lambda n: pl.BlockSpec((1, n, tl, HEAD_DIM), lambda i, j: (i, 0, j, 0))
    return pl.pallas_call(
        functools.partial(_qkv_heads_kernel, rope=rope),
        grid=(b, l // tl),
        in_specs=[pl.BlockSpec((1, tl, w), lambda i, j: (i, j, 0)), tab, tab],
        out_specs=[hspec(ATT_HEADS), hspec(ATT_KV_HEADS), hspec(ATT_KV_HEADS)],
        out_shape=[jax.ShapeDtypeStruct((b, ATT_HEADS, l, HEAD_DIM), BF16),
                   jax.ShapeDtypeStruct((b, ATT_KV_HEADS, l, HEAD_DIM), BF16),
                   jax.ShapeDtypeStruct((b, ATT_KV_HEADS, l, HEAD_DIM), BF16)],
        compiler_params=_cparams("parallel", "parallel"),
        name="qkv_heads",
    )(qkv, cos2, sin2)


def _attention_kernel(sink_ref, q_ref, kc_ref, vc_ref, *rest, local, seq_len):
    if local:
        kp_ref, ko_ref, kn_ref, vp_ref, vo_ref, vn_ref, o_ref = rest
    else:
        (o_ref,) = rest
    i = pl.program_id(1)
    blk = ATT_BLOCK
    n_ctx = kc_ref.shape[2]
    rows = ATT_GROUP * blk
    outs = []
    for g in range(ATT_KV_HEADS):
        q = q_ref[0, g * ATT_GROUP:(g + 1) * ATT_GROUP].reshape(rows, HEAD_DIM)
        if local:
            keys = jnp.concatenate([kc_ref[0, g], kp_ref[0, g], ko_ref[0, g], kn_ref[0, g]], axis=0)
            vals = jnp.concatenate([vc_ref[0, g], vp_ref[0, g], vo_ref[0, g], vn_ref[0, g]], axis=0)
        else:
            keys, vals = kc_ref[0, g], vc_ref[0, g]
        s = lax.dot_general(q, keys, (((1,), (1,)), ((), ())), preferred_element_type=F32)
        r = lax.broadcasted_iota(jnp.int32, s.shape, 0)
        if local:
            c = lax.broadcasted_iota(jnp.int32, s.shape, 1)
            kpos = c - n_ctx - blk
            qpos = r % blk
            kabs = kpos + i * blk
            ok = (c < n_ctx) | ((jnp.abs(kpos - qpos) <= WINDOW) & (kabs >= 0) & (kabs < seq_len))
            s = jnp.where(ok, s, NEG_INF)
        sink = jnp.zeros((rows, 1), F32)
        for t in range(ATT_GROUP):
            sink = jnp.where(r[:, :1] // blk == t, sink_ref[g * ATT_GROUP + t], sink)
        m = jnp.maximum(jnp.max(s, axis=-1, keepdims=True), sink)
        p = jnp.exp(s - m)
        denom = jnp.sum(p, axis=-1, keepdims=True) + jnp.exp(sink - m)
        o = jnp.dot(p.astype(BF16), vals, preferred_element_type=F32) / denom
        outs += [o[t * blk:(t + 1) * blk] for t in range(ATT_GROUP)]
    o_ref[0] = jnp.concatenate(outs, axis=1).astype(o_ref.dtype)


def attention(q, k, v, kc, vc, sink, local):
    b, _, l, d = q.shape
    c = kc.shape[2]
    nb = l // ATT_BLOCK
    qspec = pl.BlockSpec((1, ATT_HEADS, ATT_BLOCK, d), lambda i, j: (i, 0, j, 0))
    cspec = pl.BlockSpec((1, ATT_KV_HEADS, c, d), lambda i, j: (i, 0, 0, 0))
    in_specs = [pl.BlockSpec(memory_space=pltpu.SMEM), qspec, cspec, cspec]
    args = [sink, q, kc, vc]
    if local:
        kv = lambda f: pl.BlockSpec((1, ATT_KV_HEADS, ATT_BLOCK, d), lambda i, j: (i, 0, f(j), 0))
        band = [kv(lambda j: jnp.maximum(j - 1, 0)), kv(lambda j: j), kv(lambda j: jnp.minimum(j + 1, nb - 1))]
        in_specs += band + band
        args += [k, k, k, v, v, v]
    return pl.pallas_call(
        functools.partial(_attention_kernel, local=local, seq_len=l),
        grid=(b, nb),
        in_specs=in_specs,
        out_specs=pl.BlockSpec((1, ATT_BLOCK, ATT_HEADS * d), lambda i, j: (i, j, 0)),
        out_shape=jax.ShapeDtypeStruct((b, l, ATT_HEADS * d), BF16),
        compiler_params=_cparams("parallel", "parallel"),
        name="window_attention" if local else "context_attention",
    )(*args)


def hyena_filters(L, lp):
    t = jnp.arange(L, dtype=F32)
    tn = t / L
    bands = jnp.arange(1, HY_BANDS + 1, dtype=F32)
    ang = 2.0 * math.pi * tn[:, None] * bands[None, :]
    z = jnp.concatenate([tn[:, None], jnp.cos(ang), jnp.sin(ang)], axis=-1)
    hdn = jnp.sin(lp['hy_freq1'] * (z @ lp['hy_w1'] + lp['hy_b1']))
    hdn = jnp.sin(lp['hy_freq2'] * (hdn @ lp['hy_w2'] + lp['hy_b2']))
    filt = hdn @ lp['hy_w3']
    rate = jnp.linspace(HY_FAST_RATE, HY_SLOW_RATE, HY_WIDTH, dtype=F32)
    tw = jnp.linspace(0.0, 1.0, L, dtype=F32)
    window = jnp.exp(-tw[:, None] * rate[None, :])
    return filt[:, :HY_WIDTH] * window, filt[:, HY_WIDTH:] * window


def _short_conv_kernel(z_ref, zp_ref, zn_ref, w_ref, b_ref, x0_ref, vg_ref):
    i = pl.program_id(1)
    z = z_ref[0]
    tl, w3 = z.shape
    prev_row = jnp.where(i > 0, zp_ref[0, 7:8, :], 0.0)
    next_row = jnp.where(i < pl.num_programs(1) - 1, zn_ref[0, 0:1, :], 0.0)
    row = lax.broadcasted_iota(jnp.int32, (tl, w3), 0)
    z_prev = jnp.where(row == 0, prev_row, pltpu.roll(z, 1, axis=0))
    z_next = jnp.where(row == tl - 1, next_row, pltpu.roll(z, tl - 1, axis=0))
    y = b_ref[...] + z_prev * w_ref[0:1, :] + z * w_ref[1:2, :] + z_next * w_ref[2:3, :]
    w = w3 // HY_N_PROJ
    x0_ref[0] = y[:, :w]
    vg_ref[0] = y[:, 2 * w:] * y[:, w:2 * w]


def hyena_short_conv(z, w, b):
    bsz, l, w3 = z.shape
    tl = min(l, 256)
    nb8 = tl // 8
    last8 = l // 8 - 1
    out = jax.ShapeDtypeStruct((bsz, l, w3 // HY_N_PROJ), F32)
    ospec = pl.BlockSpec((1, tl, w3 // HY_N_PROJ), lambda i, j: (i, j, 0))
    return pl.pallas_call(
        _short_conv_kernel,
        grid=(bsz, l // tl),
        in_specs=[pl.BlockSpec((1, tl, w3), lambda i, j: (i, j, 0)),
                  pl.BlockSpec((1, 8, w3), lambda i, j: (i, jnp.maximum(j * nb8 - 1, 0), 0)),
                  pl.BlockSpec((1, 8, w3), lambda i, j: (i, jnp.minimum((j + 1) * nb8, last8), 0)),
                  pl.BlockSpec((HY_SHORT, w3), lambda i, j: (0, 0)),
                  pl.BlockSpec((1, w3), lambda i, j: (0, 0))],
        out_specs=[ospec, ospec],
        out_shape=[out, out],
        compiler_params=_cparams("parallel", "parallel"),
        name="hyena_short_conv",
    )(z, z, z, w, b.reshape(1, w3))


FFT_N2 = 128


def _dft_tables(l):
    n = 2 * l
    n1 = n // FFT_N2
    t1 = np.arange(l // FFT_N2)
    k1 = np.arange(n1)
    ang1 = 2.0 * np.pi * np.outer(k1, t1) / n1
    d_first = np.concatenate([np.cos(ang1), -np.sin(ang1)], axis=0)
    d_last = np.concatenate([np.cos(ang1).T, -np.sin(ang1).T], axis=1) / n
    t2 = np.arange(FFT_N2)
    ang2 = (2.0 * np.pi * np.outer(t2, t2) / FFT_N2)[None] + (2.0 * np.pi * np.outer(k1, t2) / n)[:, None, :]
    f_re, f_im = np.cos(ang2), -np.sin(ang2)
    fwd = np.concatenate([np.concatenate([f_re, -f_im], axis=2), np.concatenate([f_im, f_re], axis=2)], axis=1)
    g_re, g_im = np.transpose(f_re, (0, 2, 1)), -np.transpose(f_im, (0, 2, 1))
    inv = np.concatenate([np.concatenate([g_re, -g_im], axis=2), np.concatenate([g_im, g_re], axis=2)], axis=1)
    return jnp.asarray(d_first, F32), jnp.asarray(d_last, F32), jnp.asarray(fwd, BF16), jnp.asarray(inv, BF16)


def _fft_first_kernel(x_ref, d_ref, o_ref):
    o_ref[0] = jnp.dot(d_ref[...], x_ref[0], preferred_element_type=F32, precision=lax.Precision.HIGHEST)


def fft_first(x, d_first):
    s, l, w = x.shape
    t1n = l // FFT_N2
    rows = d_first.shape[0]
    lanes = FFT_N2 * w
    tn = min(lanes, 8192)
    return pl.pallas_call(
        _fft_first_kernel,
        grid=(s, lanes // tn),
        in_specs=[pl.BlockSpec((1, t1n, tn), lambda i, j: (i, 0, j)),
                  pl.BlockSpec((rows, t1n), lambda i, j: (0, 0))],
        out_specs=pl.BlockSpec((1, rows, tn), lambda i, j: (i, 0, j)),
        out_shape=jax.ShapeDtypeStruct((s, rows, lanes), F32),
        compiler_params=_cparams("parallel", "parallel"),
        name="fft_first",
    )(x.reshape(s, t1n, lanes), d_first)


def _fft_spectrum_kernel(a_ref, f_ref, o_ref):
    for s in range(a_ref.shape[0]):
        a = a_ref[s, :, 0].reshape(2 * FFT_N2, -1).astype(BF16)
        o_ref[s, 0] = jnp.dot(f_ref[0], a, preferred_element_type=F32)


def fft_spectrum(a, fwd):
    s, _, n1, n2, w = a.shape
    return pl.pallas_call(
        _fft_spectrum_kernel,
        grid=(n1,),
        in_specs=[pl.BlockSpec((s, 2, 1, n2, w), lambda k: (0, 0, k, 0, 0)),
                  pl.BlockSpec((1, 2 * n2, 2 * n2), lambda k: (k, 0, 0))],
        out_specs=pl.BlockSpec((s, 1, 2 * n2, w), lambda k: (0, k, 0, 0)),
        out_shape=jax.ShapeDtypeStruct((s, n1, 2 * n2, w), F32),
        compiler_params=_cparams("parallel"),
        name="fft_spectrum",
    )(a, fwd)


def _fft_filter_kernel(a_ref, f_ref, g_ref, kre_ref, kim_ref, o_ref):
    n2 = FFT_N2
    kre = kre_ref[0]
    kim = kim_ref[0]
    for s in range(a_ref.shape[0]):
        a = a_ref[s, :, 0].reshape(2 * n2, -1).astype(BF16)
        x = jnp.dot(f_ref[0], a, preferred_element_type=F32)
        xre, xim = x[:n2], x[n2:]
        y = jnp.concatenate([xre * kre - xim * kim, xre * kim + xim * kre], axis=0).astype(BF16)
        o_ref[s, :, 0] = jnp.dot(g_ref[0], y, preferred_element_type=F32).reshape(2, n2, -1)


def fft_filter(a, fwd, inv, kre, kim):
    s, _, n1, n2, w = a.shape
    mat = pl.BlockSpec((1, 2 * n2, 2 * n2), lambda k: (k, 0, 0))
    spec = pl.BlockSpec((1, n2, w), lambda k: (k, 0, 0))
    blk = pl.BlockSpec((s, 2, 1, n2, w), lambda k: (0, 0, k, 0, 0))
    return pl.pallas_call(
        _fft_filter_kernel,
        grid=(n1,),
        in_specs=[blk, mat, mat, spec, spec],
        out_specs=blk,
        out_shape=jax.ShapeDtypeStruct(a.shape, F32),
        compiler_params=_cparams("parallel"),
        name="fft_filter",
    )(a, fwd, inv, kre, kim)


def _fft_last_kernel(b_ref, d_ref, vg_ref, x0_ref, bias_ref, o_ref):
    y = jnp.dot(d_ref[...], b_ref[0], preferred_element_type=F32, precision=lax.Precision.HIGHEST)
    o_ref[0] = ((y + vg_ref[0] * bias_ref[...]) * x0_ref[0]).astype(o_ref.dtype)


def fft_last(bmid, d_last, vg, x0, bias):
    s, l, w = vg.shape
    t1n = l // FFT_N2
    rows = d_last.shape[1]
    lanes = FFT_N2 * w
    tn = min(lanes, 8192)
    row = pl.BlockSpec((1, t1n, tn), lambda i, j: (i, 0, j))
    out = pl.pallas_call(
        _fft_last_kernel,
        grid=(s, lanes // tn),
        in_specs=[pl.BlockSpec((1, rows, tn), lambda i, j: (i, 0, j)),
                  pl.BlockSpec((t1n, rows), lambda i, j: (0, 0)),
                  row, row, pl.BlockSpec((1, tn), lambda i, j: (0, 0))],
        out_specs=row,
        out_shape=jax.ShapeDtypeStruct((s, t1n, lanes), BF16),
        compiler_params=_cparams("parallel", "parallel"),
        name="fft_last",
    )(bmid.reshape(s, rows, lanes), d_last, vg.reshape(s, t1n, lanes), x0.reshape(s, t1n, lanes),
      jnp.tile(bias, tn // w).reshape(1, tn))
    return out.reshape(s, l, w)


def _dense_dft_tables(l):
    n = 2 * l
    ang = 2.0 * np.pi * np.outer(np.arange(n), np.arange(n)) / n
    fwd = np.concatenate([np.cos(ang), -np.sin(ang)], axis=0)
    inv = np.concatenate([np.cos(ang[:l]), -np.sin(ang[:l])], axis=1) / n
    return jnp.asarray(fwd, F32), jnp.asarray(inv, F32)


def _dense_conv_kernel(vg_ref, x0_ref, kern_ref, f_ref, g_ref, bias_ref, o_ref):
    hp = lax.Precision.HIGHEST
    l = vg_ref.shape[1]
    n = 2 * l
    vg = vg_ref[0]
    ks = jnp.dot(f_ref[...], kern_ref[...], preferred_element_type=F32, precision=hp)
    xs = jnp.dot(f_ref[:, :l], vg, preferred_element_type=F32, precision=hp)
    kre, kim, xre, xim = ks[:n], ks[n:], xs[:n], xs[n:]
    ys = jnp.concatenate([xre * kre - xim * kim, xre * kim + xim * kre], axis=0)
    y = jnp.dot(g_ref[...], ys, preferred_element_type=F32, precision=hp)
    o_ref[0] = ((y + vg * bias_ref[...]) * x0_ref[0]).astype(o_ref.dtype)


def hyena_dense_conv(vg, x0, kern, bias):
    bsz, l, w = vg.shape
    fwd, inv = _dense_dft_tables(l)
    row = pl.BlockSpec((1, l, w), lambda i: (i, 0, 0))
    full = lambda a: pl.BlockSpec(a.shape, lambda i: (0, 0))
    return pl.pallas_call(
        _dense_conv_kernel,
        grid=(bsz,),
        in_specs=[row, row, full(kern), full(fwd), full(inv), pl.BlockSpec((1, w), lambda i: (0, 0))],
        out_specs=row,
        out_shape=jax.ShapeDtypeStruct((bsz, l, w), BF16),
        compiler_params=_cparams("parallel"),
        name="hyena_dense_conv",
    )(vg, x0, kern, fwd, inv, bias.reshape(1, w))


def hyena_mixer(z, lp):
    bsz, l, _ = z.shape
    w = HY_WIDTH
    n1 = 2 * l // FFT_N2
    x0, vg = hyena_short_conv(z, lp['hy_short_w'], lp['hy_short_b'])
    h_fwd, h_bwd = hyena_filters(l, lp)
    if l // FFT_N2 < 8:
        kern = jnp.concatenate([h_fwd, jnp.zeros_like(h_fwd[:1]), h_bwd[1:][::-1]], axis=0)
        return hyena_dense_conv(vg, x0, kern, lp['hy_bias'])
    d_first, d_last, fwd, inv = _dft_tables(l)
    tail = jnp.concatenate([jnp.zeros_like(h_bwd[:1]), h_bwd[1:][::-1]], axis=0)
    ks = fft_spectrum(fft_first(jnp.stack([h_fwd, tail]), d_first).reshape(2, 2, n1, FFT_N2, w), fwd)
    sign = jnp.where(jnp.arange(n1) % 2 == 0, 1.0, -1.0)[:, None, None]
    kspec = ks[0] + sign * ks[1]
    a = fft_first(vg, d_first).reshape(bsz, 2, n1, FFT_N2, w)
    bmid = fft_filter(a, fwd, inv, kspec[:, :FFT_N2], kspec[:, FFT_N2:])
    return fft_last(bmid, d_last, vg, x0, lp['hy_bias'])


S5_CHUNK = 16
S5_CK = S5_CHUNK * S5_GROUP


def _s5_powers(lp, d):
    lam = lax.complex(jnp.minimum(lp['s5_a_re'][d], -1e-4), lp['s5_a_im'][d])
    dt = jnp.exp(lp['s5_log_dt'][d])[:, None]
    n = jnp.arange(S5_CHUNK + 1, dtype=F32)[:, None, None]
    apow = jnp.exp((lam * dt)[None] * n)
    b = lax.complex(lp['s5_b_re'][d], lp['s5_b_im'][d])
    bbar = ((apow[1] - 1.0) / lam)[..., None] * b
    cmat = lax.complex(lp['s5_c_re'][d], lp['s5_c_im'][d])
    return apow, bbar, cmat


def s5_prepare(lp):
    q = S5_CHUNK
    g_n, p_n = S5_GROUPS, S5_STATE
    hp = lax.Precision.HIGHEST
    ap_f, bb_f, c_f = _s5_powers(lp, 0)
    ap_b, bb_b, c_b = _s5_powers(lp, 1)
    k_f = jnp.einsum('gop,tgp,gpi->gtio', c_f, ap_f[:q], bb_f, precision=hp).real
    k_b = jnp.einsum('gop,tgp,gpi->gtio', c_b, ap_b[:q], bb_b, precision=hp).real
    lag = jnp.arange(q)[None, :] - jnp.arange(q)[:, None]
    m_f = jnp.where((lag >= 0)[None, :, :, None, None], k_f[:, jnp.clip(lag, 0, q - 1)], 0.0)
    m_b = jnp.where((lag <= 0)[None, :, :, None, None], k_b[:, jnp.clip(-lag, 0, q - 1)], 0.0)
    m_intra = (m_f + m_b).transpose(0, 1, 3, 2, 4).reshape(g_n, S5_CK, S5_CK)

    ws_f = ap_f[q - 1 - jnp.arange(q)][:, :, :, None] * bb_f[None]
    ws_b = ap_b[jnp.arange(q)][:, :, :, None] * bb_b[None]
    ws = jnp.stack([ws_f.real, ws_f.imag, ws_b.real, ws_b.imag], axis=0)
    ws = ws.transpose(2, 1, 4, 0, 3).reshape(g_n, S5_CK, 4, p_n)
    w_state = jnp.zeros((g_n // 2, 2, S5_CK, 4, 2, p_n), F32)
    w_state = w_state.at[:, 0, :, :, 0].set(ws[0::2]).at[:, 1, :, :, 1].set(ws[1::2])
    w_state = w_state.reshape(g_n // 2, 2, S5_CK, 8 * p_n)

    ca_f = c_f[None] * ap_f[1:q + 1][:, :, None, :]
    ca_b = c_b[None] * ap_b[q - jnp.arange(q)][:, :, None, :]
    wo = jnp.stack([ca_f.real, -ca_f.imag, ca_b.real, -ca_b.imag], axis=0)
    wo = wo.transpose(2, 0, 4, 1, 3).reshape(g_n, 4, p_n, S5_CK)
    w_out = jnp.zeros((g_n, 4, 2, p_n, S5_CK), F32)
    w_out = w_out.at[0::2, :, 0].set(wo[0::2]).at[1::2, :, 1].set(wo[1::2])
    w_out = w_out.reshape(g_n, 8 * p_n, S5_CK)

    def flat(z):
        return jnp.stack([z.real.reshape(-1), z.imag.reshape(-1)], axis=0)

    return m_intra.astype(BF16), w_state.astype(BF16), w_out.astype(BF16), flat(ap_f[q]), flat(ap_b[q])


def _s5_state_kernel(u_ref, w_ref, fre_ref, fim_ref, bre_ref, bim_ref):
    o = (jnp.dot(u_ref[0], w_ref[0, 0], preferred_element_type=F32)
         + jnp.dot(u_ref[1], w_ref[0, 1], preferred_element_type=F32))
    n = fre_ref.shape[1]
    fre_ref[...] = o[:, 0 * n:1 * n]
    fim_ref[...] = o[:, 1 * n:2 * n]
    bre_ref[...] = o[:, 2 * n:3 * n]
    bim_ref[...] = o[:, 3 * n:4 * n]


def s5_chunk_states(u_t, w_state):
    g_n, r, ck = u_t.shape
    pw = 2 * S5_STATE
    out = jax.ShapeDtypeStruct((r, g_n * S5_STATE), F32)
    ospec = pl.BlockSpec((r, pw), lambda i: (0, i))
    return pl.pallas_call(
        _s5_state_kernel,
        grid=(g_n // 2,),
        in_specs=[pl.BlockSpec((2, r, ck), lambda i: (i, 0, 0)),
                  pl.BlockSpec((1, 2, ck, 4 * pw), lambda i: (i, 0, 0, 0))],
        out_specs=[ospec] * 4,
        out_shape=[out] * 4,
        compiler_params=_cparams("parallel"),
        name="s5_chunk_states",
    )(u_t, w_state)


def _s5_carry_kernel(sre_ref, sim_ref, a_ref, hre_ref, him_ref, *, n_ctx_chunks, reverse):
    n_chunks = sre_ref.shape[0]
    ar = a_ref[0:1, :]
    ai = a_ref[1:2, :]
    zero = jnp.zeros(sre_ref.shape[1:], F32)

    def body(n, carry):
        hre, him = carry
        if reverse:
            c = jnp.where(n < n_ctx_chunks, n_ctx_chunks - 1 - n, n_chunks + n_ctx_chunks - 1 - n)
        else:
            c = n
        hre_ref[c] = hre
        him_ref[c] = him
        return ar * hre - ai * him + sre_ref[c], ar * him + ai * hre + sim_ref[c]

    lax.fori_loop(0, n_chunks, body, (zero, zero))


def s5_carry(sre, sim, a_chunk, n_ctx_chunks, rows, reverse):
    r, w = sre.shape
    nc = r // rows
    tl = 512
    spec = pl.BlockSpec((nc, rows, tl), lambda i: (0, 0, i))
    out = jax.ShapeDtypeStruct((nc, rows, w), F32)
    hre, him = pl.pallas_call(
        functools.partial(_s5_carry_kernel, n_ctx_chunks=n_ctx_chunks, reverse=reverse),
        grid=(w // tl,),
        in_specs=[spec, spec, pl.BlockSpec((2, tl), lambda i: (0, i))],
        out_specs=[spec, spec],
        out_shape=[out, out],
        compiler_params=_cparams("parallel"),
        name="s5_carry_bwd" if reverse else "s5_carry_fwd",
    )(sre.reshape(nc, rows, w), sim.reshape(nc, rows, w), a_chunk)
    return hre.reshape(r, w), him.reshape(r, w)


def _s5_output_kernel(u_ref, m_ref, fre_ref, fim_ref, bre_ref, bim_ref, wo_ref, y_ref):
    pw = fre_ref.shape[1]
    y = jnp.dot(u_ref[0], m_ref[0], preferred_element_type=F32)
    for comp, h_ref in enumerate((fre_ref, fim_ref, bre_ref, bim_ref)):
        y = y + jnp.dot(h_ref[...].astype(BF16), wo_ref[0, comp * pw:(comp + 1) * pw, :],
                        preferred_element_type=F32)
    y_ref[0] = y


def s5_chunk_outputs(u_t, m_intra, carried, w_out):
    g_n, r, ck = u_t.shape
    pw = 2 * S5_STATE
    hspec = pl.BlockSpec((r, pw), lambda i: (0, i // 2))
    return pl.pallas_call(
        _s5_output_kernel,
        grid=(g_n,),
        in_specs=[pl.BlockSpec((1, r, ck), lambda i: (i, 0, 0)),
                  pl.BlockSpec((1, ck, ck), lambda i: (i, 0, 0)),
                  hspec, hspec, hspec, hspec,
                  pl.BlockSpec((1, 4 * pw, ck), lambda i: (i, 0, 0))],
        out_specs=pl.BlockSpec((1, r, ck), lambda i: (i, 0, 0)),
        out_shape=jax.ShapeDtypeStruct((g_n, r, ck), F32),
        compiler_params=_cparams("parallel"),
        name="s5_chunk_outputs",
    )(u_t, m_intra, *carried, w_out)


def _s5_readout_kernel(y_ref, u_ref, d_ref, w_ref, b_ref, o_ref):
    y = _gelu_tanh(y_ref[...] + d_ref[...] * u_ref[...])
    z = jnp.dot(y.astype(BF16), w_ref[...], preferred_element_type=F32) + b_ref[...]
    o_ref[...] = (y * jax.nn.sigmoid(z)).astype(o_ref.dtype)


def s5_readout(y, u, d, glu_w, glu_b):
    m, w = y.shape
    tm = min(m, 512)
    row = pl.BlockSpec((tm, w), lambda i: (i, 0))
    vec = pl.BlockSpec((1, w), lambda i: (0, 0))
    return pl.pallas_call(
        _s5_readout_kernel,
        grid=(m // tm,),
        in_specs=[row, row, vec, pl.BlockSpec((w, w), lambda i: (0, 0)), vec],
        out_specs=row,
        out_shape=jax.ShapeDtypeStruct((m, w), BF16),
        compiler_params=_cparams("parallel"),
        name="s5_readout",
    )(y, u, d.reshape(1, w), glu_w, glu_b.reshape(1, w))


def s5_mixer(u, uc, lp, glu_w):
    b, l, w = u.shape
    c = uc.shape[1]
    q = S5_CHUNK
    n_all = (c + l) // q
    m_intra, w_state, w_out, a_f, a_b = s5_prepare(lp)
    u_all = jnp.concatenate([uc, u], axis=1)
    u_t = u_all.astype(BF16).reshape(b, n_all, q, S5_GROUPS, S5_GROUP)
    u_t = u_t.transpose(3, 1, 0, 2, 4).reshape(S5_GROUPS, n_all * b, S5_CK)
    fre, fim, bre, bim = s5_chunk_states(u_t, w_state)
    hf = s5_carry(fre, fim, a_f, c // q, b, False)
    hb = s5_carry(bre, bim, a_b, c // q, b, True)
    y_t = s5_chunk_outputs(u_t, m_intra, (*hf, *hb), w_out)
    y_all = y_t.reshape(S5_GROUPS, n_all, b, q, S5_GROUP).transpose(2, 1, 3, 0, 4).reshape(b * (c + l), w)
    out = s5_readout(y_all, u_all.reshape(b * (c + l), w), lp['s5_d'], glu_w, lp['s5_glu_b'])
    out = out.reshape(b, c + l, w)
    return out[:, c:], out[:, :c]


def kernel(x, c, ctx, c_ctx, mod_w, mod_b, norm1_g, norm2_g, in_w, gate_b, attn_sink, hy_short_w, hy_short_b,
           hy_w1, hy_b1, hy_freq1, hy_w2, hy_b2, hy_freq2, hy_w3, hy_bias, s5_a_re, s5_a_im, s5_log_dt,
           s5_b_re, s5_b_im, s5_c_re, s5_c_im, s5_d, s5_glu_w, s5_glu_b, br_attn_w, br_hyena_w, br_s5_w,
           out_w, peer_wq, peer_keys, peer_u, peer_v, final_g):
    B, L, D = x.shape
    C = ctx.shape[1]
    depth = mod_w.shape[0]
    rope_cos, rope_sin = rope_tables(L)

    cvec = jnp.zeros((8, D), F32).at[:B].set(c).at[B].set(c_ctx)
    mod_all = modulation(cvec, mod_w, mod_b)

    xc = ctx
    for layer in range(depth):
        need_ctx = layer < depth - 1
        lp = dict(attn_sink=attn_sink[layer],
                  hy_short_w=hy_short_w[layer], hy_short_b=hy_short_b[layer],
                  hy_w1=hy_w1[layer], hy_b1=hy_b1[layer], hy_freq1=hy_freq1[layer],
                  hy_w2=hy_w2[layer], hy_b2=hy_b2[layer], hy_freq2=hy_freq2[layer],
                  hy_w3=hy_w3[layer], hy_bias=hy_bias[layer],
                  s5_a_re=s5_a_re[layer], s5_a_im=s5_a_im[layer], s5_log_dt=s5_log_dt[layer],
                  s5_b_re=s5_b_re[layer], s5_b_im=s5_b_im[layer],
                  s5_c_re=s5_c_re[layer], s5_c_im=s5_c_im[layer], s5_d=s5_d[layer],
                  s5_glu_w=s5_glu_w[layer], s5_glu_b=s5_glu_b[layer])
        mod = mod_all[layer]
        ml = [mod[:B, i * D:(i + 1) * D] for i in range(N_MOD)]
        mc = [mod[B:B + 1, i * D:(i + 1) * D] for i in range(N_MOD)]

        w_in = in_w[layer].astype(BF16)
        w_parts = [w_in[:, COL_Q:COL_S5], w_in[:, COL_S5:COL_HY], w_in[:, COL_HY:COL_GATE], w_in[:, COL_GATE:]]
        wa = br_attn_w[layer].astype(BF16)
        wh = br_hyena_w[layer].astype(BF16)
        ws = br_s5_w[layer].astype(BF16)
        wo = out_w[layer].astype(BF16)
        wqt = peer_wq[layer].T.astype(BF16)
        keys = peer_keys[layer].reshape(2 * PEER_HEADS, PEER_KEYS, PEER_HALF).astype(BF16)
        u_tab = peer_u[layer].astype(BF16)
        vt_tab = peer_v[layer].T.astype(BF16)

        h = norm_modulate(x, norm1_g[layer], ml[0], ml[1])
        hc = norm_modulate(xc, norm1_g[layer], mc[0], mc[1])
        qkv, u_s5, z_hy, gates = in_projection(h.reshape(B * L, D), w_parts, gate_b[layer])
        qkv_c, uc_s5, zc_hy, gates_c = in_projection(hc.reshape(B * C, D), w_parts, gate_b[layer])
        qkv = qkv.reshape(B, L, -1)
        qkv_c = qkv_c.reshape(B, C, -1)
        u_s5 = u_s5.reshape(B, L, -1)
        uc_s5 = uc_s5.reshape(B, C, -1)

        q, k, v = qkv_heads(qkv, rope_cos, rope_sin, True)
        qc, kc, vc = qkv_heads(qkv_c, rope_cos, rope_sin, False)
        att = attention(q, k, v, kc, vc, lp['attn_sink'], True)
        hy = hyena_mixer(z_hy.reshape(B, L, -1), lp)
        s5, s5c = s5_mixer(u_s5, uc_s5, lp, s5_glu_w[layer].astype(BF16))

        x = branch_merge(att, hy, s5, gates.reshape(B, L, -1), wa, wh, ws, wo, x, ml[2])
        x = peer_block(x, norm2_g[layer], ml[3], ml[4], ml[5], wqt, keys, u_tab, vt_tab)

        if need_ctx:
            att_c = attention(qc, None, None, kc, vc, lp['attn_sink'], False)
            hy_c = hyena_mixer(zc_hy.reshape(B, C, -1), lp)
            xc = branch_merge(att_c, hy_c, s5c, gates_c.reshape(B, C, -1), wa, wh, ws, wo, xc, mc[2])
            xc = peer_block(xc, norm2_g[layer], mc[3], mc[4], mc[5], wqt, keys, u_tab, vt_tab)
    return final_norm(x, final_g)
```

```python
import functools
import math

import jax
import jax.numpy as jnp
import numpy as np
from jax import lax
from jax.experimental import pallas as pl
from jax.experimental.pallas import tpu as pltpu

F32 = jnp.float32
BF16 = jnp.bfloat16

NORM_EPS = 1e-6
NEG_INF = -1e30
N_MOD = 6

GRID_W = 64
ATT_HEADS = 8
ATT_KV_HEADS = 2
ATT_GROUP = ATT_HEADS // ATT_KV_HEADS
HEAD_DIM = 64
ATT_WIDTH = ATT_HEADS * HEAD_DIM
KV_WIDTH = ATT_KV_HEADS * HEAD_DIM
WINDOW = 128
ATT_BLOCK = 128
ROPE_BASE = 10000.0
ROPE_PAIRS_PER_AXIS = HEAD_DIM // 4

HY_WIDTH = 512
HY_N_PROJ = 3
HY_SHORT = 3
HY_BANDS = 16
HY_DECAY_TARGET = 1e-2
HY_FAST_RATE = -math.log(HY_DECAY_TARGET) / 0.3
HY_SLOW_RATE = -math.log(HY_DECAY_TARGET) / 1.5

S5_WIDTH = 512
S5_GROUP = 16
S5_GROUPS = S5_WIDTH // S5_GROUP
S5_STATE = 64

N_BRANCH = 3

PEER_HEADS = 8
PEER_KEYS = 128
PEER_HALF = 128
PEER_TOPK = 16

COL_Q = 0
COL_K = COL_Q + ATT_WIDTH
COL_V = COL_K + KV_WIDTH
COL_S5 = COL_V + KV_WIDTH
COL_HY = COL_S5 + S5_WIDTH
COL_GATE = COL_HY + HY_N_PROJ * HY_WIDTH

VMEM_LIMIT_V7X = 56 * 1024 * 1024


def _cparams(*sem):
    return pltpu.CompilerParams(dimension_semantics=sem, vmem_limit_bytes=VMEM_LIMIT_V7X)


def _gelu_tanh(x):
    return 0.5 * x * (1.0 + jnp.tanh(math.sqrt(2.0 / math.pi) * (x + 0.044715 * (x * x * x))))


def _mod_kernel(s_ref, w_ref, b_ref, o_ref):
    s = s_ref[...]
    s = (s * jax.nn.sigmoid(s)).astype(BF16)
    o_ref[0] = jnp.dot(s, w_ref[0].astype(BF16), preferred_element_type=F32) + b_ref[0]


def modulation(cvec, mod_w, mod_b):
    depth, d, n = mod_w.shape
    tn = 1024
    return pl.pallas_call(
        _mod_kernel,
        grid=(depth, n // tn),
        in_specs=[pl.BlockSpec((8, d), lambda l, j: (0, 0)),
                  pl.BlockSpec((1, d, tn), lambda l, j: (l, 0, j)),
                  pl.BlockSpec((1, 1, tn), lambda l, j: (l, 0, j))],
        out_specs=pl.BlockSpec((1, 8, tn), lambda l, j: (l, 0, j)),
        out_shape=jax.ShapeDtypeStruct((depth, 8, n), F32),
        compiler_params=_cparams("parallel", "parallel"),
        name="adaln_mod",
    )(cvec, mod_w, mod_b.reshape(depth, 1, n))


def _norm_mod_kernel(x_ref, g_ref, sh_ref, sc_ref, o_ref, *, transposed):
    x = x_ref[0]
    y = x * lax.rsqrt(jnp.mean(x * x, axis=-1, keepdims=True) + NORM_EPS)
    y = y * g_ref[...]
    y = y * (1.0 + sc_ref[0]) + sh_ref[0]
    if transposed:
        o_ref[...] = y.T.astype(o_ref.dtype)
    else:
        o_ref[0] = y.astype(o_ref.dtype)


def norm_modulate(x, g, shift, scale, transposed=False):
    b, l, d = x.shape
    tl = min(l, 512)
    nt = l // tl
    per_batch = shift.shape[0] == b and b > 1
    mod_map = (lambda i, j: (i, 0, 0)) if per_batch else (lambda i, j: (0, 0, 0))
    if transposed:
        out_spec = pl.BlockSpec((d, tl), lambda i, j: (0, i * nt + j))
        out_shape = jax.ShapeDtypeStruct((d, b * l), BF16)
    else:
        out_spec = pl.BlockSpec((1, tl, d), lambda i, j: (i, j, 0))
        out_shape = jax.ShapeDtypeStruct((b, l, d), BF16)
    return pl.pallas_call(
        functools.partial(_norm_mod_kernel, transposed=transposed),
        grid=(b, nt),
        in_specs=[pl.BlockSpec((1, tl, d), lambda i, j: (i, j, 0)),
                  pl.BlockSpec((1, d), lambda i, j: (0, 0)),
                  pl.BlockSpec((1, 1, d), mod_map),
                  pl.BlockSpec((1, 1, d), mod_map)],
        out_specs=out_spec,
        out_shape=out_shape,
        compiler_params=_cparams("parallel", "parallel"),
        name="norm_modulate_t" if transposed else "norm_modulate",
    )(x, g.reshape(1, d), shift.reshape(-1, 1, d), scale.reshape(-1, 1, d))


def _final_norm_kernel(x_ref, g_ref, o_ref):
    x = x_ref[...]
    y = x * lax.rsqrt(jnp.mean(x * x, axis=-1, keepdims=True) + NORM_EPS)
    o_ref[...] = y * g_ref[...]


def final_norm(x, g):
    b, l, d = x.shape
    x2 = x.reshape(b * l, d)
    tm = 512
    out = pl.pallas_call(
        _final_norm_kernel,
        grid=(b * l // tm,),
        in_specs=[pl.BlockSpec((tm, d), lambda i: (i, 0)), pl.BlockSpec((1, d), lambda i: (0, 0))],
        out_specs=pl.BlockSpec((tm, d), lambda i: (i, 0)),
        out_shape=jax.ShapeDtypeStruct((b * l, d), F32),
        compiler_params=_cparams("parallel"),
        name="final_norm",
    )(x2, g.reshape(1, d))
    return out.reshape(b, l, d)


def _inproj_kernel(h_ref, wqkv_ref, ws5_ref, why_ref, wg_ref, gb_ref, oqkv_ref, os5_ref, ohy_ref, og_ref):
    h = h_ref[...]
    oqkv_ref[...] = jnp.dot(h, wqkv_ref[...], preferred_element_type=F32)
    os5_ref[...] = jnp.dot(h, ws5_ref[...], preferred_element_type=F32)
    ohy_ref[...] = jnp.dot(h, why_ref[...], preferred_element_type=F32)
    og_ref[...] = jnp.dot(h, wg_ref[...], preferred_element_type=F32) + gb_ref[...]


def in_projection(h, w_parts, gate_b):
    m, d = h.shape
    tm = 256
    widths = [w.shape[1] for w in w_parts]
    w_specs = [pl.BlockSpec((d, n), lambda i: (0, 0)) for n in widths]
    return pl.pallas_call(
        _inproj_kernel,
        grid=(m // tm,),
        in_specs=[pl.BlockSpec((tm, d), lambda i: (i, 0))] + w_specs
                 + [pl.BlockSpec((1, widths[3]), lambda i: (0, 0))],
        out_specs=[pl.BlockSpec((tm, n), lambda i: (i, 0)) for n in widths],
        out_shape=[jax.ShapeDtypeStruct((m, n), F32) for n in widths],
        compiler_params=_cparams("parallel"),
        name="in_projection",
    )(h, *w_parts, gate_b.reshape(1, -1))


def _merge_kernel(att_ref, hy_ref, s5_ref, gl_ref, wa_ref, wh_ref, ws_ref, wo_ref, x_ref, g1_ref, o_ref):
    d = wo_ref.shape[0]
    g = jax.nn.sigmoid(gl_ref[0])
    m = g[:, :d] * jnp.dot(att_ref[0], wa_ref[...], preferred_element_type=F32)
    m = m + g[:, d:2 * d] * jnp.dot(hy_ref[0], wh_ref[...], preferred_element_type=F32)
    m = m + g[:, 2 * d:] * jnp.dot(s5_ref[0], ws_ref[...], preferred_element_type=F32)
    y = jnp.dot(m.astype(BF16), wo_ref[...], preferred_element_type=F32)
    o_ref[0] = x_ref[0] + g1_ref[0] * y


def branch_merge(att, hy, s5, gate_logits, wa, wh, ws, wo, x, g1):
    b, l, d = x.shape
    tl = min(l, 256)
    wd = att.shape[-1]
    per_batch = g1.shape[0] == b and b > 1
    g_map = (lambda i, j: (i, 0, 0)) if per_batch else (lambda i, j: (0, 0, 0))
    row = lambda n: pl.BlockSpec((1, tl, n), lambda i, j: (i, j, 0))
    full = lambda a: pl.BlockSpec(a.shape, lambda i, j: (0, 0))
    return pl.pallas_call(
        _merge_kernel,
        grid=(b, l // tl),
        in_specs=[row(wd), row(wd), row(wd), row(N_BRANCH * d), full(wa), full(wh), full(ws), full(wo),
                  row(d), pl.BlockSpec((1, 1, d), g_map)],
        out_specs=row(d),
        out_shape=jax.ShapeDtypeStruct((b, l, d), F32),
        compiler_params=_cparams("parallel", "parallel"),
        name="branch_merge",
    )(att, hy, s5, gate_logits, wa, wh, ws, wo, x, g1.reshape(-1, 1, d))


def _topk_rows(s, k):
    rows = []
    for _ in range(k):
        m = jnp.max(s, axis=0, keepdims=True)
        rows.append(m)
        s = jnp.where(s >= m, NEG_INF, s)
    return rows


LANES = 128


def _store_key_rows(ref, hh, v):
    for tb in range(v.shape[1] // LANES):
        ref[hh, tb] = v[:, tb * LANES:(tb + 1) * LANES]


def _peer_select_kernel(ht_ref, wqt_ref, keys_ref, rank_ref, cnt_ref, e1_ref, e2_ref, qt_scr, cand_scr):
    nh = rank_ref.shape[0]
    nk = PEER_KEYS
    qt_scr[...] = jnp.dot(wqt_ref[...], ht_ref[...], preferred_element_type=F32).astype(BF16)
    pairs = [(p, q) for p in range(PEER_TOPK) for q in range(PEER_TOPK) if (p + 1) * (q + 1) <= PEER_TOPK]

    def head(hh, carry):
        r0 = pl.multiple_of(hh * 2 * nk, 2 * nk)
        s1 = jnp.dot(keys_ref[2 * hh], qt_scr[pl.ds(r0, nk), :], preferred_element_type=F32)
        s2 = jnp.dot(keys_ref[2 * hh + 1], qt_scr[pl.ds(r0 + nk, nk), :], preferred_element_type=F32)
        a = _topk_rows(s1, PEER_TOPK)
        b = []
        rest = s2
        rank = jnp.full(s2.shape, float(PEER_TOPK), F32)
        for r in range(PEER_TOPK):
            m = jnp.max(rest, axis=0, keepdims=True)
            b.append(m)
            hit = rest >= m
            rank = jnp.where(hit, float(r), rank)
            rest = jnp.where(hit, NEG_INF, rest)
        cand_scr[...] = jnp.full(cand_scr.shape, NEG_INF, F32)
        for r, (p, q) in enumerate(pairs):
            cand_scr[r:r + 1, :] = a[p] + b[q]
        best = _topk_rows(cand_scr[...], PEER_TOPK)
        top = a[0] + b[0]
        z = jnp.exp(best[0] - top)
        for r in range(1, PEER_TOPK):
            z = z + jnp.exp(best[r] - top)
        tau = best[PEER_TOPK - 1]
        cnt = jnp.zeros(s1.shape, F32)
        for q in range(PEER_TOPK):
            cnt = cnt + jnp.where(s1 + b[q] >= tau, 1.0, 0.0)
        rank_ref[hh] = rank.astype(BF16)
        _store_key_rows(cnt_ref, hh, cnt)
        _store_key_rows(e1_ref, hh, 0.5 * jnp.exp(s1 - a[0]))
        e2_ref[hh] = (jnp.exp(s2 - b[0]) / z).astype(BF16)
        return carry

    lax.fori_loop(0, nh, head, 0)


def peer_select(ht, wqt, keys, tt):
    d, m = ht.shape
    nh = keys.shape[0] // 2
    out = jax.ShapeDtypeStruct((nh, PEER_KEYS, m), BF16)
    out32 = jax.ShapeDtypeStruct((nh, m // LANES, PEER_KEYS, LANES), F32)
    ospec = pl.BlockSpec((nh, PEER_KEYS, tt), lambda i: (0, 0, i))
    ospec32 = pl.BlockSpec((nh, tt // LANES, PEER_KEYS, LANES), lambda i: (0, i, 0, 0))
    return pl.pallas_call(
        _peer_select_kernel,
        grid=(m // tt,),
        in_specs=[pl.BlockSpec((d, tt), lambda i: (0, i)),
                  pl.BlockSpec(wqt.shape, lambda i: (0, 0)),
                  pl.BlockSpec(keys.shape, lambda i: (0, 0, 0))],
        out_specs=[ospec, ospec32, ospec32, ospec],
        out_shape=[out, out32, out32, out],
        scratch_shapes=[pltpu.VMEM((wqt.shape[0], tt), BF16), pltpu.VMEM((56, tt), F32)],
        compiler_params=_cparams("parallel"),
        name="peer_select",
    )(ht, wqt, keys)


PEER_PAIR = 2 * PEER_KEYS


def _half_gate_gelu(half_w, x):
    c = math.sqrt(2.0 / math.pi)
    p = half_w * x
    return p + p * jnp.tanh(x * (c + (c * 0.044715) * (x * x)))


def _key_row(ref, hh, ii):
    blocks = [jnp.broadcast_to(ref[hh, tb, ii:ii + 1, :], (PEER_KEYS, LANES)) for tb in range(ref.shape[1])]
    return jnp.concatenate(blocks, axis=1).astype(BF16)


def _peer_expert_kernel(ht_ref, u_ref, vt_ref, rank_ref, cnt_ref, e1_ref, e2_ref, x_ref, g2_ref, o_ref,
                        acc_scr, wa_scr):
    k = pl.program_id(1)
    nh = rank_ref.shape[0]

    @pl.when(k == 0)
    def _():
        acc_scr[...] = jnp.zeros_like(acc_scr)

    ht = ht_ref[...]
    part = None
    for p in range(u_ref.shape[0] // PEER_PAIR):
        cols = slice(p * PEER_PAIR, (p + 1) * PEER_PAIR)
        act = jnp.dot(u_ref[cols, :], ht, preferred_element_type=F32)
        w_rows = []
        for r in range(2):
            ii = 2 * p + r
            w = None
            for hh in range(nh):
                selected = rank_ref[hh] < _key_row(cnt_ref, hh, ii)
                gate = jnp.where(selected, e2_ref[hh], jnp.zeros((), BF16)) * _key_row(e1_ref, hh, ii)
                w = gate if w is None else w + gate
            w_rows.append(w)
        half_w = jnp.concatenate(w_rows, axis=0).astype(F32)
        wa_scr[p] = _half_gate_gelu(half_w, act).astype(BF16)
        d = jnp.dot(vt_ref[:, cols], wa_scr[p], preferred_element_type=F32)
        part = d if part is None else part + d
    acc_scr[...] += part

    @pl.when(k == pl.num_programs(1) - 1)
    def _():
        o_ref[...] = x_ref[...] + g2_ref[0] * acc_scr[...].T


def peer_experts(ht, u_tab, vt_tab, sel, x, g2, tokens_per_batch, tt, ec):
    d, m = ht.shape
    e = u_tab.shape[0]
    nh = sel[0].shape[0]
    rows_i = ec // PEER_KEYS
    full_keys = pl.BlockSpec((nh, PEER_KEYS, tt), lambda i, k: (0, 0, i))
    chunk_keys = pl.BlockSpec((nh, tt // LANES, rows_i, LANES), lambda i, k: (0, i, k, 0))
    tiles_per_batch = tokens_per_batch // tt
    if g2.shape[0] > 1:
        g_map = lambda i, k: (i // tiles_per_batch, 0, 0)
    else:
        g_map = lambda i, k: (0, 0, 0)
    rank, cnt, e1, e2 = sel
    return pl.pallas_call(
        _peer_expert_kernel,
        grid=(m // tt, e // ec),
        in_specs=[pl.BlockSpec((d, tt), lambda i, k: (0, i)),
                  pl.BlockSpec((ec, d), lambda i, k: (k, 0)),
                  pl.BlockSpec((d, ec), lambda i, k: (0, k)),
                  full_keys, chunk_keys, chunk_keys, full_keys,
                  pl.BlockSpec((tt, d), lambda i, k: (i, 0)),
                  pl.BlockSpec((1, 1, d), g_map)],
        out_specs=pl.BlockSpec((tt, d), lambda i, k: (i, 0)),
        out_shape=jax.ShapeDtypeStruct((m, d), F32),
        scratch_shapes=[pltpu.VMEM((d, tt), F32), pltpu.VMEM((ec // PEER_PAIR, PEER_PAIR, tt), BF16)],
        compiler_params=_cparams("parallel", "arbitrary"),
        name="peer_experts",
    )(ht, u_tab, vt_tab, rank, cnt, e1, e2, x, g2.reshape(-1, 1, d))


def peer_block(x, g_norm, shift, scale, gate, wqt, keys, u_tab, vt_tab):
    b, l, d = x.shape
    ht = norm_modulate(x, g_norm, shift, scale, transposed=True)
    tt = min(l, 512)
    sel = peer_select(ht, wqt, keys, tt)
    out = peer_experts(ht, u_tab, vt_tab, sel, x.reshape(b * l, d), gate, l, tt, 2048)
    return out.reshape(b, l, d)


def rope_tables(l):
    rows_n = l // GRID_W
    row = jnp.repeat(jnp.arange(rows_n), GRID_W).astype(F32)
    col = jnp.tile(jnp.arange(GRID_W), rows_n).astype(F32)
    inv = jnp.power(ROPE_BASE, -jnp.arange(ROPE_PAIRS_PER_AXIS, dtype=F32) / ROPE_PAIRS_PER_AXIS)
    ang = jnp.concatenate([row[:, None] * inv, col[:, None] * inv], axis=-1)
    c, s = jnp.cos(ang), jnp.sin(ang)
    return jnp.concatenate([c, c, c, c], axis=-1), jnp.concatenate([-s, s, -s, s], axis=-1)


def _rope_pair(x, cos2, sin2):
    half = HEAD_DIM // 2
    lane = lax.broadcasted_iota(jnp.int32, x.shape, 1)
    swapped = jnp.where(lane % HEAD_DIM < half, pltpu.roll(x, 2 * HEAD_DIM - half, axis=1),
                        pltpu.roll(x, half, axis=1))
    return x * cos2 + swapped * sin2


def _qkv_heads_kernel(qkv_ref, cos_ref, sin_ref, q_ref, k_ref, v_ref, *, rope):
    scale = HEAD_DIM ** -0.5
    x = qkv_ref[0]
    pair = 2 * HEAD_DIM
    for j in range((ATT_WIDTH + KV_WIDTH) // pair):
        blk = x[:, j * pair:(j + 1) * pair]
        if rope:
            blk = _rope_pair(blk, cos_ref[...], sin_ref[...])
        for t in range(2):
            head = blk[:, t * HEAD_DIM:(t + 1) * HEAD_DIM]
            h = 2 * j + t
            if h < ATT_HEADS:
                q_ref[0, h] = (head * scale).astype(q_ref.dtype)
            else:
                k_ref[0, h - ATT_HEADS] = head.astype(k_ref.dtype)
    for t in range(ATT_KV_HEADS):
        v_ref[0, t] = x[:, COL_V + t * HEAD_DIM:COL_V + (t + 1) * HEAD_DIM].astype(v_ref.dtype)


def qkv_heads(qkv, cos2, sin2, rope):
    b, l, w = qkv.shape
    tl = min(l, 256)
    tab = pl.BlockSpec((tl, 2 * HEAD_DIM), lambda i, j: (j, 0))
    hspec = lambda n: pl.BlockSpec((1, n, tl, HEAD_DIM), lambda i, j: (i, 0, j, 0))
    return pl.pallas_call(
        functools.partial(_qkv_heads_kernel, rope=rope),
        grid=(b, l // tl),
        in_specs=[pl.BlockSpec((1, tl, w), lambda i, j: (i, j, 0)), tab, tab],
        out_specs=[hspec(ATT_HEADS), hspec(ATT_KV_HEADS), hspec(ATT_KV_HEADS)],
        out_shape=[jax.ShapeDtypeStruct((b, ATT_HEADS, l, HEAD_DIM), BF16),
                   jax.ShapeDtypeStruct((b, ATT_KV_HEADS, l, HEAD_DIM), BF16),
                   jax.ShapeDtypeStruct((b, ATT_KV_HEADS, l, HEAD_DIM), BF16)],
        compiler_params=_cparams("parallel", "parallel"),
        name="qkv_heads",
    )(qkv, cos2, sin2)


def _attention_kernel(sink_ref, q_ref, kc_ref, vc_ref, *rest, local, seq_len):
    if local:
        kp_ref, ko_ref, kn_ref, vp_ref, vo_ref, vn_ref, o_ref = rest
    else:
        (o_ref,) = rest
    i = pl.program_id(1)
    blk = ATT_BLOCK
    n_ctx = kc_ref.shape[2]
    rows = ATT_GROUP * blk
    outs = []
    for g in range(ATT_KV_HEADS):
        q = q_ref[0, g * ATT_GROUP:(g + 1) * ATT_GROUP].reshape(rows, HEAD_DIM)
        if local:
            keys = jnp.concatenate([kc_ref[0, g], kp_ref[0, g], ko_ref[0, g], kn_ref[0, g]], axis=0)
            vals = jnp.concatenate([vc_ref[0, g], vp_ref[0, g], vo_ref[0, g], vn_ref[0, g]], axis=0)
        else:
            keys, vals = kc_ref[0, g], vc_ref[0, g]
        s = lax.dot_general(q, keys, (((1,), (1,)), ((), ())), preferred_element_type=F32)
        r = lax.broadcasted_iota(jnp.int32, s.shape, 0)
        if local:
            c = lax.broadcasted_iota(jnp.int32, s.shape, 1)
            kpos = c - n_ctx - blk
            qpos = r % blk
            kabs = kpos + i * blk
            ok = (c < n_ctx) | ((jnp.abs(kpos - qpos) <= WINDOW) & (kabs >= 0) & (kabs < seq_len))
            s = jnp.where(ok, s, NEG_INF)
        sink = jnp.zeros((rows, 1), F32)
        for t in range(ATT_GROUP):
            sink = jnp.where(r[:, :1] // blk == t, sink_ref[g * ATT_GROUP + t], sink)
        m = jnp.maximum(jnp.max(s, axis=-1, keepdims=True), sink)
        p = jnp.exp(s - m)
        denom = jnp.sum(p, axis=-1, keepdims=True) + jnp.exp(sink - m)
        o = jnp.dot(p.astype(BF16), vals, preferred_element_type=F32) / denom
        outs += [o[t * blk:(t + 1) * blk] for t in range(ATT_GROUP)]
    o_ref[0] = jnp.concatenate(outs, axis=1).astype(o_ref.dtype)


def attention(q, k, v, kc, vc, sink, local):
    b, _, l, d = q.shape
    c = kc.shape[2]
    nb = l // ATT_BLOCK
    qspec = pl.BlockSpec((1, ATT_HEADS, ATT_BLOCK, d), lambda i, j: (i, 0, j, 0))
    cspec = pl.BlockSpec((1, ATT_KV_HEADS, c, d), lambda i, j: (i, 0, 0, 0))
    in_specs = [pl.BlockSpec(memory_space=pltpu.SMEM), qspec, cspec, cspec]
    args = [sink, q, kc, vc]
    if local:
        kv = lambda f: pl.BlockSpec((1, ATT_KV_HEADS, ATT_BLOCK, d), lambda i, j: (i, 0, f(j), 0))
        band = [kv(lambda j: jnp.maximum(j - 1, 0)), kv(lambda j: j), kv(lambda j: jnp.minimum(j + 1, nb - 1))]
        in_specs += band + band
        args += [k, k, k, v, v, v]
    return pl.pallas_call(
        functools.partial(_attention_kernel, local=local, seq_len=l),
        grid=(b, nb),
        in_specs=in_specs,
        out_specs=pl.BlockSpec((1, ATT_BLOCK, ATT_HEADS * d), lambda i, j: (i, j, 0)),
        out_shape=jax.ShapeDtypeStruct((b, l, ATT_HEADS * d), BF16),
        compiler_params=_cparams("parallel", "parallel"),
        name="window_attention" if local else "context_attention",
    )(*args)


def hyena_filters(L, lp):
    t = jnp.arange(L, dtype=F32)
    tn = t / L
    bands = jnp.arange(1, HY_BANDS + 1, dtype=F32)
    ang = 2.0 * math.pi * tn[:, None] * bands[None, :]
    z = jnp.concatenate([tn[:, None], jnp.cos(ang), jnp.sin(ang)], axis=-1)
    hdn = jnp.sin(lp['hy_freq1'] * (z @ lp['hy_w1'] + lp['hy_b1']))
    hdn = jnp.sin(lp['hy_freq2'] * (hdn @ lp['hy_w2'] + lp['hy_b2']))
    filt = hdn @ lp['hy_w3']
    rate = jnp.linspace(HY_FAST_RATE, HY_SLOW_RATE, HY_WIDTH, dtype=F32)
    tw = jnp.linspace(0.0, 1.0, L, dtype=F32)
    window = jnp.exp(-tw[:, None] * rate[None, :])
    return filt[:, :HY_WIDTH] * window, filt[:, HY_WIDTH:] * window


def _short_conv_kernel(z_ref, zp_ref, zn_ref, w_ref, b_ref, x0_ref, vg_ref, vgb_ref):
    i = pl.program_id(1)
    z = z_ref[0]
    tl, w3 = z.shape
    prev_row = jnp.where(i > 0, zp_ref[0, 7:8, :], 0.0)
    next_row = jnp.where(i < pl.num_programs(1) - 1, zn_ref[0, 0:1, :], 0.0)
    row = lax.broadcasted_iota(jnp.int32, (tl, w3), 0)
    z_prev = jnp.where(row == 0, prev_row, pltpu.roll(z, 1, axis=0))
    z_next = jnp.where(row == tl - 1, next_row, pltpu.roll(z, tl - 1, axis=0))
    y = b_ref[...] + z_prev * w_ref[0:1, :] + z * w_ref[1:2, :] + z_next * w_ref[2:3, :]
    w = w3 // HY_N_PROJ
    x0_ref[0] = y[:, :w]
    vg = y[:, 2 * w:] * y[:, w:2 * w]
    vg_ref[0] = vg
    vgb_ref[0] = vg.astype(BF16)


def hyena_short_conv(z, w, b):
    bsz, l, w3 = z.shape
    tl = min(l, 256)
    nb8 = tl // 8
    last8 = l // 8 - 1
    out = jax.ShapeDtypeStruct((bsz, l, w3 // HY_N_PROJ), F32)
    ospec = pl.BlockSpec((1, tl, w3 // HY_N_PROJ), lambda i, j: (i, j, 0))
    return pl.pallas_call(
        _short_conv_kernel,
        grid=(bsz, l // tl),
        in_specs=[pl.BlockSpec((1, tl, w3), lambda i, j: (i, j, 0)),
                  pl.BlockSpec((1, 8, w3), lambda i, j: (i, jnp.maximum(j * nb8 - 1, 0), 0)),
                  pl.BlockSpec((1, 8, w3), lambda i, j: (i, jnp.minimum((j + 1) * nb8, last8), 0)),
                  pl.BlockSpec((HY_SHORT, w3), lambda i, j: (0, 0)),
                  pl.BlockSpec((1, w3), lambda i, j: (0, 0))],
        out_specs=[ospec, ospec, ospec],
        out_shape=[out, out, jax.ShapeDtypeStruct(out.shape, BF16)],
        compiler_params=_cparams("parallel", "parallel"),
        name="hyena_short_conv",
    )(z, z, z, w, b.reshape(1, w3))


def dft_tables(l):
    n = 2 * l
    split = 64
    t = jnp.arange(l, dtype=jnp.int32)[None, :]
    kh = jnp.arange(l // split, dtype=jnp.int32)[:, None]
    kl = jnp.arange(split, dtype=jnp.int32)[:, None]
    ang_h = (2.0 * math.pi / n) * (((kh * split) * t) % n).astype(F32)
    ang_l = (2.0 * math.pi / n) * ((kl * t) % n).astype(F32)
    ch, sh = jnp.cos(ang_h)[:, None, :], jnp.sin(ang_h)[:, None, :]
    cl, sl = jnp.cos(ang_l)[None, :, :], jnp.sin(ang_l)[None, :, :]
    cos_t = (ch * cl - sh * sl).reshape(l, l).astype(BF16)
    sin_t = (sh * cl + ch * sl).reshape(l, l).astype(BF16)
    return cos_t, sin_t


def _filter_spectrum_kernel(c_ref, s_ref, hp_ref, hm_ref, kre_ref, kim_ref):
    kre_ref[...] = jnp.dot(c_ref[...], hp_ref[...], preferred_element_type=F32)
    kim_ref[...] = -jnp.dot(s_ref[...], hm_ref[...], preferred_element_type=F32)


def filter_spectrum(cos_t, sin_t, h_fwd, h_bwd):
    l, w = h_fwd.shape
    hb0 = h_bwd.at[0].set(0.0)
    tk = 512
    tab = pl.BlockSpec((tk, l), lambda k: (k, 0))
    full = pl.BlockSpec((l, w), lambda k: (0, 0))
    ospec = pl.BlockSpec((tk, w), lambda k: (k, 0))
    out = jax.ShapeDtypeStruct((l, w), F32)
    kre, kim = pl.pallas_call(
        _filter_spectrum_kernel,
        grid=(l // tk,),
        in_specs=[tab, tab, full, full],
        out_specs=[ospec, ospec],
        out_shape=[out, out],
        compiler_params=_cparams("parallel"),
        name="hyena_filter_spectrum",
    )(cos_t, sin_t, (h_fwd + hb0).astype(BF16), (h_fwd - hb0).astype(BF16))
    sign = jnp.where(jnp.arange(l) % 2 == 0, 1.0, -1.0)[:, None]
    k_nyq = jnp.sum(sign * (h_fwd + hb0), axis=0, keepdims=True)
    return kre, kim, k_nyq


def _long_conv_kernel(x_ref, crow_ref, srow_ref, ccol_ref, scol_ref, kre_ref, kim_ref, knyq_ref, y_ref,
                      acc_scr, xn_scr):
    k = pl.program_id(1)
    nk = pl.num_programs(1)
    x = x_ref[0]
    l = x.shape[0]
    tk = crow_ref.shape[0]

    @pl.when(k == 0)
    def _():
        acc_scr[...] = jnp.zeros_like(acc_scr)
        t_idx = lax.broadcasted_iota(jnp.int32, (8, l), 1)
        alt = jnp.where(t_idx % 2 == 0, 1.0, -1.0).astype(BF16)
        xn_scr[...] = jnp.dot(alt, x, preferred_element_type=F32)

    xre = jnp.dot(crow_ref[...], x, preferred_element_type=F32)
    xim = -jnp.dot(srow_ref[...], x, preferred_element_type=F32)
    kre, kim = kre_ref[...], kim_ref[...]
    k_idx = lax.broadcasted_iota(jnp.int32, (tk, 1), 0) + k * tk
    weight = jnp.where(k_idx == 0, 1.0, 2.0)
    yre = ((xre * kre - xim * kim) * weight).astype(BF16)
    nyim = ((xre * kim + xim * kre) * -weight).astype(BF16)
    acc_scr[...] += (jnp.dot(ccol_ref[...], yre, preferred_element_type=F32)
                     + jnp.dot(scol_ref[...], nyim, preferred_element_type=F32))

    @pl.when(k == nk - 1)
    def _():
        t_idx = lax.broadcasted_iota(jnp.int32, (l, 1), 0)
        nyq = xn_scr[0:1, :] * knyq_ref[...]
        y_ref[0] = (acc_scr[...] + jnp.where(t_idx % 2 == 0, nyq, -nyq)) * (0.5 / l)


def long_conv(x, cos_t, sin_t, kre, kim, k_nyq):
    b, l, w = x.shape
    tk = 256
    row_tab = pl.BlockSpec((tk, l), lambda i, k: (k, 0))
    col_tab = pl.BlockSpec((l, tk), lambda i, k: (0, k))
    kspec = pl.BlockSpec((tk, w), lambda i, k: (k, 0))
    return pl.pallas_call(
        _long_conv_kernel,
        grid=(b, l // tk),
        in_specs=[pl.BlockSpec((1, l, w), lambda i, k: (i, 0, 0)), row_tab, row_tab, col_tab, col_tab,
                  kspec, kspec, pl.BlockSpec((1, w), lambda i, k: (0, 0))],
        out_specs=pl.BlockSpec((1, l, w), lambda i, k: (i, 0, 0)),
        out_shape=jax.ShapeDtypeStruct((b, l, w), F32),
        scratch_shapes=[pltpu.VMEM((l, w), F32), pltpu.VMEM((8, w), F32)],
        compiler_params=_cparams("parallel", "arbitrary"),
        name="hyena_long_conv",
    )(x, cos_t, sin_t, cos_t, sin_t, kre, kim, k_nyq)


def _hyena_gate_kernel(y_ref, vg_ref, x0_ref, bias_ref, o_ref):
    o_ref[...] = ((y_ref[...] + vg_ref[...] * bias_ref[...]) * x0_ref[...]).astype(o_ref.dtype)


def hyena_gate(y, vg, x0, bias):
    b, l, w = y.shape
    m = b * l
    tm = 1024
    row = pl.BlockSpec((tm, w), lambda i: (i, 0))
    out = pl.pallas_call(
        _hyena_gate_kernel,
        grid=(m // tm,),
        in_specs=[row, row, row, pl.BlockSpec((1, w), lambda i: (0, 0))],
        out_specs=row,
        out_shape=jax.ShapeDtypeStruct((m, w), BF16),
        compiler_params=_cparams("parallel"),
        name="hyena_gate",
    )(y.reshape(m, w), vg.reshape(m, w), x0.reshape(m, w), bias.reshape(1, w))
    return out.reshape(b, l, w)


def _dense_dft_tables(l):
    n = 2 * l
    ang = 2.0 * np.pi * np.outer(np.arange(n), np.arange(n)) / n
    fwd = np.concatenate([np.cos(ang), -np.sin(ang)], axis=0)
    inv = np.concatenate([np.cos(ang[:l]), -np.sin(ang[:l])], axis=1) / n
    return jnp.asarray(fwd, F32), jnp.asarray(inv, F32)


def _dense_conv_kernel(vg_ref, x0_ref, kern_ref, f_ref, g_ref, bias_ref, o_ref):
    hp = lax.Precision.HIGHEST
    l = vg_ref.shape[1]
    n = 2 * l
    vg = vg_ref[0]
    ks = jnp.dot(f_ref[...], kern_ref[...], preferred_element_type=F32, precision=hp)
    xs = jnp.dot(f_ref[:, :l], vg, preferred_element_type=F32, precision=hp)
    kre, kim, xre, xim = ks[:n], ks[n:], xs[:n], xs[n:]
    ys = jnp.concatenate([xre * kre - xim * kim, xre * kim + xim * kre], axis=0)
    y = jnp.dot(g_ref[...], ys, preferred_element_type=F32, precision=hp)
    o_ref[0] = ((y + vg * bias_ref[...]) * x0_ref[0]).astype(o_ref.dtype)


def hyena_dense_conv(vg, x0, kern, bias):
    bsz, l, w = vg.shape
    fwd, inv = _dense_dft_tables(l)
    row = pl.BlockSpec((1, l, w), lambda i: (i, 0, 0))
    full = lambda a: pl.BlockSpec(a.shape, lambda i: (0, 0))
    return pl.pallas_call(
        _dense_conv_kernel,
        grid=(bsz,),
        in_specs=[row, row, full(kern), full(fwd), full(inv), pl.BlockSpec((1, w), lambda i: (0, 0))],
        out_specs=row,
        out_shape=jax.ShapeDtypeStruct((bsz, l, w), BF16),
        compiler_params=_cparams("parallel"),
        name="hyena_dense_conv",
    )(vg, x0, kern, fwd, inv, bias.reshape(1, w))


HY_DENSE_MAX_LEN = 512


def hyena_mixer(z, lp, tables=None):
    l = z.shape[1]
    x0, vg, vg_bf16 = hyena_short_conv(z, lp['hy_short_w'], lp['hy_short_b'])
    h_fwd, h_bwd = hyena_filters(l, lp)
    if l <= HY_DENSE_MAX_LEN:
        kern = jnp.concatenate([h_fwd, jnp.zeros_like(h_fwd[:1]), h_bwd[1:][::-1]], axis=0)
        return hyena_dense_conv(vg, x0, kern, lp['hy_bias'])
    cos_t, sin_t = tables if tables is not None else dft_tables(l)
    kre, kim, k_nyq = filter_spectrum(cos_t, sin_t, h_fwd, h_bwd)
    y = long_conv(vg_bf16, cos_t, sin_t, kre, kim, k_nyq)
    return hyena_gate(y, vg, x0, lp['hy_bias'])


S5_CHUNK = 16
S5_CK = S5_CHUNK * S5_GROUP


def _s5_powers(lp, d):
    lam = lax.complex(jnp.minimum(lp['s5_a_re'][d], -1e-4), lp['s5_a_im'][d])
    dt = jnp.exp(lp['s5_log_dt'][d])[:, None]
    n = jnp.arange(S5_CHUNK + 1, dtype=F32)[:, None, None]
    apow = jnp.exp((lam * dt)[None] * n)
    b = lax.complex(lp['s5_b_re'][d], lp['s5_b_im'][d])
    bbar = ((apow[1] - 1.0) / lam)[..., None] * b
    cmat = lax.complex(lp['s5_c_re'][d], lp['s5_c_im'][d])
    return apow, bbar, cmat


def s5_prepare(lp):
    q = S5_CHUNK
    g_n, p_n = S5_GROUPS, S5_STATE
    hp = lax.Precision.HIGHEST
    ap_f, bb_f, c_f = _s5_powers(lp, 0)
    ap_b, bb_b, c_b = _s5_powers(lp, 1)
    k_f = jnp.einsum('gop,tgp,gpi->gtio', c_f, ap_f[:q], bb_f, precision=hp).real
    k_b = jnp.einsum('gop,tgp,gpi->gtio', c_b, ap_b[:q], bb_b, precision=hp).real
    lag = jnp.arange(q)[None, :] - jnp.arange(q)[:, None]
    m_f = jnp.where((lag >= 0)[None, :, :, None, None], k_f[:, jnp.clip(lag, 0, q - 1)], 0.0)
    m_b = jnp.where((lag <= 0)[None, :, :, None, None], k_b[:, jnp.clip(-lag, 0, q - 1)], 0.0)
    m_intra = (m_f + m_b).transpose(0, 1, 3, 2, 4).reshape(g_n, S5_CK, S5_CK)

    ws_f = ap_f[q - 1 - jnp.arange(q)][:, :, :, None] * bb_f[None]
    ws_b = ap_b[jnp.arange(q)][:, :, :, None] * bb_b[None]
    ws = jnp.stack([ws_f.real, ws_f.imag, ws_b.real, ws_b.imag], axis=0)
    ws = ws.transpose(2, 1, 4, 0, 3).reshape(g_n, S5_CK, 4, p_n)
    w_state = jnp.zeros((g_n // 2, 2, S5_CK, 4, 2, p_n), F32)
    w_state = w_state.at[:, 0, :, :, 0].set(ws[0::2]).at[:, 1, :, :, 1].set(ws[1::2])
    w_state = w_state.reshape(g_n // 2, 2, S5_CK, 8 * p_n)

    ca_f = c_f[None] * ap_f[1:q + 1][:, :, None, :]
    ca_b = c_b[None] * ap_b[q - jnp.arange(q)][:, :, None, :]
    wo = jnp.stack([ca_f.real, -ca_f.imag, ca_b.real, -ca_b.imag], axis=0)
    wo = wo.transpose(2, 0, 4, 1, 3).reshape(g_n, 4, p_n, S5_CK)
    w_out = jnp.zeros((g_n, 4, 2, p_n, S5_CK), F32)
    w_out = w_out.at[0::2, :, 0].set(wo[0::2]).at[1::2, :, 1].set(wo[1::2])
    w_out = w_out.reshape(g_n, 8 * p_n, S5_CK)

    def flat(z):
        return jnp.stack([z.real.reshape(-1), z.imag.reshape(-1)], axis=0)

    return m_intra.astype(BF16), w_state.astype(BF16), w_out.astype(BF16), flat(ap_f[q]), flat(ap_b[q])


def _s5_state_kernel(u_ref, w_ref, fre_ref, fim_ref, bre_ref, bim_ref):
    o = (jnp.dot(u_ref[0], w_ref[0, 0], preferred_element_type=F32)
         + jnp.dot(u_ref[1], w_ref[0, 1], preferred_element_type=F32))
    n = fre_ref.shape[1]
    fre_ref[...] = o[:, 0 * n:1 * n]
    fim_ref[...] = o[:, 1 * n:2 * n]
    bre_ref[...] = o[:, 2 * n:3 * n]
    bim_ref[...] = o[:, 3 * n:4 * n]


def s5_chunk_states(u_t, w_state):
    g_n, r, ck = u_t.shape
    pw = 2 * S5_STATE
    out = jax.ShapeDtypeStruct((r, g_n * S5_STATE), F32)
    ospec = pl.BlockSpec((r, pw), lambda i: (0, i))
    return pl.pallas_call(
        _s5_state_kernel,
        grid=(g_n // 2,),
        in_specs=[pl.BlockSpec((2, r, ck), lambda i: (i, 0, 0)),
                  pl.BlockSpec((1, 2, ck, 4 * pw), lambda i: (i, 0, 0, 0))],
        out_specs=[ospec] * 4,
        out_shape=[out] * 4,
        compiler_params=_cparams("parallel"),
        name="s5_chunk_states",
    )(u_t, w_state)


def _s5_carry_kernel(sre_ref, sim_ref, a_ref, hre_ref, him_ref, *, n_ctx_chunks, reverse):
    n_chunks = sre_ref.shape[0]
    ar = a_ref[0:1, :]
    ai = a_ref[1:2, :]
    zero = jnp.zeros(sre_ref.shape[1:], F32)

    def body(n, carry):
        hre, him = carry
        if reverse:
            c = jnp.where(n < n_ctx_chunks, n_ctx_chunks - 1 - n, n_chunks + n_ctx_chunks - 1 - n)
        else:
            c = n
        hre_ref[c] = hre
        him_ref[c] = him
        return ar * hre - ai * him + sre_ref[c], ar * him + ai * hre + sim_ref[c]

    lax.fori_loop(0, n_chunks, body, (zero, zero))


def s5_carry(sre, sim, a_chunk, n_ctx_chunks, rows, reverse):
    r, w = sre.shape
    nc = r // rows
    tl = 512
    spec = pl.BlockSpec((nc, rows, tl), lambda i: (0, 0, i))
    out = jax.ShapeDtypeStruct((nc, rows, w), F32)
    hre, him = pl.pallas_call(
        functools.partial(_s5_carry_kernel, n_ctx_chunks=n_ctx_chunks, reverse=reverse),
        grid=(w // tl,),
        in_specs=[spec, spec, pl.BlockSpec((2, tl), lambda i: (0, i))],
        out_specs=[spec, spec],
        out_shape=[out, out],
        compiler_params=_cparams("parallel"),
        name="s5_carry_bwd" if reverse else "s5_carry_fwd",
    )(sre.reshape(nc, rows, w), sim.reshape(nc, rows, w), a_chunk)
    return hre.reshape(r, w), him.reshape(r, w)


def _s5_output_kernel(u_ref, m_ref, fre_ref, fim_ref, bre_ref, bim_ref, wo_ref, y_ref):
    pw = fre_ref.shape[1]
    y = jnp.dot(u_ref[0], m_ref[0], preferred_element_type=F32)
    for comp, h_ref in enumerate((fre_ref, fim_ref, bre_ref, bim_ref)):
        y = y + jnp.dot(h_ref[...].astype(BF16), wo_ref[0, comp * pw:(comp + 1) * pw, :],
                        preferred_element_type=F32)
    y_ref[0] = y


def s5_chunk_outputs(u_t, m_intra, carried, w_out):
    g_n, r, ck = u_t.shape
    pw = 2 * S5_STATE
    hspec = pl.BlockSpec((r, pw), lambda i: (0, i // 2))
    return pl.pallas_call(
        _s5_output_kernel,
        grid=(g_n,),
        in_specs=[pl.BlockSpec((1, r, ck), lambda i: (i, 0, 0)),
                  pl.BlockSpec((1, ck, ck), lambda i: (i, 0, 0)),
                  hspec, hspec, hspec, hspec,
                  pl.BlockSpec((1, 4 * pw, ck), lambda i: (i, 0, 0))],
        out_specs=pl.BlockSpec((1, r, ck), lambda i: (i, 0, 0)),
        out_shape=jax.ShapeDtypeStruct((g_n, r, ck), F32),
        compiler_params=_cparams("parallel"),
        name="s5_chunk_outputs",
    )(u_t, m_intra, *carried, w_out)


def _s5_readout_kernel(y_ref, u_ref, d_ref, w_ref, b_ref, o_ref):
    y = _gelu_tanh(y_ref[...] + d_ref[...] * u_ref[...])
    z = jnp.dot(y.astype(BF16), w_ref[...], preferred_element_type=F32) + b_ref[...]
    o_ref[...] = (y * jax.nn.sigmoid(z)).astype(o_ref.dtype)


def s5_readout(y, u, d, glu_w, glu_b):
    m, w = y.shape
    tm = min(m, 512)
    row = pl.BlockSpec((tm, w), lambda i: (i, 0))
    vec = pl.BlockSpec((1, w), lambda i: (0, 0))
    return pl.pallas_call(
        _s5_readout_kernel,
        grid=(m // tm,),
        in_specs=[row, row, vec, pl.BlockSpec((w, w), lambda i: (0, 0)), vec],
        out_specs=row,
        out_shape=jax.ShapeDtypeStruct((m, w), BF16),
        compiler_params=_cparams("parallel"),
        name="s5_readout",
    )(y, u, d.reshape(1, w), glu_w, glu_b.reshape(1, w))


def s5_mixer(u, uc, lp, glu_w):
    b, l, w = u.shape
    c = uc.shape[1]
    q = S5_CHUNK
    n_all = (c + l) // q
    m_intra, w_state, w_out, a_f, a_b = s5_prepare(lp)
    u_all = jnp.concatenate([uc, u], axis=1)
    u_t = u_all.astype(BF16).reshape(b, n_all, q, S5_GROUPS, S5_GROUP)
    u_t = u_t.transpose(3, 1, 0, 2, 4).reshape(S5_GROUPS, n_all * b, S5_CK)
    fre, fim, bre, bim = s5_chunk_states(u_t, w_state)
    hf = s5_carry(fre, fim, a_f, c // q, b, False)
    hb = s5_carry(bre, bim, a_b, c // q, b, True)
    y_t = s5_chunk_outputs(u_t, m_intra, (*hf, *hb), w_out)
    y_all = y_t.reshape(S5_GROUPS, n_all, b, q, S5_GROUP).transpose(2, 1, 3, 0, 4).reshape(b * (c + l), w)
    out = s5_readout(y_all, u_all.reshape(b * (c + l), w), lp['s5_d'], glu_w, lp['s5_glu_b'])
    out = out.reshape(b, c + l, w)
    return out[:, c:], out[:, :c]


def kernel(x, c, ctx, c_ctx, mod_w, mod_b, norm1_g, norm2_g, in_w, gate_b, attn_sink, hy_short_w, hy_short_b,
           hy_w1, hy_b1, hy_freq1, hy_w2, hy_b2, hy_freq2, hy_w3, hy_bias, s5_a_re, s5_a_im, s5_log_dt,
           s5_b_re, s5_b_im, s5_c_re, s5_c_im, s5_d, s5_glu_w, s5_glu_b, br_attn_w, br_hyena_w, br_s5_w,
           out_w, peer_wq, peer_keys, peer_u, peer_v, final_g):
    B, L, D = x.shape
    C = ctx.shape[1]
    depth = mod_w.shape[0]
    rope_cos, rope_sin = rope_tables(L)
    conv_tables = dft_tables(L) if L > HY_DENSE_MAX_LEN else None

    cvec = jnp.zeros((8, D), F32).at[:B].set(c).at[B].set(c_ctx)
    mod_all = modulation(cvec, mod_w, mod_b)

    xc = ctx
    for layer in range(depth):
        need_ctx = layer < depth - 1
        lp = dict(attn_sink=attn_sink[layer],
                  hy_short_w=hy_short_w[layer], hy_short_b=hy_short_b[layer],
                  hy_w1=hy_w1[layer], hy_b1=hy_b1[layer], hy_freq1=hy_freq1[layer],
                  hy_w2=hy_w2[layer], hy_b2=hy_b2[layer], hy_freq2=hy_freq2[layer],
                  hy_w3=hy_w3[layer], hy_bias=hy_bias[layer],
                  s5_a_re=s5_a_re[layer], s5_a_im=s5_a_im[layer], s5_log_dt=s5_log_dt[layer],
                  s5_b_re=s5_b_re[layer], s5_b_im=s5_b_im[layer],
                  s5_c_re=s5_c_re[layer], s5_c_im=s5_c_im[layer], s5_d=s5_d[layer],
                  s5_glu_w=s5_glu_w[layer], s5_glu_b=s5_glu_b[layer])
        mod = mod_all[layer]
        ml = [mod[:B, i * D:(i + 1) * D] for i in range(N_MOD)]
        mc = [mod[B:B + 1, i * D:(i + 1) * D] for i in range(N_MOD)]

        w_in = in_w[layer].astype(BF16)
        w_parts = [w_in[:, COL_Q:COL_S5], w_in[:, COL_S5:COL_HY], w_in[:, COL_HY:COL_GATE], w_in[:, COL_GATE:]]
        wa = br_attn_w[layer].astype(BF16)
        wh = br_hyena_w[layer].astype(BF16)
        ws = br_s5_w[layer].astype(BF16)
        wo = out_w[layer].astype(BF16)
        wqt = peer_wq[layer].T.astype(BF16)
        keys = peer_keys[layer].reshape(2 * PEER_HEADS, PEER_KEYS, PEER_HALF).astype(BF16)
        u_tab = peer_u[layer].astype(BF16)
        vt_tab = peer_v[layer].T.astype(BF16)

        h = norm_modulate(x, norm1_g[layer], ml[0], ml[1])
        hc = norm_modulate(xc, norm1_g[layer], mc[0], mc[1])
        qkv, u_s5, z_hy, gates = in_projection(h.reshape(B * L, D), w_parts, gate_b[layer])
        qkv_c, uc_s5, zc_hy, gates_c = in_projection(hc.reshape(B * C, D), w_parts, gate_b[layer])
        qkv = qkv.reshape(B, L, -1)
        qkv_c = qkv_c.reshape(B, C, -1)
        u_s5 = u_s5.reshape(B, L, -1)
        uc_s5 = uc_s5.reshape(B, C, -1)

        q, k, v = qkv_heads(qkv, rope_cos, rope_sin, True)
        qc, kc, vc = qkv_heads(qkv_c, rope_cos, rope_sin, False)
        att = attention(q, k, v, kc, vc, lp['attn_sink'], True)
        hy = hyena_mixer(z_hy.reshape(B, L, -1), lp, conv_tables)
        s5, s5c = s5_mixer(u_s5, uc_s5, lp, s5_glu_w[layer].astype(BF16))

        x = branch_merge(att, hy, s5, gates.reshape(B, L, -1), wa, wh, ws, wo, x, ml[2])
        x = peer_block(x, norm2_g[layer], ml[3], ml[4], ml[5], wqt, keys, u_tab, vt_tab)

        if need_ctx:
            att_c = attention(qc, None, None, kc, vc, lp['attn_sink'], False)
            hy_c = hyena_mixer(zc_hy.reshape(B, C, -1), lp)
            xc = branch_merge(att_c, hy_c, s5c, gates_c.reshape(B, C, -1), wa, wh, ws, wo, xc, mc[2])
            xc = peer_block(xc, norm2_g[layer], mc[3], mc[4], mc[5], wqt, keys, u_tab, vt_tab)
    return final_norm(x, final_g)
```

```python
import functools
import math

import jax
import jax.numpy as jnp
import numpy as np
from jax import lax
from jax.experimental import pallas as pl
from jax.experimental.pallas import tpu as pltpu

F32 = jnp.float32
BF16 = jnp.bfloat16

NORM_EPS = 1e-6
NEG_INF = -1e30
N_MOD = 6

GRID_W = 64
ATT_HEADS = 8
ATT_KV_HEADS = 2
ATT_GROUP = ATT_HEADS // ATT_KV_HEADS
HEAD_DIM = 64
ATT_WIDTH = ATT_HEADS * HEAD_DIM
KV_WIDTH = ATT_KV_HEADS * HEAD_DIM
WINDOW = 128
ATT_BLOCK = 128
ROPE_BASE = 10000.0
ROPE_PAIRS_PER_AXIS = HEAD_DIM // 4

HY_WIDTH = 512
HY_N_PROJ = 3
HY_SHORT = 3
HY_BANDS = 16
HY_DECAY_TARGET = 1e-2
HY_FAST_RATE = -math.log(HY_DECAY_TARGET) / 0.3
HY_SLOW_RATE = -math.log(HY_DECAY_TARGET) / 1.5

S5_WIDTH = 512
S5_GROUP = 16
S5_GROUPS = S5_WIDTH // S5_GROUP
S5_STATE = 64

N_BRANCH = 3

PEER_HEADS = 8
PEER_KEYS = 128
PEER_HALF = 128
PEER_TOPK = 16

COL_Q = 0
COL_K = COL_Q + ATT_WIDTH
COL_V = COL_K + KV_WIDTH
COL_S5 = COL_V + KV_WIDTH
COL_HY = COL_S5 + S5_WIDTH
COL_GATE = COL_HY + HY_N_PROJ * HY_WIDTH

VMEM_LIMIT_V7X = 56 * 1024 * 1024


def _cparams(*sem):
    return pltpu.CompilerParams(dimension_semantics=sem, vmem_limit_bytes=VMEM_LIMIT_V7X)


def _gelu_tanh(x):
    return 0.5 * x * (1.0 + jnp.tanh(math.sqrt(2.0 / math.pi) * (x + 0.044715 * (x * x * x))))


def _mod_kernel(s_ref, w_ref, b_ref, o_ref):
    s = s_ref[...]
    s = (s * jax.nn.sigmoid(s)).astype(BF16)
    o_ref[0] = jnp.dot(s, w_ref[0].astype(BF16), preferred_element_type=F32) + b_ref[0]


def modulation(cvec, mod_w, mod_b):
    depth, d, n = mod_w.shape
    tn = 1024
    return pl.pallas_call(
        _mod_kernel,
        grid=(depth, n // tn),
        in_specs=[pl.BlockSpec((8, d), lambda l, j: (0, 0)),
                  pl.BlockSpec((1, d, tn), lambda l, j: (l, 0, j)),
                  pl.BlockSpec((1, 1, tn), lambda l, j: (l, 0, j))],
        out_specs=pl.BlockSpec((1, 8, tn), lambda l, j: (l, 0, j)),
        out_shape=jax.ShapeDtypeStruct((depth, 8, n), F32),
        compiler_params=_cparams("parallel", "parallel"),
        name="adaln_mod",
    )(cvec, mod_w, mod_b.reshape(depth, 1, n))


def _norm_mod_kernel(x_ref, g_ref, sh_ref, sc_ref, o_ref, *, transposed):
    x = x_ref[0]
    y = x * lax.rsqrt(jnp.mean(x * x, axis=-1, keepdims=True) + NORM_EPS)
    y = y * g_ref[...]
    y = y * (1.0 + sc_ref[0]) + sh_ref[0]
    if transposed:
        o_ref[...] = y.T.astype(o_ref.dtype)
    else:
        o_ref[0] = y.astype(o_ref.dtype)


def norm_modulate(x, g, shift, scale, transposed=False):
    b, l, d = x.shape
    tl = min(l, 512)
    nt = l // tl
    per_batch = shift.shape[0] == b and b > 1
    mod_map = (lambda i, j: (i, 0, 0)) if per_batch else (lambda i, j: (0, 0, 0))
    if transposed:
        out_spec = pl.BlockSpec((d, tl), lambda i, j: (0, i * nt + j))
        out_shape = jax.ShapeDtypeStruct((d, b * l), BF16)
    else:
        out_spec = pl.BlockSpec((1, tl, d), lambda i, j: (i, j, 0))
        out_shape = jax.ShapeDtypeStruct((b, l, d), BF16)
    return pl.pallas_call(
        functools.partial(_norm_mod_kernel, transposed=transposed),
        grid=(b, nt),
        in_specs=[pl.BlockSpec((1, tl, d), lambda i, j: (i, j, 0)),
                  pl.BlockSpec((1, d), lambda i, j: (0, 0)),
                  pl.BlockSpec((1, 1, d), mod_map),
                  pl.BlockSpec((1, 1, d), mod_map)],
        out_specs=out_spec,
        out_shape=out_shape,
        compiler_params=_cparams("parallel", "parallel"),
        name="norm_modulate_t" if transposed else "norm_modulate",
    )(x, g.reshape(1, d), shift.reshape(-1, 1, d), scale.reshape(-1, 1, d))


def _final_norm_kernel(x_ref, g_ref, o_ref):
    x = x_ref[...]
    y = x * lax.rsqrt(jnp.mean(x * x, axis=-1, keepdims=True) + NORM_EPS)
    o_ref[...] = y * g_ref[...]


def final_norm(x, g):
    b, l, d = x.shape
    x2 = x.reshape(b * l, d)
    tm = 512
    out = pl.pallas_call(
        _final_norm_kernel,
        grid=(b * l // tm,),
        in_specs=[pl.BlockSpec((tm, d), lambda i: (i, 0)), pl.BlockSpec((1, d), lambda i: (0, 0))],
        out_specs=pl.BlockSpec((tm, d), lambda i: (i, 0)),
        out_shape=jax.ShapeDtypeStruct((b * l, d), F32),
        compiler_params=_cparams("parallel"),
        name="final_norm",
    )(x2, g.reshape(1, d))
    return out.reshape(b, l, d)


def _inproj_kernel(h_ref, wqkv_ref, ws5_ref, why_ref, wg_ref, gb_ref, oqkv_ref, os5_ref, ohy_ref, og_ref):
    h = h_ref[...]
    oqkv_ref[...] = jnp.dot(h, wqkv_ref[...], preferred_element_type=F32)
    os5_ref[...] = jnp.dot(h, ws5_ref[...], preferred_element_type=F32)
    ohy_ref[...] = jnp.dot(h, why_ref[...], preferred_element_type=F32)
    og_ref[...] = jnp.dot(h, wg_ref[...], preferred_element_type=F32) + gb_ref[...]


def in_projection(h, w_parts, gate_b):
    m, d = h.shape
    tm = 256
    widths = [w.shape[1] for w in w_parts]
    w_specs = [pl.BlockSpec((d, n), lambda i: (0, 0)) for n in widths]
    return pl.pallas_call(
        _inproj_kernel,
        grid=(m // tm,),
        in_specs=[pl.BlockSpec((tm, d), lambda i: (i, 0))] + w_specs
                 + [pl.BlockSpec((1, widths[3]), lambda i: (0, 0))],
        out_specs=[pl.BlockSpec((tm, n), lambda i: (i, 0)) for n in widths],
        out_shape=[jax.ShapeDtypeStruct((m, n), F32) for n in widths],
        compiler_params=_cparams("parallel"),
        name="in_projection",
    )(h, *w_parts, gate_b.reshape(1, -1))


def _merge_kernel(att_ref, hy_ref, s5_ref, gl_ref, wa_ref, wh_ref, ws_ref, wo_ref, x_ref, g1_ref, o_ref):
    d = wo_ref.shape[0]
    g = jax.nn.sigmoid(gl_ref[0])
    m = g[:, :d] * jnp.dot(att_ref[0], wa_ref[...], preferred_element_type=F32)
    m = m + g[:, d:2 * d] * jnp.dot(hy_ref[0], wh_ref[...], preferred_element_type=F32)
    m = m + g[:, 2 * d:] * jnp.dot(s5_ref[0], ws_ref[...], preferred_element_type=F32)
    y = jnp.dot(m.astype(BF16), wo_ref[...], preferred_element_type=F32)
    o_ref[0] = x_ref[0] + g1_ref[0] * y


def branch_merge(att, hy, s5, gate_logits, wa, wh, ws, wo, x, g1):
    b, l, d = x.shape
    tl = min(l, 256)
    wd = att.shape[-1]
    per_batch = g1.shape[0] == b and b > 1
    g_map = (lambda i, j: (i, 0, 0)) if per_batch else (lambda i, j: (0, 0, 0))
    row = lambda n: pl.BlockSpec((1, tl, n), lambda i, j: (i, j, 0))
    full = lambda a: pl.BlockSpec(a.shape, lambda i, j: (0, 0))
    return pl.pallas_call(
        _merge_kernel,
        grid=(b, l // tl),
        in_specs=[row(wd), row(wd), row(wd), row(N_BRANCH * d), full(wa), full(wh), full(ws), full(wo),
                  row(d), pl.BlockSpec((1, 1, d), g_map)],
        out_specs=row(d),
        out_shape=jax.ShapeDtypeStruct((b, l, d), F32),
        compiler_params=_cparams("parallel", "parallel"),
        name="branch_merge",
    )(att, hy, s5, gate_logits, wa, wh, ws, wo, x, g1.reshape(-1, 1, d))


def _topk_rows(s, k):
    rows = []
    for _ in range(k):
        m = jnp.max(s, axis=0, keepdims=True)
        rows.append(m)
        s = jnp.where(s >= m, NEG_INF, s)
    return rows


LANES = 128


def _store_key_rows(ref, hh, v):
    for tb in range(v.shape[1] // LANES):
        ref[hh, tb] = v[:, tb * LANES:(tb + 1) * LANES]


def _peer_select_kernel(ht_ref, wqt_ref, keys_ref, rank_ref, cnt_ref, e1_ref, e2_ref, qt_scr, cand_scr):
    nh = rank_ref.shape[0]
    nk = PEER_KEYS
    qt_scr[...] = jnp.dot(wqt_ref[...], ht_ref[...], preferred_element_type=F32).astype(BF16)
    pairs = [(p, q) for p in range(PEER_TOPK) for q in range(PEER_TOPK) if (p + 1) * (q + 1) <= PEER_TOPK]

    def head(hh, carry):
        r0 = pl.multiple_of(hh * 2 * nk, 2 * nk)
        s1 = jnp.dot(keys_ref[2 * hh], qt_scr[pl.ds(r0, nk), :], preferred_element_type=F32)
        s2 = jnp.dot(keys_ref[2 * hh + 1], qt_scr[pl.ds(r0 + nk, nk), :], preferred_element_type=F32)
        a = _topk_rows(s1, PEER_TOPK)
        b = []
        rest = s2
        rank = jnp.full(s2.shape, float(PEER_TOPK), F32)
        for r in range(PEER_TOPK):
            m = jnp.max(rest, axis=0, keepdims=True)
            b.append(m)
            hit = rest >= m
            rank = jnp.where(hit, float(r), rank)
            rest = jnp.where(hit, NEG_INF, rest)
        cand_scr[...] = jnp.full(cand_scr.shape, NEG_INF, F32)
        for r, (p, q) in enumerate(pairs):
            cand_scr[r:r + 1, :] = a[p] + b[q]
        best = _topk_rows(cand_scr[...], PEER_TOPK)
        top = a[0] + b[0]
        z = jnp.exp(best[0] - top)
        for r in range(1, PEER_TOPK):
            z = z + jnp.exp(best[r] - top)
        tau = best[PEER_TOPK - 1]
        reached = jnp.where(cand_scr[...] >= tau, 1.0, 0.0)
        cnt = jnp.zeros(s1.shape, F32)
        r = 0
        for p in range(PEER_TOPK):
            n_q = sum(1 for pp, _ in pairs if pp == p)
            cnt_p = reached[r:r + 1]
            for rr in range(r + 1, r + n_q):
                cnt_p = cnt_p + reached[rr:rr + 1]
            r += n_q
            cnt = jnp.where(s1 == a[p], cnt_p, cnt)
        rank_ref[hh] = rank.astype(BF16)
        _store_key_rows(cnt_ref, hh, cnt)
        _store_key_rows(e1_ref, hh, 0.5 * jnp.exp(s1 - a[0]))
        e2_ref[hh] = (jnp.exp(s2 - b[0]) / z).astype(BF16)
        return carry

    lax.fori_loop(0, nh, head, 0)


def peer_select(ht, wqt, keys, tt):
    d, m = ht.shape
    nh = keys.shape[0] // 2
    out = jax.ShapeDtypeStruct((nh, PEER_KEYS, m), BF16)
    out32 = jax.ShapeDtypeStruct((nh, m // LANES, PEER_KEYS, LANES), F32)
    ospec = pl.BlockSpec((nh, PEER_KEYS, tt), lambda i: (0, 0, i))
    ospec32 = pl.BlockSpec((nh, tt // LANES, PEER_KEYS, LANES), lambda i: (0, i, 0, 0))
    return pl.pallas_call(
        _peer_select_kernel,
        grid=(m // tt,),
        in_specs=[pl.BlockSpec((d, tt), lambda i: (0, i)),
                  pl.BlockSpec(wqt.shape, lambda i: (0, 0)),
                  pl.BlockSpec(keys.shape, lambda i: (0, 0, 0))],
        out_specs=[ospec, ospec32, ospec32, ospec],
        out_shape=[out, out32, out32, out],
        scratch_shapes=[pltpu.VMEM((wqt.shape[0], tt), BF16), pltpu.VMEM((56, tt), F32)],
        compiler_params=_cparams("parallel"),
        name="peer_select",
    )(ht, wqt, keys)


PEER_PAIR = 2 * PEER_KEYS
PEER_LOOKAHEAD = 2


def _half_gate_gelu(half_w, x):
    c = math.sqrt(2.0 / math.pi)
    p = half_w * x
    return p + p * jnp.tanh(x * (c + (c * 0.044715) * (x * x)))


def _key_row(ref, hh, ii):
    blocks = [jnp.broadcast_to(ref[hh, tb, ii:ii + 1, :], (PEER_KEYS, LANES)) for tb in range(ref.shape[1])]
    return jnp.concatenate(blocks, axis=1).astype(BF16)


def _peer_expert_kernel(ht_ref, u_ref, vt_ref, rank_ref, cnt_ref, e1_ref, e2_ref, x_ref, g2_ref, o_ref,
                        acc_scr, wa_scr):
    k = pl.program_id(1)
    nh = rank_ref.shape[0]

    @pl.when(k == 0)
    def _():
        acc_scr[...] = jnp.zeros_like(acc_scr)

    ht = ht_ref[...]
    n_pairs = u_ref.shape[0] // PEER_PAIR

    def cols(p):
        return slice(p * PEER_PAIR, (p + 1) * PEER_PAIR)

    def first_matmul(p):
        return jnp.dot(u_ref[cols(p), :], ht, preferred_element_type=F32)

    def gated(p, act):
        w_rows = []
        for r in range(2):
            ii = 2 * p + r
            w = None
            for hh in range(nh):
                selected = rank_ref[hh] < _key_row(cnt_ref, hh, ii)
                gate = jnp.where(selected, e2_ref[hh], jnp.zeros((), BF16)) * _key_row(e1_ref, hh, ii)
                w = gate if w is None else w + gate
            w_rows.append(w)
        half_w = jnp.concatenate(w_rows, axis=0).astype(F32)
        return _half_gate_gelu(half_w, act).astype(BF16)

    acts = {q: first_matmul(q) for q in range(min(PEER_LOOKAHEAD, n_pairs))}
    part = None
    for p in range(n_pairs + 1):
        if p + PEER_LOOKAHEAD < n_pairs:
            acts[p + PEER_LOOKAHEAD] = first_matmul(p + PEER_LOOKAHEAD)
        if p < n_pairs:
            wa_scr[p] = gated(p, acts.pop(p))
        if p >= 1:
            d = jnp.dot(vt_ref[:, cols(p - 1)], wa_scr[p - 1], preferred_element_type=F32)
            part = d if part is None else part + d
    acc_scr[...] += part

    @pl.when(k == pl.num_programs(1) - 1)
    def _():
        o_ref[...] = x_ref[...] + g2_ref[0] * acc_scr[...].T


def peer_experts(ht, u_tab, vt_tab, sel, x, g2, tokens_per_batch, tt, ec):
    d, m = ht.shape
    e = u_tab.shape[0]
    nh = sel[0].shape[0]
    rows_i = ec // PEER_KEYS
    full_keys = pl.BlockSpec((nh, PEER_KEYS, tt), lambda i, k: (0, 0, i))
    chunk_keys = pl.BlockSpec((nh, tt // LANES, rows_i, LANES), lambda i, k: (0, i, k, 0))
    tiles_per_batch = tokens_per_batch // tt
    if g2.shape[0] > 1:
        g_map = lambda i, k: (i // tiles_per_batch, 0, 0)
    else:
        g_map = lambda i, k: (0, 0, 0)
    rank, cnt, e1, e2 = sel
    return pl.pallas_call(
        _peer_expert_kernel,
        grid=(m // tt, e // ec),
        in_specs=[pl.BlockSpec((d, tt), lambda i, k: (0, i)),
                  pl.BlockSpec((ec, d), lambda i, k: (k, 0)),
                  pl.BlockSpec((d, ec), lambda i, k: (0, k)),
                  full_keys, chunk_keys, chunk_keys, full_keys,
                  pl.BlockSpec((tt, d), lambda i, k: (i, 0)),
                  pl.BlockSpec((1, 1, d), g_map)],
        out_specs=pl.BlockSpec((tt, d), lambda i, k: (i, 0)),
        out_shape=jax.ShapeDtypeStruct((m, d), F32),
        scratch_shapes=[pltpu.VMEM((d, tt), F32), pltpu.VMEM((ec // PEER_PAIR, PEER_PAIR, tt), BF16)],
        compiler_params=_cparams("parallel", "arbitrary"),
        name="peer_experts",
    )(ht, u_tab, vt_tab, rank, cnt, e1, e2, x, g2.reshape(-1, 1, d))


def peer_block(x, g_norm, shift, scale, gate, wqt, keys, u_tab, vt_tab):
    b, l, d = x.shape
    ht = norm_modulate(x, g_norm, shift, scale, transposed=True)
    tt = min(l, 512)
    sel = peer_select(ht, wqt, keys, tt)
    out = peer_experts(ht, u_tab, vt_tab, sel, x.reshape(b * l, d), gate, l, tt, 2048)
    return out.reshape(b, l, d)


def rope_tables(l):
    rows_n = l // GRID_W
    row = jnp.repeat(jnp.arange(rows_n), GRID_W).astype(F32)
    col = jnp.tile(jnp.arange(GRID_W), rows_n).astype(F32)
    inv = jnp.power(ROPE_BASE, -jnp.arange(ROPE_PAIRS_PER_AXIS, dtype=F32) / ROPE_PAIRS_PER_AXIS)
    ang = jnp.concatenate([row[:, None] * inv, col[:, None] * inv], axis=-1)
    c, s = jnp.cos(ang), jnp.sin(ang)
    return jnp.concatenate([c, c, c, c], axis=-1), jnp.concatenate([-s, s, -s, s], axis=-1)


def _rope_pair(x, cos2, sin2):
    half = HEAD_DIM // 2
    lane = lax.broadcasted_iota(jnp.int32, x.shape, 1)
    swapped = jnp.where(lane % HEAD_DIM < half, pltpu.roll(x, 2 * HEAD_DIM - half, axis=1),
                        pltpu.roll(x, half, axis=1))
    return x * cos2 + swapped * sin2


def _qkv_heads_kernel(qkv_ref, cos_ref, sin_ref, q_ref, k_ref, v_ref, *, rope):
    scale = HEAD_DIM ** -0.5
    x = qkv_ref[0]
    pair = 2 * HEAD_DIM
    for j in range((ATT_WIDTH + KV_WIDTH) // pair):
        blk = x[:, j * pair:(j + 1) * pair]
        if rope:
            blk = _rope_pair(blk, cos_ref[...], sin_ref[...])
        for t in range(2):
            head = blk[:, t * HEAD_DIM:(t + 1) * HEAD_DIM]
            h = 2 * j + t
            if h < ATT_HEADS:
                q_ref[0, h] = (head * scale).astype(q_ref.dtype)
            else:
                k_ref[0, h - ATT_HEADS] = head.astype(k_ref.dtype)
    for t in range(ATT_KV_HEADS):
        v_ref[0, t] = x[:, COL_V + t * HEAD_DIM:COL_V + (t + 1) * HEAD_DIM].astype(v_ref.dtype)


def qkv_heads(qkv, cos2, sin2, rope):
    b, l, w = qkv.shape
    tl = min(l, 256)
    tab = pl.BlockSpec((tl, 2 * HEAD_DIM), lambda i, j: (j, 0))
    hspec = lambda n: pl.BlockSpec((1, n, tl, HEAD_DIM), lambda i, j: (i, 0, j, 0))
    return pl.pallas_call(
        functools.partial(_qkv_heads_kernel, rope=rope),
        grid=(b, l // tl),
        in_specs=[pl.BlockSpec((1, tl, w), lambda i, j: (i, j, 0)), tab, tab],
        out_specs=[hspec(ATT_HEADS), hspec(ATT_KV_HEADS), hspec(ATT_KV_HEADS)],
        out_shape=[jax.ShapeDtypeStruct((b, ATT_HEADS, l, HEAD_DIM), BF16),
                   jax.ShapeDtypeStruct((b, ATT_KV_HEADS, l, HEAD_DIM), BF16),
                   jax.ShapeDtypeStruct((b, ATT_KV_HEADS, l, HEAD_DIM), BF16)],
        compiler_params=_cparams("parallel", "parallel"),
        name="qkv_heads",
    )(qkv, cos2, sin2)


def _attention_kernel(sink_ref, q_ref, kc_ref, vc_ref, *rest, local):
    if local:
        kp_ref, ko_ref, kn_ref, vp_ref, vo_ref, vn_ref, bias_ref, o_ref = rest
    else:
        (o_ref,) = rest
    blk = ATT_BLOCK
    rows = ATT_GROUP * blk
    outs = []
    for g in range(ATT_KV_HEADS):
        q = q_ref[0, g * ATT_GROUP:(g + 1) * ATT_GROUP].reshape(rows, HEAD_DIM)
        if local:
            keys = jnp.concatenate([kc_ref[0, g], kp_ref[0, g], ko_ref[0, g], kn_ref[0, g]], axis=0)
            vals = jnp.concatenate([vc_ref[0, g], vp_ref[0, g], vo_ref[0, g], vn_ref[0, g]], axis=0)
        else:
            keys, vals = kc_ref[0, g], vc_ref[0, g]
        s = lax.dot_general(q, keys, (((1,), (1,)), ((), ())), preferred_element_type=F32)
        r = lax.broadcasted_iota(jnp.int32, (rows, 1), 0)
        if local:
            s = s + jnp.concatenate([bias_ref[0]] * ATT_GROUP, axis=0)
        sink = jnp.zeros((rows, 1), F32)
        for t in range(ATT_GROUP):
            sink = jnp.where(r // blk == t, sink_ref[g * ATT_GROUP + t], sink)
        m = jnp.maximum(jnp.max(s, axis=-1, keepdims=True), sink)
        p = jnp.exp(s - m)
        denom = jnp.sum(p, axis=-1, keepdims=True) + jnp.exp(sink - m)
        o = jnp.dot(p.astype(BF16), vals, preferred_element_type=F32) / denom
        outs += [o[t * blk:(t + 1) * blk] for t in range(ATT_GROUP)]
    o_ref[0] = jnp.concatenate(outs, axis=1).astype(o_ref.dtype)


def attention(q, k, v, kc, vc, sink, local):
    b, _, l, d = q.shape
    c = kc.shape[2]
    nb = l // ATT_BLOCK
    qspec = pl.BlockSpec((1, ATT_HEADS, ATT_BLOCK, d), lambda i, j: (i, 0, j, 0))
    cspec = pl.BlockSpec((1, ATT_KV_HEADS, c, d), lambda i, j: (i, 0, 0, 0))
    in_specs = [pl.BlockSpec(memory_space=pltpu.SMEM), qspec, cspec, cspec]
    args = [sink, q, kc, vc]
    if local:
        kv = lambda f: pl.BlockSpec((1, ATT_KV_HEADS, ATT_BLOCK, d), lambda i, j: (i, 0, f(j), 0))
        band = [kv(lambda j: jnp.maximum(j - 1, 0)), kv(lambda j: j), kv(lambda j: jnp.minimum(j + 1, nb - 1))]
        in_specs += band + band
        args += [k, k, k, v, v, v]
        qpos = jnp.arange(ATT_BLOCK)[:, None]
        kpos = jnp.arange(-ATT_BLOCK, 2 * ATT_BLOCK)[None, :]
        near = jnp.abs(kpos - qpos) <= WINDOW
        variants = [near, near & (kpos >= 0), near & (kpos < ATT_BLOCK), near & (kpos >= 0) & (kpos < ATT_BLOCK)]
        bias = jnp.stack([jnp.concatenate([jnp.zeros((ATT_BLOCK, c), F32), jnp.where(ok, 0.0, NEG_INF)], axis=1)
                          for ok in variants])
        in_specs.append(pl.BlockSpec((1, ATT_BLOCK, c + 3 * ATT_BLOCK),
                                     lambda i, j: ((j == 0) + 2 * (j == nb - 1), 0, 0)))
        args.append(bias)
    return pl.pallas_call(
        functools.partial(_attention_kernel, local=local),
        grid=(b, nb),
        in_specs=in_specs,
        out_specs=pl.BlockSpec((1, ATT_BLOCK, ATT_HEADS * d), lambda i, j: (i, j, 0)),
        out_shape=jax.ShapeDtypeStruct((b, l, ATT_HEADS * d), BF16),
        compiler_params=_cparams("parallel", "parallel"),
        name="window_attention" if local else "context_attention",
    )(*args)


def hyena_filters(L, lp):
    t = jnp.arange(L, dtype=F32)
    tn = t / L
    bands = jnp.arange(1, HY_BANDS + 1, dtype=F32)
    ang = 2.0 * math.pi * tn[:, None] * bands[None, :]
    z = jnp.concatenate([tn[:, None], jnp.cos(ang), jnp.sin(ang)], axis=-1)
    hdn = jnp.sin(lp['hy_freq1'] * (z @ lp['hy_w1'] + lp['hy_b1']))
    hdn = jnp.sin(lp['hy_freq2'] * (hdn @ lp['hy_w2'] + lp['hy_b2']))
    filt = hdn @ lp['hy_w3']
    rate = jnp.linspace(HY_FAST_RATE, HY_SLOW_RATE, HY_WIDTH, dtype=F32)
    tw = jnp.linspace(0.0, 1.0, L, dtype=F32)
    window = jnp.exp(-tw[:, None] * rate[None, :])
    return filt[:, :HY_WIDTH] * window, filt[:, HY_WIDTH:] * window


def _short_conv_kernel(z_ref, zp_ref, zn_ref, w_ref, b_ref, x0_ref, vg_ref, vgb_ref):
    i = pl.program_id(1)
    z = z_ref[0]
    tl, w3 = z.shape
    prev_row = jnp.where(i > 0, zp_ref[0, 7:8, :], 0.0)
    next_row = jnp.where(i < pl.num_programs(1) - 1, zn_ref[0, 0:1, :], 0.0)
    row = lax.broadcasted_iota(jnp.int32, (tl, w3), 0)
    z_prev = jnp.where(row == 0, prev_row, pltpu.roll(z, 1, axis=0))
    z_next = jnp.where(row == tl - 1, next_row, pltpu.roll(z, tl - 1, axis=0))
    y = b_ref[...] + z_prev * w_ref[0:1, :] + z * w_ref[1:2, :] + z_next * w_ref[2:3, :]
    w = w3 // HY_N_PROJ
    x0_ref[0] = y[:, :w]
    vg = y[:, 2 * w:] * y[:, w:2 * w]
    vg_ref[0] = vg
    vgb_ref[0] = vg.astype(BF16)


def hyena_short_conv(z, w, b):
    bsz, l, w3 = z.shape
    tl = min(l, 256)
    nb8 = tl // 8
    last8 = l // 8 - 1
    out = jax.ShapeDtypeStruct((bsz, l, w3 // HY_N_PROJ), F32)
    ospec = pl.BlockSpec((1, tl, w3 // HY_N_PROJ), lambda i, j: (i, j, 0))
    return pl.pallas_call(
        _short_conv_kernel,
        grid=(bsz, l // tl),
        in_specs=[pl.BlockSpec((1, tl, w3), lambda i, j: (i, j, 0)),
                  pl.BlockSpec((1, 8, w3), lambda i, j: (i, jnp.maximum(j * nb8 - 1, 0), 0)),
                  pl.BlockSpec((1, 8, w3), lambda i, j: (i, jnp.minimum((j + 1) * nb8, last8), 0)),
                  pl.BlockSpec((HY_SHORT, w3), lambda i, j: (0, 0)),
                  pl.BlockSpec((1, w3), lambda i, j: (0, 0))],
        out_specs=[ospec, ospec, ospec],
        out_shape=[out, out, jax.ShapeDtypeStruct(out.shape, BF16)],
        compiler_params=_cparams("parallel", "parallel"),
        name="hyena_short_conv",
    )(z, z, z, w, b.reshape(1, w3))


def dft_tables(l):
    n = 2 * l
    split = 64
    t = jnp.arange(l, dtype=jnp.int32)[None, :]
    kh = jnp.arange(l // split, dtype=jnp.int32)[:, None]
    kl = jnp.arange(split, dtype=jnp.int32)[:, None]
    ang_h = (2.0 * math.pi / n) * (((kh * split) * t) % n).astype(F32)
    ang_l = (2.0 * math.pi / n) * ((kl * t) % n).astype(F32)
    ch, sh = jnp.cos(ang_h)[:, None, :], jnp.sin(ang_h)[:, None, :]
    cl, sl = jnp.cos(ang_l)[None, :, :], jnp.sin(ang_l)[None, :, :]
    cos_t = (ch * cl - sh * sl).reshape(l, l).astype(BF16)
    sin_t = (sh * cl + ch * sl).reshape(l, l).astype(BF16)
    return cos_t, sin_t


def _filter_spectrum_kernel(c_ref, s_ref, hp_ref, hm_ref, kre_ref, kim_ref):
    kre_ref[...] = jnp.dot(c_ref[...], hp_ref[...], preferred_element_type=F32)
    kim_ref[...] = -jnp.dot(s_ref[...], hm_ref[...], preferred_element_type=F32)


def filter_spectrum(cos_t, sin_t, h_fwd, h_bwd):
    l, w = h_fwd.shape
    hb0 = h_bwd.at[0].set(0.0)
    tk = 512
    tab = pl.BlockSpec((tk, l), lambda k: (k, 0))
    full = pl.BlockSpec((l, w), lambda k: (0, 0))
    ospec = pl.BlockSpec((tk, w), lambda k: (k, 0))
    out = jax.ShapeDtypeStruct((l, w), F32)
    kre, kim = pl.pallas_call(
        _filter_spectrum_kernel,
        grid=(l // tk,),
        in_specs=[tab, tab, full, full],
        out_specs=[ospec, ospec],
        out_shape=[out, out],
        compiler_params=_cparams("parallel"),
        name="hyena_filter_spectrum",
    )(cos_t, sin_t, (h_fwd + hb0).astype(BF16), (h_fwd - hb0).astype(BF16))
    sign = jnp.where(jnp.arange(l) % 2 == 0, 1.0, -1.0)[:, None]
    k_nyq = jnp.sum(sign * (h_fwd + hb0), axis=0, keepdims=True)
    return kre, kim, k_nyq


def _long_conv_kernel(x_ref, crow_ref, srow_ref, ccol_ref, scol_ref, kre_ref, kim_ref, knyq_ref, y_ref,
                      acc_scr, xn_scr):
    k = pl.program_id(1)
    nk = pl.num_programs(1)
    x = x_ref[0]
    l = x.shape[0]
    tk = crow_ref.shape[0]

    @pl.when(k == 0)
    def _():
        acc_scr[...] = jnp.zeros_like(acc_scr)
        t_idx = lax.broadcasted_iota(jnp.int32, (8, l), 1)
        alt = jnp.where(t_idx % 2 == 0, 1.0, -1.0).astype(BF16)
        xn_scr[...] = jnp.dot(alt, x, preferred_element_type=F32)

    xre = jnp.dot(crow_ref[...], x, preferred_element_type=F32)
    xim = -jnp.dot(srow_ref[...], x, preferred_element_type=F32)
    kre, kim = kre_ref[...], kim_ref[...]
    k_idx = lax.broadcasted_iota(jnp.int32, (tk, 1), 0) + k * tk
    weight = jnp.where(k_idx == 0, 1.0, 2.0)
    yre = ((xre * kre - xim * kim) * weight).astype(BF16)
    nyim = ((xre * kim + xim * kre) * -weight).astype(BF16)
    acc_scr[...] += (jnp.dot(ccol_ref[...], yre, preferred_element_type=F32)
                     + jnp.dot(scol_ref[...], nyim, preferred_element_type=F32))

    @pl.when(k == nk - 1)
    def _():
        t_idx = lax.broadcasted_iota(jnp.int32, (l, 1), 0)
        nyq = xn_scr[0:1, :] * knyq_ref[...]
        y_ref[0] = (acc_scr[...] + jnp.where(t_idx % 2 == 0, nyq, -nyq)) * (0.5 / l)


def long_conv(x, cos_t, sin_t, kre, kim, k_nyq):
    b, l, w = x.shape
    tk = 256
    row_tab = pl.BlockSpec((tk, l), lambda i, k: (k, 0))
    col_tab = pl.BlockSpec((l, tk), lambda i, k: (0, k))
    kspec = pl.BlockSpec((tk, w), lambda i, k: (k, 0))
    return pl.pallas_call(
        _long_conv_kernel,
        grid=(b, l // tk),
        in_specs=[pl.BlockSpec((1, l, w), lambda i, k: (i, 0, 0)), row_tab, row_tab, col_tab, col_tab,
                  kspec, kspec, pl.BlockSpec((1, w), lambda i, k: (0, 0))],
        out_specs=pl.BlockSpec((1, l, w), lambda i, k: (i, 0, 0)),
        out_shape=jax.ShapeDtypeStruct((b, l, w), F32),
        scratch_shapes=[pltpu.VMEM((l, w), F32), pltpu.VMEM((8, w), F32)],
        compiler_params=_cparams("parallel", "arbitrary"),
        name="hyena_long_conv",
    )(x, cos_t, sin_t, cos_t, sin_t, kre, kim, k_nyq)


def _hyena_gate_kernel(y_ref, vg_ref, x0_ref, bias_ref, o_ref):
    o_ref[...] = ((y_ref[...] + vg_ref[...] * bias_ref[...]) * x0_ref[...]).astype(o_ref.dtype)


def hyena_gate(y, vg, x0, bias):
    b, l, w = y.shape
    m = b * l
    tm = 1024
    row = pl.BlockSpec((tm, w), lambda i: (i, 0))
    out = pl.pallas_call(
        _hyena_gate_kernel,
        grid=(m // tm,),
        in_specs=[row, row, row, pl.BlockSpec((1, w), lambda i: (0, 0))],
        out_specs=row,
        out_shape=jax.ShapeDtypeStruct((m, w), BF16),
        compiler_params=_cparams("parallel"),
        name="hyena_gate",
    )(y.reshape(m, w), vg.reshape(m, w), x0.reshape(m, w), bias.reshape(1, w))
    return out.reshape(b, l, w)


def _dense_dft_tables(l):
    n = 2 * l
    ang = 2.0 * np.pi * np.outer(np.arange(n), np.arange(n)) / n
    fwd = np.concatenate([np.cos(ang), -np.sin(ang)], axis=0)
    inv = np.concatenate([np.cos(ang[:l]), -np.sin(ang[:l])], axis=1) / n
    return jnp.asarray(fwd, F32), jnp.asarray(inv, F32)


def _dense_conv_kernel(vg_ref, x0_ref, kern_ref, f_ref, g_ref, bias_ref, o_ref):
    hp = lax.Precision.HIGHEST
    l = vg_ref.shape[1]
    n = 2 * l
    vg = vg_ref[0]
    ks = jnp.dot(f_ref[...], kern_ref[...], preferred_element_type=F32, precision=hp)
    xs = jnp.dot(f_ref[:, :l], vg, preferred_element_type=F32, precision=hp)
    kre, kim, xre, xim = ks[:n], ks[n:], xs[:n], xs[n:]
    ys = jnp.concatenate([xre * kre - xim * kim, xre * kim + xim * kre], axis=0)
    y = jnp.dot(g_ref[...], ys, preferred_element_type=F32, precision=hp)
    o_ref[0] = ((y + vg * bias_ref[...]) * x0_ref[0]).astype(o_ref.dtype)


def hyena_dense_conv(vg, x0, kern, bias):
    bsz, l, w = vg.shape
    fwd, inv = _dense_dft_tables(l)
    row = pl.BlockSpec((1, l, w), lambda i: (i, 0, 0))
    full = lambda a: pl.BlockSpec(a.shape, lambda i: (0, 0))
    return pl.pallas_call(
        _dense_conv_kernel,
        grid=(bsz,),
        in_specs=[row, row, full(kern), full(fwd), full(inv), pl.BlockSpec((1, w), lambda i: (0, 0))],
        out_specs=row,
        out_shape=jax.ShapeDtypeStruct((bsz, l, w), BF16),
        compiler_params=_cparams("parallel"),
        name="hyena_dense_conv",
    )(vg, x0, kern, fwd, inv, bias.reshape(1, w))


HY_DENSE_MAX_LEN = 512


def hyena_mixer(z, lp, tables=None):
    l = z.shape[1]
    x0, vg, vg_bf16 = hyena_short_conv(z, lp['hy_short_w'], lp['hy_short_b'])
    h_fwd, h_bwd = hyena_filters(l, lp)
    if l <= HY_DENSE_MAX_LEN:
        kern = jnp.concatenate([h_fwd, jnp.zeros_like(h_fwd[:1]), h_bwd[1:][::-1]], axis=0)
        return hyena_dense_conv(vg, x0, kern, lp['hy_bias'])
    cos_t, sin_t = tables if tables is not None else dft_tables(l)
    kre, kim, k_nyq = filter_spectrum(cos_t, sin_t, h_fwd, h_bwd)
    y = long_conv(vg_bf16, cos_t, sin_t, kre, kim, k_nyq)
    return hyena_gate(y, vg, x0, lp['hy_bias'])


S5_CHUNK = 16
S5_CK = S5_CHUNK * S5_GROUP


def _s5_powers(lp, d):
    lam = lax.complex(jnp.minimum(lp['s5_a_re'][d], -1e-4), lp['s5_a_im'][d])
    dt = jnp.exp(lp['s5_log_dt'][d])[:, None]
    n = jnp.arange(S5_CHUNK + 1, dtype=F32)[:, None, None]
    apow = jnp.exp((lam * dt)[None] * n)
    b = lax.complex(lp['s5_b_re'][d], lp['s5_b_im'][d])
    bbar = ((apow[1] - 1.0) / lam)[..., None] * b
    cmat = lax.complex(lp['s5_c_re'][d], lp['s5_c_im'][d])
    return apow, bbar, cmat


def s5_prepare(lp):
    q = S5_CHUNK
    g_n, p_n = S5_GROUPS, S5_STATE
    hp = lax.Precision.HIGHEST
    ap_f, bb_f, c_f = _s5_powers(lp, 0)
    ap_b, bb_b, c_b = _s5_powers(lp, 1)
    k_f = jnp.einsum('gop,tgp,gpi->gtio', c_f, ap_f[:q], bb_f, precision=hp).real
    k_b = jnp.einsum('gop,tgp,gpi->gtio', c_b, ap_b[:q], bb_b, precision=hp).real
    lag = jnp.arange(q)[None, :] - jnp.arange(q)[:, None]
    m_f = jnp.where((lag >= 0)[None, :, :, None, None], k_f[:, jnp.clip(lag, 0, q - 1)], 0.0)
    m_b = jnp.where((lag <= 0)[None, :, :, None, None], k_b[:, jnp.clip(-lag, 0, q - 1)], 0.0)
    m_intra = (m_f + m_b).transpose(0, 1, 3, 2, 4).reshape(g_n, S5_CK, S5_CK)

    ws_f = ap_f[q - 1 - jnp.arange(q)][:, :, :, None] * bb_f[None]
    ws_b = ap_b[jnp.arange(q)][:, :, :, None] * bb_b[None]
    ws = jnp.stack([ws_f.real, ws_f.imag, ws_b.real, ws_b.imag], axis=0)
    ws = ws.transpose(2, 1, 4, 0, 3).reshape(g_n, S5_CK, 4, p_n)
    w_state = jnp.zeros((g_n // 2, 2, S5_CK, 4, 2, p_n), F32)
    w_state = w_state.at[:, 0, :, :, 0].set(ws[0::2]).at[:, 1, :, :, 1].set(ws[1::2])
    w_state = w_state.reshape(g_n // 2, 2, S5_CK, 8 * p_n)

    ca_f = c_f[None] * ap_f[1:q + 1][:, :, None, :]
    ca_b = c_b[None] * ap_b[q - jnp.arange(q)][:, :, None, :]
    wo = jnp.stack([ca_f.real, -ca_f.imag, ca_b.real, -ca_b.imag], axis=0)
    wo = wo.transpose(2, 0, 4, 1, 3).reshape(g_n, 4, p_n, S5_CK)
    w_out = jnp.zeros((g_n, 4, 2, p_n, S5_CK), F32)
    w_out = w_out.at[0::2, :, 0].set(wo[0::2]).at[1::2, :, 1].set(wo[1::2])
    w_out = w_out.reshape(g_n, 8 * p_n, S5_CK)

    def flat(z):
        return jnp.stack([z.real.reshape(-1), z.imag.reshape(-1)], axis=0)

    return m_intra.astype(BF16), w_state.astype(BF16), w_out.astype(BF16), flat(ap_f[q]), flat(ap_b[q])


def _s5_state_kernel(u_ref, w_ref, fre_ref, fim_ref, bre_ref, bim_ref):
    o = (jnp.dot(u_ref[0], w_ref[0, 0], preferred_element_type=F32)
         + jnp.dot(u_ref[1], w_ref[0, 1], preferred_element_type=F32))
    n = fre_ref.shape[2]
    for comp, ref in enumerate((fre_ref, fim_ref, bre_ref, bim_ref)):
        ref[...] = o[:, comp * n:(comp + 1) * n].reshape(ref.shape)


def s5_chunk_states(u_t, w_state, rows):
    g_n, r, ck = u_t.shape
    pw = 2 * S5_STATE
    out = jax.ShapeDtypeStruct((r // rows, rows, g_n * S5_STATE), F32)
    ospec = pl.BlockSpec((r // rows, rows, pw), lambda i: (0, 0, i))
    return pl.pallas_call(
        _s5_state_kernel,
        grid=(g_n // 2,),
        in_specs=[pl.BlockSpec((2, r, ck), lambda i: (i, 0, 0)),
                  pl.BlockSpec((1, 2, ck, 4 * pw), lambda i: (i, 0, 0, 0))],
        out_specs=[ospec] * 4,
        out_shape=[out] * 4,
        compiler_params=_cparams("parallel"),
        name="s5_chunk_states",
    )(u_t, w_state)


def _s5_carry_kernel(sre_ref, sim_ref, a_ref, hre_ref, him_ref, *, n_ctx_chunks, reverse):
    n_chunks = sre_ref.shape[0]
    ar = a_ref[0:1, :]
    ai = a_ref[1:2, :]
    zero = jnp.zeros(sre_ref.shape[1:], F32)

    def body(n, carry):
        hre, him = carry
        if reverse:
            c = jnp.where(n < n_ctx_chunks, n_ctx_chunks - 1 - n, n_chunks + n_ctx_chunks - 1 - n)
        else:
            c = n
        hre_ref[c] = hre
        him_ref[c] = him
        return ar * hre - ai * him + sre_ref[c], ar * him + ai * hre + sim_ref[c]

    lax.fori_loop(0, n_chunks, body, (zero, zero))


def s5_carry(sre, sim, a_chunk, n_ctx_chunks, reverse):
    nc, rows, w = sre.shape
    tl = 512
    spec = pl.BlockSpec((nc, rows, tl), lambda i: (0, 0, i))
    out = jax.ShapeDtypeStruct((nc, rows, w), F32)
    return pl.pallas_call(
        functools.partial(_s5_carry_kernel, n_ctx_chunks=n_ctx_chunks, reverse=reverse),
        grid=(w // tl,),
        in_specs=[spec, spec, pl.BlockSpec((2, tl), lambda i: (0, i))],
        out_specs=[spec, spec],
        out_shape=[out, out],
        compiler_params=_cparams("parallel"),
        name="s5_carry_bwd" if reverse else "s5_carry_fwd",
    )(sre, sim, a_chunk)


def _s5_output_kernel(u_ref, m_ref, fre_ref, fim_ref, bre_ref, bim_ref, wo_ref, y_ref):
    nc, rows, pw = fre_ref.shape
    y = jnp.dot(u_ref[0], m_ref[0], preferred_element_type=F32)
    for comp, h_ref in enumerate((fre_ref, fim_ref, bre_ref, bim_ref)):
        y = y + jnp.dot(h_ref[...].reshape(nc * rows, pw).astype(BF16), wo_ref[0, comp * pw:(comp + 1) * pw, :],
                        preferred_element_type=F32)
    y_ref[0] = y


def s5_chunk_outputs(u_t, m_intra, carried, w_out):
    g_n, r, ck = u_t.shape
    pw = 2 * S5_STATE
    nc, rows, _ = carried[0].shape
    hspec = pl.BlockSpec((nc, rows, pw), lambda i: (0, 0, i // 2))
    return pl.pallas_call(
        _s5_output_kernel,
        grid=(g_n,),
        in_specs=[pl.BlockSpec((1, r, ck), lambda i: (i, 0, 0)),
                  pl.BlockSpec((1, ck, ck), lambda i: (i, 0, 0)),
                  hspec, hspec, hspec, hspec,
                  pl.BlockSpec((1, 4 * pw, ck), lambda i: (i, 0, 0))],
        out_specs=pl.BlockSpec((1, r, ck), lambda i: (i, 0, 0)),
        out_shape=jax.ShapeDtypeStruct((g_n, r, ck), F32),
        compiler_params=_cparams("parallel"),
        name="s5_chunk_outputs",
    )(u_t, m_intra, *carried, w_out)


def _s5_readout_kernel(y_ref, u_ref, d_ref, w_ref, b_ref, o_ref):
    y = _gelu_tanh(y_ref[...] + d_ref[...] * u_ref[...])
    z = jnp.dot(y.astype(BF16), w_ref[...], preferred_element_type=F32) + b_ref[...]
    o_ref[...] = (y * jax.nn.sigmoid(z)).astype(o_ref.dtype)


def s5_readout(y, u, d, glu_w, glu_b):
    m, w = y.shape
    tm = min(m, 512)
    row = pl.BlockSpec((tm, w), lambda i: (i, 0))
    vec = pl.BlockSpec((1, w), lambda i: (0, 0))
    return pl.pallas_call(
        _s5_readout_kernel,
        grid=(m // tm,),
        in_specs=[row, row, vec, pl.BlockSpec((w, w), lambda i: (0, 0)), vec],
        out_specs=row,
        out_shape=jax.ShapeDtypeStruct((m, w), BF16),
        compiler_params=_cparams("parallel"),
        name="s5_readout",
    )(y, u, d.reshape(1, w), glu_w, glu_b.reshape(1, w))


def s5_mixer(u, uc, lp, glu_w):
    b, l, w = u.shape
    c = uc.shape[1]
    q = S5_CHUNK
    n_all = (c + l) // q
    m_intra, w_state, w_out, a_f, a_b = s5_prepare(lp)
    u_all = jnp.concatenate([uc, u], axis=1)
    u_t = u_all.astype(BF16).reshape(b, n_all, q, S5_GROUPS, S5_GROUP)
    u_t = u_t.transpose(3, 1, 0, 2, 4).reshape(S5_GROUPS, n_all * b, S5_CK)
    fre, fim, bre, bim = s5_chunk_states(u_t, w_state, b)
    hf = s5_carry(fre, fim, a_f, c // q, False)
    hb = s5_carry(bre, bim, a_b, c // q, True)
    y_t = s5_chunk_outputs(u_t, m_intra, (*hf, *hb), w_out)
    y_all = y_t.reshape(S5_GROUPS, n_all, b, q, S5_GROUP).transpose(2, 1, 3, 0, 4).reshape(b * (c + l), w)
    out = s5_readout(y_all, u_all.reshape(b * (c + l), w), lp['s5_d'], glu_w, lp['s5_glu_b'])
    out = out.reshape(b, c + l, w)
    return out[:, c:], out[:, :c]


def kernel(x, c, ctx, c_ctx, mod_w, mod_b, norm1_g, norm2_g, in_w, gate_b, attn_sink, hy_short_w, hy_short_b,
           hy_w1, hy_b1, hy_freq1, hy_w2, hy_b2, hy_freq2, hy_w3, hy_bias, s5_a_re, s5_a_im, s5_log_dt,
           s5_b_re, s5_b_im, s5_c_re, s5_c_im, s5_d, s5_glu_w, s5_glu_b, br_attn_w, br_hyena_w, br_s5_w,
           out_w, peer_wq, peer_keys, peer_u, peer_v, final_g):
    B, L, D = x.shape
    C = ctx.shape[1]
    depth = mod_w.shape[0]
    rope_cos, rope_sin = rope_tables(L)
    conv_tables = dft_tables(L) if L > HY_DENSE_MAX_LEN else None

    cvec = jnp.zeros((8, D), F32).at[:B].set(c).at[B].set(c_ctx)
    mod_all = modulation(cvec, mod_w, mod_b)

    xc = ctx
    for layer in range(depth):
        need_ctx = layer < depth - 1
        lp = dict(attn_sink=attn_sink[layer],
                  hy_short_w=hy_short_w[layer], hy_short_b=hy_short_b[layer],
                  hy_w1=hy_w1[layer], hy_b1=hy_b1[layer], hy_freq1=hy_freq1[layer],
                  hy_w2=hy_w2[layer], hy_b2=hy_b2[layer], hy_freq2=hy_freq2[layer],
                  hy_w3=hy_w3[layer], hy_bias=hy_bias[layer],
                  s5_a_re=s5_a_re[layer], s5_a_im=s5_a_im[layer], s5_log_dt=s5_log_dt[layer],
                  s5_b_re=s5_b_re[layer], s5_b_im=s5_b_im[layer],
                  s5_c_re=s5_c_re[layer], s5_c_im=s5_c_im[layer], s5_d=s5_d[layer],
                  s5_glu_w=s5_glu_w[layer], s5_glu_b=s5_glu_b[layer])
        mod = mod_all[layer]
        ml = [mod[:B, i * D:(i + 1) * D] for i in range(N_MOD)]
        mc = [mod[B:B + 1, i * D:(i + 1) * D] for i in range(N_MOD)]

        w_in = in_w[layer].astype(BF16)
        w_parts = [w_in[:, COL_Q:COL_S5], w_in[:, COL_S5:COL_HY], w_in[:, COL_HY:COL_GATE], w_in[:, COL_GATE:]]
        wa = br_attn_w[layer].astype(BF16)
        wh = br_hyena_w[layer].astype(BF16)
        ws = br_s5_w[layer].astype(BF16)
        wo = out_w[layer].astype(BF16)
        wqt = peer_wq[layer].T.astype(BF16)
        keys = peer_keys[layer].reshape(2 * PEER_HEADS, PEER_KEYS, PEER_HALF).astype(BF16)
        u_tab = peer_u[layer].astype(BF16)
        vt_tab = peer_v[layer].T.astype(BF16)

        h = norm_modulate(x, norm1_g[layer], ml[0], ml[1])
        hc = norm_modulate(xc, norm1_g[layer], mc[0], mc[1])
        qkv, u_s5, z_hy, gates = in_projection(h.reshape(B * L, D), w_parts, gate_b[layer])
        qkv_c, uc_s5, zc_hy, gates_c = in_projection(hc.reshape(B * C, D), w_parts, gate_b[layer])
        qkv = qkv.reshape(B, L, -1)
        qkv_c = qkv_c.reshape(B, C, -1)
        u_s5 = u_s5.reshape(B, L, -1)
        uc_s5 = uc_s5.reshape(B, C, -1)

        q, k, v = qkv_heads(qkv, rope_cos, rope_sin, True)
        qc, kc, vc = qkv_heads(qkv_c, rope_cos, rope_sin, False)
        att = attention(q, k, v, kc, vc, lp['attn_sink'], True)
        hy = hyena_mixer(z_hy.reshape(B, L, -1), lp, conv_tables)
        s5, s5c = s5_mixer(u_s5, uc_s5, lp, s5_glu_w[layer].astype(BF16))

        x = branch_merge(att, hy, s5, gates.reshape(B, L, -1), wa, wh, ws, wo, x, ml[2])
        x = peer_block(x, norm2_g[layer], ml[3], ml[4], ml[5], wqt, keys, u_tab, vt_tab)

        if need_ctx:
            att_c = attention(qc, None, None, kc, vc, lp['attn_sink'], False)
            hy_c = hyena_mixer(zc_hy.reshape(B, C, -1), lp)
            xc = branch_merge(att_c, hy_c, s5c, gates_c.reshape(B, C, -1), wa, wh, ws, wo, xc, mc[2])
            xc = peer_block(xc, norm2_g[layer], mc[3], mc[4], mc[5], wqt, keys, u_tab, vt_tab)
    return final_norm(x, final_g)
```

```python
import functools
import math

import jax
import jax.numpy as jnp
import numpy as np
from jax import lax
from jax.experimental import pallas as pl
from jax.experimental.pallas import tpu as pltpu

F32 = jnp.float32
BF16 = jnp.bfloat16

NORM_EPS = 1e-6
NEG_INF = -1e30
N_MOD = 6

GRID_W = 64
ATT_HEADS = 8
ATT_KV_HEADS = 2
ATT_GROUP = ATT_HEADS // ATT_KV_HEADS
HEAD_DIM = 64
ATT_WIDTH = ATT_HEADS * HEAD_DIM
KV_WIDTH = ATT_KV_HEADS * HEAD_DIM
WINDOW = 128
ATT_BLOCK = 128
ROPE_BASE = 10000.0
ROPE_PAIRS_PER_AXIS = HEAD_DIM // 4

HY_WIDTH = 512
HY_N_PROJ = 3
HY_SHORT = 3
HY_BANDS = 16
HY_DECAY_TARGET = 1e-2
HY_FAST_RATE = -math.log(HY_DECAY_TARGET) / 0.3
HY_SLOW_RATE = -math.log(HY_DECAY_TARGET) / 1.5

S5_WIDTH = 512
S5_GROUP = 16
S5_GROUPS = S5_WIDTH // S5_GROUP
S5_STATE = 64

N_BRANCH = 3

PEER_HEADS = 8
PEER_KEYS = 128
PEER_HALF = 128
PEER_TOPK = 16

COL_Q = 0
COL_K = COL_Q + ATT_WIDTH
COL_V = COL_K + KV_WIDTH
COL_S5 = COL_V + KV_WIDTH
COL_HY = COL_S5 + S5_WIDTH
COL_GATE = COL_HY + HY_N_PROJ * HY_WIDTH

VMEM_LIMIT_V7X = 56 * 1024 * 1024


def _cparams(*sem):
    return pltpu.CompilerParams(dimension_semantics=sem, vmem_limit_bytes=VMEM_LIMIT_V7X)


def _gelu_tanh(x):
    return 0.5 * x * (1.0 + jnp.tanh(math.sqrt(2.0 / math.pi) * (x + 0.044715 * (x * x * x))))


def _mod_kernel(s_ref, w_ref, b_ref, o_ref):
    s = s_ref[...]
    s = (s * jax.nn.sigmoid(s)).astype(BF16)
    o_ref[0] = jnp.dot(s, w_ref[0].astype(BF16), preferred_element_type=F32) + b_ref[0]


def modulation(cvec, mod_w, mod_b):
    depth, d, n = mod_w.shape
    tn = 1024
    return pl.pallas_call(
        _mod_kernel,
        grid=(depth, n // tn),
        in_specs=[pl.BlockSpec((8, d), lambda l, j: (0, 0)),
                  pl.BlockSpec((1, d, tn), lambda l, j: (l, 0, j)),
                  pl.BlockSpec((1, 1, tn), lambda l, j: (l, 0, j))],
        out_specs=pl.BlockSpec((1, 8, tn), lambda l, j: (l, 0, j)),
        out_shape=jax.ShapeDtypeStruct((depth, 8, n), F32),
        compiler_params=_cparams("parallel", "parallel"),
        name="adaln_mod",
    )(cvec, mod_w, mod_b.reshape(depth, 1, n))


def _norm_mod_kernel(x_ref, g_ref, sh_ref, sc_ref, o_ref, *, transposed):
    x = x_ref[0]
    y = x * lax.rsqrt(jnp.mean(x * x, axis=-1, keepdims=True) + NORM_EPS)
    y = y * g_ref[...]
    y = y * (1.0 + sc_ref[0]) + sh_ref[0]
    if transposed:
        o_ref[...] = y.T.astype(o_ref.dtype)
    else:
        o_ref[0] = y.astype(o_ref.dtype)


def norm_modulate(x, g, shift, scale, transposed=False):
    b, l, d = x.shape
    tl = min(l, 512)
    nt = l // tl
    per_batch = shift.shape[0] == b and b > 1
    mod_map = (lambda i, j: (i, 0, 0)) if per_batch else (lambda i, j: (0, 0, 0))
    if transposed:
        out_spec = pl.BlockSpec((d, tl), lambda i, j: (0, i * nt + j))
        out_shape = jax.ShapeDtypeStruct((d, b * l), BF16)
    else:
        out_spec = pl.BlockSpec((1, tl, d), lambda i, j: (i, j, 0))
        out_shape = jax.ShapeDtypeStruct((b, l, d), BF16)
    return pl.pallas_call(
        functools.partial(_norm_mod_kernel, transposed=transposed),
        grid=(b, nt),
        in_specs=[pl.BlockSpec((1, tl, d), lambda i, j: (i, j, 0)),
                  pl.BlockSpec((1, d), lambda i, j: (0, 0)),
                  pl.BlockSpec((1, 1, d), mod_map),
                  pl.BlockSpec((1, 1, d), mod_map)],
        out_specs=out_spec,
        out_shape=out_shape,
        compiler_params=_cparams("parallel", "parallel"),
        name="norm_modulate_t" if transposed else "norm_modulate",
    )(x, g.reshape(1, d), shift.reshape(-1, 1, d), scale.reshape(-1, 1, d))


def _final_norm_kernel(x_ref, g_ref, o_ref):
    x = x_ref[...]
    y = x * lax.rsqrt(jnp.mean(x * x, axis=-1, keepdims=True) + NORM_EPS)
    o_ref[...] = y * g_ref[...]


def final_norm(x, g):
    b, l, d = x.shape
    x2 = x.reshape(b * l, d)
    tm = 512
    out = pl.pallas_call(
        _final_norm_kernel,
        grid=(b * l // tm,),
        in_specs=[pl.BlockSpec((tm, d), lambda i: (i, 0)), pl.BlockSpec((1, d), lambda i: (0, 0))],
        out_specs=pl.BlockSpec((tm, d), lambda i: (i, 0)),
        out_shape=jax.ShapeDtypeStruct((b * l, d), F32),
        compiler_params=_cparams("parallel"),
        name="final_norm",
    )(x2, g.reshape(1, d))
    return out.reshape(b, l, d)


def _inproj_kernel(h_ref, wqkv_ref, ws5_ref, why_ref, wg_ref, gb_ref, oqkv_ref, os5_ref, ohy_ref, og_ref):
    h = h_ref[...]
    oqkv_ref[...] = jnp.dot(h, wqkv_ref[...], preferred_element_type=F32)
    os5_ref[...] = jnp.dot(h, ws5_ref[...], preferred_element_type=F32)
    ohy_ref[...] = jnp.dot(h, why_ref[...], preferred_element_type=F32)
    og_ref[...] = jnp.dot(h, wg_ref[...], preferred_element_type=F32) + gb_ref[...]


def in_projection(h, w_parts, gate_b):
    m, d = h.shape
    tm = 256
    widths = [w.shape[1] for w in w_parts]
    w_specs = [pl.BlockSpec((d, n), lambda i: (0, 0)) for n in widths]
    return pl.pallas_call(
        _inproj_kernel,
        grid=(m // tm,),
        in_specs=[pl.BlockSpec((tm, d), lambda i: (i, 0))] + w_specs
                 + [pl.BlockSpec((1, widths[3]), lambda i: (0, 0))],
        out_specs=[pl.BlockSpec((tm, n), lambda i: (i, 0)) for n in widths],
        out_shape=[jax.ShapeDtypeStruct((m, n), F32) for n in widths],
        compiler_params=_cparams("parallel"),
        name="in_projection",
    )(h, *w_parts, gate_b.reshape(1, -1))


def _merge_kernel(att_ref, hy_ref, s5_ref, gl_ref, wa_ref, wh_ref, ws_ref, wo_ref, x_ref, g1_ref, o_ref):
    d = wo_ref.shape[0]
    g = jax.nn.sigmoid(gl_ref[0])
    m = g[:, :d] * jnp.dot(att_ref[0], wa_ref[...], preferred_element_type=F32)
    m = m + g[:, d:2 * d] * jnp.dot(hy_ref[0], wh_ref[...], preferred_element_type=F32)
    m = m + g[:, 2 * d:] * jnp.dot(s5_ref[0], ws_ref[...], preferred_element_type=F32)
    y = jnp.dot(m.astype(BF16), wo_ref[...], preferred_element_type=F32)
    o_ref[0] = x_ref[0] + g1_ref[0] * y


def branch_merge(att, hy, s5, gate_logits, wa, wh, ws, wo, x, g1):
    b, l, d = x.shape
    tl = min(l, 256)
    wd = att.shape[-1]
    per_batch = g1.shape[0] == b and b > 1
    g_map = (lambda i, j: (i, 0, 0)) if per_batch else (lambda i, j: (0, 0, 0))
    row = lambda n: pl.BlockSpec((1, tl, n), lambda i, j: (i, j, 0))
    full = lambda a: pl.BlockSpec(a.shape, lambda i, j: (0, 0))
    return pl.pallas_call(
        _merge_kernel,
        grid=(b, l // tl),
        in_specs=[row(wd), row(wd), row(wd), row(N_BRANCH * d), full(wa), full(wh), full(ws), full(wo),
                  row(d), pl.BlockSpec((1, 1, d), g_map)],
        out_specs=row(d),
        out_shape=jax.ShapeDtypeStruct((b, l, d), F32),
        compiler_params=_cparams("parallel", "parallel"),
        name="branch_merge",
    )(att, hy, s5, gate_logits, wa, wh, ws, wo, x, g1.reshape(-1, 1, d))


def _topk_rows(s, k):
    rows = []
    for _ in range(k):
        m = jnp.max(s, axis=0, keepdims=True)
        rows.append(m)
        s = jnp.where(s >= m, NEG_INF, s)
    return rows


LANES = 128


def _peer_select_kernel(ht_ref, wqt_ref, keys_ref, rank_ref, cnt_ref, e1_ref, e2_ref, qt_scr, cand_scr):
    nh = rank_ref.shape[0]
    nk = PEER_KEYS
    pairs = [(p, q) for p in range(PEER_TOPK) for q in range(PEER_TOPK) if (p + 1) * (q + 1) <= PEER_TOPK]

    def project(hh):
        r0 = pl.multiple_of(hh * 2 * nk, 2 * nk)
        qt_scr[hh % 2] = jnp.dot(wqt_ref[pl.ds(r0, 2 * nk), :], ht_ref[...],
                                 preferred_element_type=F32).astype(BF16)

    project(0)

    def head(hh, carry):
        qt = qt_scr[hh % 2]
        s1 = jnp.dot(keys_ref[2 * hh], qt[:nk], preferred_element_type=F32)
        s2 = jnp.dot(keys_ref[2 * hh + 1], qt[nk:], preferred_element_type=F32)
        project(jnp.minimum(hh + 1, nh - 1))
        for tb in range(s1.shape[1] // LANES):
            lanes = slice(tb * LANES, (tb + 1) * LANES)
            select_block(hh, tb, lanes, s1[:, lanes], s2[:, lanes])
        return carry

    def select_block(hh, tb, lanes, s1, s2):
        a = _topk_rows(s1, PEER_TOPK)
        b = []
        rest = s2
        rank = jnp.full(s2.shape, float(PEER_TOPK), F32)
        for r in range(PEER_TOPK):
            m = jnp.max(rest, axis=0, keepdims=True)
            b.append(m)
            hit = rest >= m
            rank = jnp.where(hit, float(r), rank)
            rest = jnp.where(hit, NEG_INF, rest)
        cand_scr[:, lanes] = jnp.full((cand_scr.shape[0], LANES), NEG_INF, F32)
        for r, (p, q) in enumerate(pairs):
            cand_scr[r:r + 1, lanes] = a[p] + b[q]
        cand = cand_scr[:, lanes]
        best = _topk_rows(cand, PEER_TOPK)
        top = a[0] + b[0]
        z = jnp.exp(best[0] - top)
        for r in range(1, PEER_TOPK):
            z = z + jnp.exp(best[r] - top)
        tau = best[PEER_TOPK - 1]
        reached = jnp.where(cand >= tau, 1.0, 0.0)
        cnt = jnp.zeros(s1.shape, F32)
        r = 0
        for p in range(PEER_TOPK):
            n_q = sum(1 for pp, _ in pairs if pp == p)
            cnt_p = reached[r:r + 1]
            for rr in range(r + 1, r + n_q):
                cnt_p = cnt_p + reached[rr:rr + 1]
            r += n_q
            cnt = jnp.where(s1 == a[p], cnt_p, cnt)
        rank_ref[hh, :, lanes] = rank.astype(BF16)
        cnt_ref[hh, tb] = cnt
        e1_ref[hh, tb] = 0.5 * jnp.exp(s1 - a[0])
        e2_ref[hh, :, lanes] = (jnp.exp(s2 - b[0]) / z).astype(BF16)

    lax.fori_loop(0, nh, head, 0)


def peer_select(ht, wqt, keys, tt):
    d, m = ht.shape
    nh = keys.shape[0] // 2
    out = jax.ShapeDtypeStruct((nh, PEER_KEYS, m), BF16)
    out32 = jax.ShapeDtypeStruct((nh, m // LANES, PEER_KEYS, LANES), F32)
    ospec = pl.BlockSpec((nh, PEER_KEYS, tt), lambda i: (0, 0, i))
    ospec32 = pl.BlockSpec((nh, tt // LANES, PEER_KEYS, LANES), lambda i: (0, i, 0, 0))
    return pl.pallas_call(
        _peer_select_kernel,
        grid=(m // tt,),
        in_specs=[pl.BlockSpec((d, tt), lambda i: (0, i)),
                  pl.BlockSpec(wqt.shape, lambda i: (0, 0)),
                  pl.BlockSpec(keys.shape, lambda i: (0, 0, 0))],
        out_specs=[ospec, ospec32, ospec32, ospec],
        out_shape=[out, out32, out32, out],
        scratch_shapes=[pltpu.VMEM((2, 2 * PEER_KEYS, tt), BF16), pltpu.VMEM((56, tt), F32)],
        compiler_params=_cparams("parallel"),
        name="peer_select",
    )(ht, wqt, keys)


PEER_PAIR = 2 * PEER_KEYS
PEER_LOOKAHEAD = 2


def _half_gate_gelu(half_w, x):
    c = math.sqrt(2.0 / math.pi)
    p = half_w * x
    return p + p * jnp.tanh(x * (c + (c * 0.044715) * (x * x)))


def _key_row(ref, hh, ii):
    blocks = [jnp.broadcast_to(ref[hh, tb, ii:ii + 1, :], (PEER_KEYS, LANES)) for tb in range(ref.shape[1])]
    return jnp.concatenate(blocks, axis=1).astype(BF16)


def _peer_expert_kernel(ht_ref, u_ref, vt_ref, rank_ref, cnt_ref, e1_ref, e2_ref, x_ref, g2_ref, o_ref,
                        acc_scr, wa_scr):
    k = pl.program_id(1)
    nh = rank_ref.shape[0]

    @pl.when(k == 0)
    def _():
        acc_scr[...] = jnp.zeros_like(acc_scr)

    ht = ht_ref[...]
    n_pairs = u_ref.shape[0] // PEER_PAIR

    def cols(p):
        return slice(p * PEER_PAIR, (p + 1) * PEER_PAIR)

    def first_matmul(p):
        return jnp.dot(u_ref[cols(p), :], ht, preferred_element_type=F32)

    def gated(p, act):
        w_rows = []
        for r in range(2):
            ii = 2 * p + r
            w = None
            for hh in range(nh):
                selected = rank_ref[hh] < _key_row(cnt_ref, hh, ii)
                gate = jnp.where(selected, e2_ref[hh], jnp.zeros((), BF16)) * _key_row(e1_ref, hh, ii)
                w = gate if w is None else w + gate
            w_rows.append(w)
        half_w = jnp.concatenate(w_rows, axis=0)
        return _half_gate_gelu(half_w, act.astype(BF16))

    acts = {q: first_matmul(q) for q in range(min(PEER_LOOKAHEAD, n_pairs))}
    part = None
    for p in range(n_pairs + 1):
        if p + PEER_LOOKAHEAD < n_pairs:
            acts[p + PEER_LOOKAHEAD] = first_matmul(p + PEER_LOOKAHEAD)
        if p < n_pairs:
            wa_scr[p] = gated(p, acts.pop(p))
        if p >= 1:
            d = jnp.dot(vt_ref[:, cols(p - 1)], wa_scr[p - 1], preferred_element_type=F32)
            part = d if part is None else part + d
    acc_scr[...] += part

    @pl.when(k == pl.num_programs(1) - 1)
    def _():
        o_ref[...] = x_ref[...] + g2_ref[0] * acc_scr[...].T


def peer_experts(ht, u_tab, vt_tab, sel, x, g2, tokens_per_batch, tt, ec):
    d, m = ht.shape
    e = u_tab.shape[0]
    nh = sel[0].shape[0]
    rows_i = ec // PEER_KEYS
    full_keys = pl.BlockSpec((nh, PEER_KEYS, tt), lambda i, k: (0, 0, i))
    chunk_keys = pl.BlockSpec((nh, tt // LANES, rows_i, LANES), lambda i, k: (0, i, k, 0))
    tiles_per_batch = tokens_per_batch // tt
    if g2.shape[0] > 1:
        g_map = lambda i, k: (i // tiles_per_batch, 0, 0)
    else:
        g_map = lambda i, k: (0, 0, 0)
    rank, cnt, e1, e2 = sel
    return pl.pallas_call(
        _peer_expert_kernel,
        grid=(m // tt, e // ec),
        in_specs=[pl.BlockSpec((d, tt), lambda i, k: (0, i)),
                  pl.BlockSpec((ec, d), lambda i, k: (k, 0)),
                  pl.BlockSpec((d, ec), lambda i, k: (0, k)),
                  full_keys, chunk_keys, chunk_keys, full_keys,
                  pl.BlockSpec((tt, d), lambda i, k: (i, 0)),
                  pl.BlockSpec((1, 1, d), g_map)],
        out_specs=pl.BlockSpec((tt, d), lambda i, k: (i, 0)),
        out_shape=jax.ShapeDtypeStruct((m, d), F32),
        scratch_shapes=[pltpu.VMEM((d, tt), F32), pltpu.VMEM((ec // PEER_PAIR, PEER_PAIR, tt), BF16)],
        compiler_params=_cparams("parallel", "arbitrary"),
        name="peer_experts",
    )(ht, u_tab, vt_tab, rank, cnt, e1, e2, x, g2.reshape(-1, 1, d))


def peer_block(x, g_norm, shift, scale, gate, wqt, keys, u_tab, vt_tab):
    b, l, d = x.shape
    ht = norm_modulate(x, g_norm, shift, scale, transposed=True)
    tt = min(l, 512)
    sel = peer_select(ht, wqt, keys, tt)
    out = peer_experts(ht, u_tab, vt_tab, sel, x.reshape(b * l, d), gate, l, tt, 2048)
    return out.reshape(b, l, d)


def rope_tables(l):
    rows_n = l // GRID_W
    row = jnp.repeat(jnp.arange(rows_n), GRID_W).astype(F32)
    col = jnp.tile(jnp.arange(GRID_W), rows_n).astype(F32)
    inv = jnp.power(ROPE_BASE, -jnp.arange(ROPE_PAIRS_PER_AXIS, dtype=F32) / ROPE_PAIRS_PER_AXIS)
    ang = jnp.concatenate([row[:, None] * inv, col[:, None] * inv], axis=-1)
    c, s = jnp.cos(ang), jnp.sin(ang)
    return jnp.concatenate([c, c, c, c], axis=-1), jnp.concatenate([-s, s, -s, s], axis=-1)


def _rope_pair(x, cos2, sin2):
    half = HEAD_DIM // 2
    lane = lax.broadcasted_iota(jnp.int32, x.shape, 1)
    swapped = jnp.where(lane % HEAD_DIM < half, pltpu.roll(x, 2 * HEAD_DIM - half, axis=1),
                        pltpu.roll(x, half, axis=1))
    return x * cos2 + swapped * sin2


def _qkv_heads_kernel(qkv_ref, cos_ref, sin_ref, q_ref, k_ref, v_ref, *, rope):
    scale = HEAD_DIM ** -0.5
    x = qkv_ref[0]
    pair = 2 * HEAD_DIM
    for j in range((ATT_WIDTH + KV_WIDTH) // pair):
        blk = x[:, j * pair:(j + 1) * pair]
        if rope:
            blk = _rope_pair(blk, cos_ref[...], sin_ref[...])
        for t in range(2):
            head = blk[:, t * HEAD_DIM:(t + 1) * HEAD_DIM]
            h = 2 * j + t
            if h < ATT_HEADS:
                q_ref[0, h] = (head * scale).astype(q_ref.dtype)
            else:
                k_ref[0, h - ATT_HEADS] = head.astype(k_ref.dtype)
    for t in range(ATT_KV_HEADS):
        v_ref[0, t] = x[:, COL_V + t * HEAD_DIM:COL_V + (t + 1) * HEAD_DIM].astype(v_ref.dtype)


def qkv_heads(qkv, cos2, sin2, rope):
    b, l, w = qkv.shape
    tl = min(l, 256)
    tab = pl.BlockSpec((tl, 2 * HEAD_DIM), lambda i, j: (j, 0))
    hspec = lambda n: pl.BlockSpec((1, n, tl, HEAD_DIM), lambda i, j: (i, 0, j, 0))
    return pl.pallas_call(
        functools.partial(_qkv_heads_kernel, rope=rope),
        grid=(b, l // tl),
        in_specs=[pl.BlockSpec((1, tl, w), lambda i, j: (i, j, 0)), tab, tab],
        out_specs=[hspec(ATT_HEADS), hspec(ATT_KV_HEADS), hspec(ATT_KV_HEADS)],
        out_shape=[jax.ShapeDtypeStruct((b, ATT_HEADS, l, HEAD_DIM), BF16),
                   jax.ShapeDtypeStruct((b, ATT_KV_HEADS, l, HEAD_DIM), BF16),
                   jax.ShapeDtypeStruct((b, ATT_KV_HEADS, l, HEAD_DIM), BF16)],
        compiler_params=_cparams("parallel", "parallel"),
        name="qkv_heads",
    )(qkv, cos2, sin2)


def _attention_kernel(sink_ref, q_ref, kc_ref, vc_ref, *rest, local):
    if local:
        kp_ref, ko_ref, kn_ref, vp_ref, vo_ref, vn_ref, bias_ref, o_ref = rest
    else:
        (o_ref,) = rest
    blk = ATT_BLOCK
    rows = ATT_GROUP * blk
    outs = []
    for g in range(ATT_KV_HEADS):
        q = q_ref[0, g * ATT_GROUP:(g + 1) * ATT_GROUP].reshape(rows, HEAD_DIM)
        if local:
            keys = jnp.concatenate([kc_ref[0, g], kp_ref[0, g], ko_ref[0, g], kn_ref[0, g]], axis=0)
            vals = jnp.concatenate([vc_ref[0, g], vp_ref[0, g], vo_ref[0, g], vn_ref[0, g]], axis=0)
        else:
            keys, vals = kc_ref[0, g], vc_ref[0, g]
        s = lax.dot_general(q, keys, (((1,), (1,)), ((), ())), preferred_element_type=F32)
        r = lax.broadcasted_iota(jnp.int32, (rows, 1), 0)
        if local:
            s = s + jnp.concatenate([bias_ref[0]] * ATT_GROUP, axis=0)
        sink = jnp.zeros((rows, 1), F32)
        for t in range(ATT_GROUP):
            sink = jnp.where(r // blk == t, sink_ref[g * ATT_GROUP + t], sink)
        m = jnp.maximum(jnp.max(s, axis=-1, keepdims=True), sink)
        p = jnp.exp(s - m)
        denom = jnp.sum(p, axis=-1, keepdims=True) + jnp.exp(sink - m)
        o = jnp.dot(p.astype(BF16), vals, preferred_element_type=F32) / denom
        outs += [o[t * blk:(t + 1) * blk] for t in range(ATT_GROUP)]
    o_ref[0] = jnp.concatenate(outs, axis=1).astype(o_ref.dtype)


def attention(q, k, v, kc, vc, sink, local):
    b, _, l, d = q.shape
    c = kc.shape[2]
    nb = l // ATT_BLOCK
    qspec = pl.BlockSpec((1, ATT_HEADS, ATT_BLOCK, d), lambda i, j: (i, 0, j, 0))
    cspec = pl.BlockSpec((1, ATT_KV_HEADS, c, d), lambda i, j: (i, 0, 0, 0))
    in_specs = [pl.BlockSpec(memory_space=pltpu.SMEM), qspec, cspec, cspec]
    args = [sink, q, kc, vc]
    if local:
        kv = lambda f: pl.BlockSpec((1, ATT_KV_HEADS, ATT_BLOCK, d), lambda i, j: (i, 0, f(j), 0))
        band = [kv(lambda j: jnp.maximum(j - 1, 0)), kv(lambda j: j), kv(lambda j: jnp.minimum(j + 1, nb - 1))]
        in_specs += band + band
        args += [k, k, k, v, v, v]
        qpos = jnp.arange(ATT_BLOCK)[:, None]
        kpos = jnp.arange(-ATT_BLOCK, 2 * ATT_BLOCK)[None, :]
        near = jnp.abs(kpos - qpos) <= WINDOW
        variants = [near, near & (kpos >= 0), near & (kpos < ATT_BLOCK), near & (kpos >= 0) & (kpos < ATT_BLOCK)]
        bias = jnp.stack([jnp.concatenate([jnp.zeros((ATT_BLOCK, c), F32), jnp.where(ok, 0.0, NEG_INF)], axis=1)
                          for ok in variants])
        in_specs.append(pl.BlockSpec((1, ATT_BLOCK, c + 3 * ATT_BLOCK),
                                     lambda i, j: ((j == 0) + 2 * (j == nb - 1), 0, 0)))
        args.append(bias)
    return pl.pallas_call(
        functools.partial(_attention_kernel, local=local),
        grid=(b, nb),
        in_specs=in_specs,
        out_specs=pl.BlockSpec((1, ATT_BLOCK, ATT_HEADS * d), lambda i, j: (i, j, 0)),
        out_shape=jax.ShapeDtypeStruct((b, l, ATT_HEADS * d), BF16),
        compiler_params=_cparams("parallel", "parallel"),
        name="window_attention" if local else "context_attention",
    )(*args)


def hyena_filters(L, lp):
    t = jnp.arange(L, dtype=F32)
    tn = t / L
    bands = jnp.arange(1, HY_BANDS + 1, dtype=F32)
    ang = 2.0 * math.pi * tn[:, None] * bands[None, :]
    z = jnp.concatenate([tn[:, None], jnp.cos(ang), jnp.sin(ang)], axis=-1)
    hdn = jnp.sin(lp['hy_freq1'] * (z @ lp['hy_w1'] + lp['hy_b1']))
    hdn = jnp.sin(lp['hy_freq2'] * (hdn @ lp['hy_w2'] + lp['hy_b2']))
    filt = hdn @ lp['hy_w3']
    rate = jnp.linspace(HY_FAST_RATE, HY_SLOW_RATE, HY_WIDTH, dtype=F32)
    tw = jnp.linspace(0.0, 1.0, L, dtype=F32)
    window = jnp.exp(-tw[:, None] * rate[None, :])
    return filt[:, :HY_WIDTH] * window, filt[:, HY_WIDTH:] * window


def _short_conv_kernel(z_ref, zp_ref, zn_ref, w_ref, b_ref, x0_ref, vg_ref, vgb_ref):
    i = pl.program_id(1)
    z = z_ref[0]
    tl, w3 = z.shape
    prev_row = jnp.where(i > 0, zp_ref[0, 7:8, :], 0.0)
    next_row = jnp.where(i < pl.num_programs(1) - 1, zn_ref[0, 0:1, :], 0.0)
    row = lax.broadcasted_iota(jnp.int32, (tl, w3), 0)
    z_prev = jnp.where(row == 0, prev_row, pltpu.roll(z, 1, axis=0))
    z_next = jnp.where(row == tl - 1, next_row, pltpu.roll(z, tl - 1, axis=0))
    y = b_ref[...] + z_prev * w_ref[0:1, :] + z * w_ref[1:2, :] + z_next * w_ref[2:3, :]
    w = w3 // HY_N_PROJ
    x0_ref[0] = y[:, :w]
    vg = y[:, 2 * w:] * y[:, w:2 * w]
    vg_ref[0] = vg
    vgb_ref[0] = vg.astype(BF16)


def hyena_short_conv(z, w, b):
    bsz, l, w3 = z.shape
    tl = min(l, 256)
    nb8 = tl // 8
    last8 = l // 8 - 1
    out = jax.ShapeDtypeStruct((bsz, l, w3 // HY_N_PROJ), F32)
    ospec = pl.BlockSpec((1, tl, w3 // HY_N_PROJ), lambda i, j: (i, j, 0))
    return pl.pallas_call(
        _short_conv_kernel,
        grid=(bsz, l // tl),
        in_specs=[pl.BlockSpec((1, tl, w3), lambda i, j: (i, j, 0)),
                  pl.BlockSpec((1, 8, w3), lambda i, j: (i, jnp.maximum(j * nb8 - 1, 0), 0)),
                  pl.BlockSpec((1, 8, w3), lambda i, j: (i, jnp.minimum((j + 1) * nb8, last8), 0)),
                  pl.BlockSpec((HY_SHORT, w3), lambda i, j: (0, 0)),
                  pl.BlockSpec((1, w3), lambda i, j: (0, 0))],
        out_specs=[ospec, ospec, ospec],
        out_shape=[out, out, jax.ShapeDtypeStruct(out.shape, BF16)],
        compiler_params=_cparams("parallel", "parallel"),
        name="hyena_short_conv",
    )(z, z, z, w, b.reshape(1, w3))


def dft_tables(l):
    n = 2 * l
    split = 64
    t = jnp.arange(l, dtype=jnp.int32)[None, :]
    kh = jnp.arange(l // split, dtype=jnp.int32)[:, None]
    kl = jnp.arange(split, dtype=jnp.int32)[:, None]
    ang_h = (2.0 * math.pi / n) * (((kh * split) * t) % n).astype(F32)
    ang_l = (2.0 * math.pi / n) * ((kl * t) % n).astype(F32)
    ch, sh = jnp.cos(ang_h)[:, None, :], jnp.sin(ang_h)[:, None, :]
    cl, sl = jnp.cos(ang_l)[None, :, :], jnp.sin(ang_l)[None, :, :]
    cos_t = (ch * cl - sh * sl).reshape(l, l).astype(BF16)
    sin_t = (sh * cl + ch * sl).reshape(l, l).astype(BF16)
    return cos_t, sin_t


def _filter_spectrum_kernel(c_ref, s_ref, hp_ref, hm_ref, kre_ref, kim_ref):
    kre_ref[...] = jnp.dot(c_ref[...], hp_ref[...], preferred_element_type=F32)
    kim_ref[...] = -jnp.dot(s_ref[...], hm_ref[...], preferred_element_type=F32)


def filter_spectrum(cos_t, sin_t, h_fwd, h_bwd):
    l, w = h_fwd.shape
    hb0 = h_bwd.at[0].set(0.0)
    tk = 512
    tab = pl.BlockSpec((tk, l), lambda k: (k, 0))
    full = pl.BlockSpec((l, w), lambda k: (0, 0))
    ospec = pl.BlockSpec((tk, w), lambda k: (k, 0))
    out = jax.ShapeDtypeStruct((l, w), F32)
    kre, kim = pl.pallas_call(
        _filter_spectrum_kernel,
        grid=(l // tk,),
        in_specs=[tab, tab, full, full],
        out_specs=[ospec, ospec],
        out_shape=[out, out],
        compiler_params=_cparams("parallel"),
        name="hyena_filter_spectrum",
    )(cos_t, sin_t, (h_fwd + hb0).astype(BF16), (h_fwd - hb0).astype(BF16))
    sign = jnp.where(jnp.arange(l) % 2 == 0, 1.0, -1.0)[:, None]
    k_nyq = jnp.sum(sign * (h_fwd + hb0), axis=0, keepdims=True)
    return kre, kim, k_nyq


def _long_conv_kernel(x_ref, crow_ref, srow_ref, ccol_ref, scol_ref, kre_ref, kim_ref, knyq_ref, y_ref,
                      acc_scr, xn_scr):
    k = pl.program_id(1)
    nk = pl.num_programs(1)
    x = x_ref[0]
    l = x.shape[0]
    tk = crow_ref.shape[0]

    @pl.when(k == 0)
    def _():
        acc_scr[...] = jnp.zeros_like(acc_scr)
        t_idx = lax.broadcasted_iota(jnp.int32, (8, l), 1)
        alt = jnp.where(t_idx % 2 == 0, 1.0, -1.0).astype(BF16)
        xn_scr[...] = jnp.dot(alt, x, preferred_element_type=F32)

    xre = jnp.dot(crow_ref[...], x, preferred_element_type=F32)
    xim = -jnp.dot(srow_ref[...], x, preferred_element_type=F32)
    kre, kim = kre_ref[...], kim_ref[...]
    k_idx = lax.broadcasted_iota(jnp.int32, (tk, 1), 0) + k * tk
    weight = jnp.where(k_idx == 0, 1.0, 2.0)
    yre = ((xre * kre - xim * kim) * weight).astype(BF16)
    nyim = ((xre * kim + xim * kre) * -weight).astype(BF16)
    acc_scr[...] += (jnp.dot(ccol_ref[...], yre, preferred_element_type=F32)
                     + jnp.dot(scol_ref[...], nyim, preferred_element_type=F32))

    @pl.when(k == nk - 1)
    def _():
        t_idx = lax.broadcasted_iota(jnp.int32, (l, 1), 0)
        nyq = xn_scr[0:1, :] * knyq_ref[...]
        y_ref[0] = (acc_scr[...] + jnp.where(t_idx % 2 == 0, nyq, -nyq)) * (0.5 / l)


def long_conv(x, cos_t, sin_t, kre, kim, k_nyq):
    b, l, w = x.shape
    tk = 256
    row_tab = pl.BlockSpec((tk, l), lambda i, k: (k, 0))
    col_tab = pl.BlockSpec((l, tk), lambda i, k: (0, k))
    kspec = pl.BlockSpec((tk, w), lambda i, k: (k, 0))
    return pl.pallas_call(
        _long_conv_kernel,
        grid=(b, l // tk),
        in_specs=[pl.BlockSpec((1, l, w), lambda i, k: (i, 0, 0)), row_tab, row_tab, col_tab, col_tab,
                  kspec, kspec, pl.BlockSpec((1, w), lambda i, k: (0, 0))],
        out_specs=pl.BlockSpec((1, l, w), lambda i, k: (i, 0, 0)),
        out_shape=jax.ShapeDtypeStruct((b, l, w), F32),
        scratch_shapes=[pltpu.VMEM((l, w), F32), pltpu.VMEM((8, w), F32)],
        compiler_params=_cparams("parallel", "arbitrary"),
        name="hyena_long_conv",
    )(x, cos_t, sin_t, cos_t, sin_t, kre, kim, k_nyq)


def _hyena_gate_kernel(y_ref, vg_ref, x0_ref, bias_ref, o_ref):
    o_ref[...] = ((y_ref[...] + vg_ref[...] * bias_ref[...]) * x0_ref[...]).astype(o_ref.dtype)


def hyena_gate(y, vg, x0, bias):
    b, l, w = y.shape
    m = b * l
    tm = 1024
    row = pl.BlockSpec((tm, w), lambda i: (i, 0))
    out = pl.pallas_call(
        _hyena_gate_kernel,
        grid=(m // tm,),
        in_specs=[row, row, row, pl.BlockSpec((1, w), lambda i: (0, 0))],
        out_specs=row,
        out_shape=jax.ShapeDtypeStruct((m, w), BF16),
        compiler_params=_cparams("parallel"),
        name="hyena_gate",
    )(y.reshape(m, w), vg.reshape(m, w), x0.reshape(m, w), bias.reshape(1, w))
    return out.reshape(b, l, w)


def _dense_dft_tables(l):
    n = 2 * l
    ang = 2.0 * np.pi * np.outer(np.arange(n), np.arange(n)) / n
    fwd = np.concatenate([np.cos(ang), -np.sin(ang)], axis=0)
    inv = np.concatenate([np.cos(ang[:l]), -np.sin(ang[:l])], axis=1) / n
    return jnp.asarray(fwd, F32), jnp.asarray(inv, F32)


def _dense_conv_kernel(vg_ref, x0_ref, kern_ref, f_ref, g_ref, bias_ref, o_ref):
    hp = lax.Precision.HIGHEST
    l = vg_ref.shape[1]
    n = 2 * l
    vg = vg_ref[0]
    ks = jnp.dot(f_ref[...], kern_ref[...], preferred_element_type=F32, precision=hp)
    xs = jnp.dot(f_ref[:, :l], vg, preferred_element_type=F32, precision=hp)
    kre, kim, xre, xim = ks[:n], ks[n:], xs[:n], xs[n:]
    ys = jnp.concatenate([xre * kre - xim * kim, xre * kim + xim * kre], axis=0)
    y = jnp.dot(g_ref[...], ys, preferred_element_type=F32, precision=hp)
    o_ref[0] = ((y + vg * bias_ref[...]) * x0_ref[0]).astype(o_ref.dtype)


def hyena_dense_conv(vg, x0, kern, bias):
    bsz, l, w = vg.shape
    fwd, inv = _dense_dft_tables(l)
    row = pl.BlockSpec((1, l, w), lambda i: (i, 0, 0))
    full = lambda a: pl.BlockSpec(a.shape, lambda i: (0, 0))
    return pl.pallas_call(
        _dense_conv_kernel,
        grid=(bsz,),
        in_specs=[row, row, full(kern), full(fwd), full(inv), pl.BlockSpec((1, w), lambda i: (0, 0))],
        out_specs=row,
        out_shape=jax.ShapeDtypeStruct((bsz, l, w), BF16),
        compiler_params=_cparams("parallel"),
        name="hyena_dense_conv",
    )(vg, x0, kern, fwd, inv, bias.reshape(1, w))


HY_DENSE_MAX_LEN = 512


def hyena_mixer(z, lp, tables=None):
    l = z.shape[1]
    x0, vg, vg_bf16 = hyena_short_conv(z, lp['hy_short_w'], lp['hy_short_b'])
    h_fwd, h_bwd = hyena_filters(l, lp)
    if l <= HY_DENSE_MAX_LEN:
        kern = jnp.concatenate([h_fwd, jnp.zeros_like(h_fwd[:1]), h_bwd[1:][::-1]], axis=0)
        return hyena_dense_conv(vg, x0, kern, lp['hy_bias'])
    cos_t, sin_t = tables if tables is not None else dft_tables(l)
    kre, kim, k_nyq = filter_spectrum(cos_t, sin_t, h_fwd, h_bwd)
    y = long_conv(vg_bf16, cos_t, sin_t, kre, kim, k_nyq)
    return hyena_gate(y, vg, x0, lp['hy_bias'])


S5_CHUNK = 16
S5_CK = S5_CHUNK * S5_GROUP


def _s5_powers(lp, d):
    lam = lax.complex(jnp.minimum(lp['s5_a_re'][d], -1e-4), lp['s5_a_im'][d])
    dt = jnp.exp(lp['s5_log_dt'][d])[:, None]
    n = jnp.arange(S5_CHUNK + 1, dtype=F32)[:, None, None]
    apow = jnp.exp((lam * dt)[None] * n)
    b = lax.complex(lp['s5_b_re'][d], lp['s5_b_im'][d])
    bbar = ((apow[1] - 1.0) / lam)[..., None] * b
    cmat = lax.complex(lp['s5_c_re'][d], lp['s5_c_im'][d])
    return apow, bbar, cmat


def s5_prepare(lp):
    q = S5_CHUNK
    g_n, p_n = S5_GROUPS, S5_STATE
    hp = lax.Precision.HIGHEST
    ap_f, bb_f, c_f = _s5_powers(lp, 0)
    ap_b, bb_b, c_b = _s5_powers(lp, 1)
    k_f = jnp.einsum('gop,tgp,gpi->gtio', c_f, ap_f[:q], bb_f, precision=hp).real
    k_b = jnp.einsum('gop,tgp,gpi->gtio', c_b, ap_b[:q], bb_b, precision=hp).real
    zero = jnp.zeros_like(k_f)
    fwd_ext = jnp.concatenate([zero, k_f], axis=1)
    bwd_ext = jnp.concatenate([k_b[:, ::-1], zero], axis=1)
    m_f = jnp.stack([fwd_ext[:, q - sp:2 * q - sp] for sp in range(q)], axis=1)
    m_b = jnp.stack([bwd_ext[:, q - 1 - sp:2 * q - 1 - sp] for sp in range(q)], axis=1)
    m_intra = (m_f + m_b).transpose(0, 1, 3, 2, 4).reshape(g_n, S5_CK, S5_CK)

    ws_f = ap_f[q - 1 - jnp.arange(q)][:, :, :, None] * bb_f[None]
    ws_b = ap_b[jnp.arange(q)][:, :, :, None] * bb_b[None]
    ws = jnp.stack([ws_f.real, ws_f.imag, ws_b.real, ws_b.imag], axis=0)
    ws = ws.transpose(2, 1, 4, 0, 3).reshape(g_n, S5_CK, 4, p_n)
    w_state = jnp.zeros((g_n // 2, 2, S5_CK, 4, 2, p_n), F32)
    w_state = w_state.at[:, 0, :, :, 0].set(ws[0::2]).at[:, 1, :, :, 1].set(ws[1::2])
    w_state = w_state.reshape(g_n // 2, 2, S5_CK, 8 * p_n)

    ca_f = c_f[None] * ap_f[1:q + 1][:, :, None, :]
    ca_b = c_b[None] * ap_b[q - jnp.arange(q)][:, :, None, :]
    wo = jnp.stack([ca_f.real, -ca_f.imag, ca_b.real, -ca_b.imag], axis=0)
    wo = wo.transpose(2, 0, 4, 1, 3).reshape(g_n, 4, p_n, S5_CK)
    w_out = jnp.zeros((g_n, 4, 2, p_n, S5_CK), F32)
    w_out = w_out.at[0::2, :, 0].set(wo[0::2]).at[1::2, :, 1].set(wo[1::2])
    w_out = w_out.reshape(g_n, 8 * p_n, S5_CK)

    def flat(z):
        return jnp.stack([z.real.reshape(-1), z.imag.reshape(-1)], axis=0)

    return m_intra.astype(BF16), w_state.astype(BF16), w_out.astype(BF16), flat(ap_f[q]), flat(ap_b[q])


def _s5_state_kernel(u_ref, w_ref, fre_ref, fim_ref, bre_ref, bim_ref):
    o = (jnp.dot(u_ref[0], w_ref[0, 0], preferred_element_type=F32)
         + jnp.dot(u_ref[1], w_ref[0, 1], preferred_element_type=F32))
    n = fre_ref.shape[2]
    for comp, ref in enumerate((fre_ref, fim_ref, bre_ref, bim_ref)):
        ref[...] = o[:, comp * n:(comp + 1) * n].reshape(ref.shape)


def s5_chunk_states(u_t, w_state, rows):
    g_n, r, ck = u_t.shape
    pw = 2 * S5_STATE
    out = jax.ShapeDtypeStruct((r // rows, rows, g_n * S5_STATE), F32)
    ospec = pl.BlockSpec((r // rows, rows, pw), lambda i: (0, 0, i))
    return pl.pallas_call(
        _s5_state_kernel,
        grid=(g_n // 2,),
        in_specs=[pl.BlockSpec((2, r, ck), lambda i: (i, 0, 0)),
                  pl.BlockSpec((1, 2, ck, 4 * pw), lambda i: (i, 0, 0, 0))],
        out_specs=[ospec] * 4,
        out_shape=[out] * 4,
        compiler_params=_cparams("parallel"),
        name="s5_chunk_states",
    )(u_t, w_state)


def _s5_carry_kernel(sre_ref, sim_ref, a_ref, hre_ref, him_ref, *, n_ctx_chunks, reverse):
    n_chunks = sre_ref.shape[0]
    ar = a_ref[0:1, :]
    ai = a_ref[1:2, :]
    zero = jnp.zeros(sre_ref.shape[1:], F32)

    def body(n, carry):
        hre, him = carry
        if reverse:
            c = jnp.where(n < n_ctx_chunks, n_ctx_chunks - 1 - n, n_chunks + n_ctx_chunks - 1 - n)
        else:
            c = n
        hre_ref[c] = hre
        him_ref[c] = him
        return ar * hre - ai * him + sre_ref[c], ar * him + ai * hre + sim_ref[c]

    lax.fori_loop(0, n_chunks, body, (zero, zero))


def s5_carry(sre, sim, a_chunk, n_ctx_chunks, reverse):
    nc, rows, w = sre.shape
    tl = 512
    spec = pl.BlockSpec((nc, rows, tl), lambda i: (0, 0, i))
    out = jax.ShapeDtypeStruct((nc, rows, w), F32)
    return pl.pallas_call(
        functools.partial(_s5_carry_kernel, n_ctx_chunks=n_ctx_chunks, reverse=reverse),
        grid=(w // tl,),
        in_specs=[spec, spec, pl.BlockSpec((2, tl), lambda i: (0, i))],
        out_specs=[spec, spec],
        out_shape=[out, out],
        compiler_params=_cparams("parallel"),
        name="s5_carry_bwd" if reverse else "s5_carry_fwd",
    )(sre, sim, a_chunk)


def _s5_output_kernel(u_ref, m_ref, fre_ref, fim_ref, bre_ref, bim_ref, wo_ref, y_ref):
    nc, rows, pw = fre_ref.shape
    y = jnp.dot(u_ref[0], m_ref[0], preferred_element_type=F32)
    for comp, h_ref in enumerate((fre_ref, fim_ref, bre_ref, bim_ref)):
        y = y + jnp.dot(h_ref[...].reshape(nc * rows, pw).astype(BF16), wo_ref[0, comp * pw:(comp + 1) * pw, :],
                        preferred_element_type=F32)
    y_ref[0] = y


def s5_chunk_outputs(u_t, m_intra, carried, w_out):
    g_n, r, ck = u_t.shape
    pw = 2 * S5_STATE
    nc, rows, _ = carried[0].shape
    hspec = pl.BlockSpec((nc, rows, pw), lambda i: (0, 0, i // 2))
    return pl.pallas_call(
        _s5_output_kernel,
        grid=(g_n,),
        in_specs=[pl.BlockSpec((1, r, ck), lambda i: (i, 0, 0)),
                  pl.BlockSpec((1, ck, ck), lambda i: (i, 0, 0)),
                  hspec, hspec, hspec, hspec,
                  pl.BlockSpec((1, 4 * pw, ck), lambda i: (i, 0, 0))],
        out_specs=pl.BlockSpec((1, r, ck), lambda i: (i, 0, 0)),
        out_shape=jax.ShapeDtypeStruct((g_n, r, ck), F32),
        compiler_params=_cparams("parallel"),
        name="s5_chunk_outputs",
    )(u_t, m_intra, *carried, w_out)


def _s5_readout_kernel(y_ref, u_ref, d_ref, w_ref, b_ref, o_ref):
    y = _gelu_tanh(y_ref[...] + d_ref[...] * u_ref[...])
    z = jnp.dot(y.astype(BF16), w_ref[...], preferred_element_type=F32) + b_ref[...]
    o_ref[...] = (y * jax.nn.sigmoid(z)).astype(o_ref.dtype)


def s5_readout(y, u, d, glu_w, glu_b):
    m, w = y.shape
    tm = min(m, 512)
    row = pl.BlockSpec((tm, w), lambda i: (i, 0))
    vec = pl.BlockSpec((1, w), lambda i: (0, 0))
    return pl.pallas_call(
        _s5_readout_kernel,
        grid=(m // tm,),
        in_specs=[row, row, vec, pl.BlockSpec((w, w), lambda i: (0, 0)), vec],
        out_specs=row,
        out_shape=jax.ShapeDtypeStruct((m, w), BF16),
        compiler_params=_cparams("parallel"),
        name="s5_readout",
    )(y, u, d.reshape(1, w), glu_w, glu_b.reshape(1, w))


def s5_mixer(u, uc, lp, glu_w):
    b, l, w = u.shape
    c = uc.shape[1]
    q = S5_CHUNK
    n_all = (c + l) // q
    m_intra, w_state, w_out, a_f, a_b = s5_prepare(lp)
    u_all = jnp.concatenate([uc, u], axis=1)
    u_t = u_all.astype(BF16).reshape(b, n_all, q, S5_GROUPS, S5_GROUP)
    u_t = u_t.transpose(3, 1, 0, 2, 4).reshape(S5_GROUPS, n_all * b, S5_CK)
    fre, fim, bre, bim = s5_chunk_states(u_t, w_state, b)
    hf = s5_carry(fre, fim, a_f, c // q, False)
    hb = s5_carry(bre, bim, a_b, c // q, True)
    y_t = s5_chunk_outputs(u_t, m_intra, (*hf, *hb), w_out)
    y_all = y_t.reshape(S5_GROUPS, n_all, b, q, S5_GROUP).transpose(2, 1, 3, 0, 4).reshape(b * (c + l), w)
    out = s5_readout(y_all, u_all.reshape(b * (c + l), w), lp['s5_d'], glu_w, lp['s5_glu_b'])
    out = out.reshape(b, c + l, w)
    return out[:, c:], out[:, :c]


def kernel(x, c, ctx, c_ctx, mod_w, mod_b, norm1_g, norm2_g, in_w, gate_b, attn_sink, hy_short_w, hy_short_b,
           hy_w1, hy_b1, hy_freq1, hy_w2, hy_b2, hy_freq2, hy_w3, hy_bias, s5_a_re, s5_a_im, s5_log_dt,
           s5_b_re, s5_b_im, s5_c_re, s5_c_im, s5_d, s5_glu_w, s5_glu_b, br_attn_w, br_hyena_w, br_s5_w,
           out_w, peer_wq, peer_keys, peer_u, peer_v, final_g):
    B, L, D = x.shape
    C = ctx.shape[1]
    depth = mod_w.shape[0]
    rope_cos, rope_sin = rope_tables(L)
    conv_tables = dft_tables(L) if L > HY_DENSE_MAX_LEN else None

    cvec = jnp.zeros((8, D), F32).at[:B].set(c).at[B].set(c_ctx)
    mod_all = modulation(cvec, mod_w, mod_b)

    xc = ctx
    for layer in range(depth):
        need_ctx = layer < depth - 1
        lp = dict(attn_sink=attn_sink[layer],
                  hy_short_w=hy_short_w[layer], hy_short_b=hy_short_b[layer],
                  hy_w1=hy_w1[layer], hy_b1=hy_b1[layer], hy_freq1=hy_freq1[layer],
                  hy_w2=hy_w2[layer], hy_b2=hy_b2[layer], hy_freq2=hy_freq2[layer],
                  hy_w3=hy_w3[layer], hy_bias=hy_bias[layer],
                  s5_a_re=s5_a_re[layer], s5_a_im=s5_a_im[layer], s5_log_dt=s5_log_dt[layer],
                  s5_b_re=s5_b_re[layer], s5_b_im=s5_b_im[layer],
                  s5_c_re=s5_c_re[layer], s5_c_im=s5_c_im[layer], s5_d=s5_d[layer],
                  s5_glu_w=s5_glu_w[layer], s5_glu_b=s5_glu_b[layer])
        mod = mod_all[layer]
        ml = [mod[:B, i * D:(i + 1) * D] for i in range(N_MOD)]
        mc = [mod[B:B + 1, i * D:(i + 1) * D] for i in range(N_MOD)]

        w_in = in_w[layer].astype(BF16)
        w_parts = [w_in[:, COL_Q:COL_S5], w_in[:, COL_S5:COL_HY], w_in[:, COL_HY:COL_GATE], w_in[:, COL_GATE:]]
        wa = br_attn_w[layer].astype(BF16)
        wh = br_hyena_w[layer].astype(BF16)
        ws = br_s5_w[layer].astype(BF16)
        wo = out_w[layer].astype(BF16)
        wqt = peer_wq[layer].T.astype(BF16)
        keys = peer_keys[layer].reshape(2 * PEER_HEADS, PEER_KEYS, PEER_HALF).astype(BF16)
        u_tab = peer_u[layer].astype(BF16)
        vt_tab = peer_v[layer].T.astype(BF16)

        h = norm_modulate(x, norm1_g[layer], ml[0], ml[1])
        hc = norm_modulate(xc, norm1_g[layer], mc[0], mc[1])
        qkv, u_s5, z_hy, gates = in_projection(h.reshape(B * L, D), w_parts, gate_b[layer])
        qkv_c, uc_s5, zc_hy, gates_c = in_projection(hc.reshape(B * C, D), w_parts, gate_b[layer])
        qkv = qkv.reshape(B, L, -1)
        qkv_c = qkv_c.reshape(B, C, -1)
        u_s5 = u_s5.reshape(B, L, -1)
        uc_s5 = uc_s5.reshape(B, C, -1)

        q, k, v = qkv_heads(qkv, rope_cos, rope_sin, True)
        qc, kc, vc = qkv_heads(qkv_c, rope_cos, rope_sin, False)
        att = attention(q, k, v, kc, vc, lp['attn_sink'], True)
        hy = hyena_mixer(z_hy.reshape(B, L, -1), lp, conv_tables)
        s5, s5c = s5_mixer(u_s5, uc_s5, lp, s5_glu_w[layer].astype(BF16))

        x = branch_merge(att, hy, s5, gates.reshape(B, L, -1), wa, wh, ws, wo, x, ml[2])
        x = peer_block(x, norm2_g[layer], ml[3], ml[4], ml[5], wqt, keys, u_tab, vt_tab)

        if need_ctx:
            att_c = attention(qc, None, None, kc, vc, lp['attn_sink'], False)
            hy_c = hyena_mixer(zc_hy.reshape(B, C, -1), lp)
            xc = branch_merge(att_c, hy_c, s5c, gates_c.reshape(B, C, -1), wa, wh, ws, wo, xc, mc[2])
            xc = peer_block(xc, norm2_g[layer], mc[3], mc[4], mc[5], wqt, keys, u_tab, vt_tab)
    return final_norm(x, final_g)
```

```python
import functools
import math

import jax
import jax.numpy as jnp
import numpy as np
from jax import lax
from jax.experimental import pallas as pl
from jax.experimental.pallas import tpu as pltpu

F32 = jnp.float32
BF16 = jnp.bfloat16

NORM_EPS = 1e-6
NEG_INF = -1e30
N_MOD = 6

GRID_W = 64
ATT_HEADS = 8
ATT_KV_HEADS = 2
ATT_GROUP = ATT_HEADS // ATT_KV_HEADS
HEAD_DIM = 64
ATT_WIDTH = ATT_HEADS * HEAD_DIM
KV_WIDTH = ATT_KV_HEADS * HEAD_DIM
WINDOW = 128
ATT_BLOCK = 128
ROPE_BASE = 10000.0
ROPE_PAIRS_PER_AXIS = HEAD_DIM // 4

HY_WIDTH = 512
HY_N_PROJ = 3
HY_SHORT = 3
HY_BANDS = 16
HY_DECAY_TARGET = 1e-2
HY_FAST_RATE = -math.log(HY_DECAY_TARGET) / 0.3
HY_SLOW_RATE = -math.log(HY_DECAY_TARGET) / 1.5

S5_WIDTH = 512
S5_GROUP = 16
S5_GROUPS = S5_WIDTH // S5_GROUP
S5_STATE = 64

N_BRANCH = 3

PEER_HEADS = 8
PEER_KEYS = 128
PEER_HALF = 128
PEER_TOPK = 16

COL_Q = 0
COL_K = COL_Q + ATT_WIDTH
COL_V = COL_K + KV_WIDTH
COL_S5 = COL_V + KV_WIDTH
COL_HY = COL_S5 + S5_WIDTH
COL_GATE = COL_HY + HY_N_PROJ * HY_WIDTH

VMEM_LIMIT_V7X = 56 * 1024 * 1024


def _cparams(*sem):
    return pltpu.CompilerParams(dimension_semantics=sem, vmem_limit_bytes=VMEM_LIMIT_V7X)


def _gelu_tanh(x):
    return 0.5 * x * (1.0 + jnp.tanh(math.sqrt(2.0 / math.pi) * (x + 0.044715 * (x * x * x))))


def _mod_kernel(s_ref, w_ref, b_ref, o_ref):
    s = s_ref[...]
    s = (s * jax.nn.sigmoid(s)).astype(BF16)
    o_ref[0] = jnp.dot(s, w_ref[0].astype(BF16), preferred_element_type=F32) + b_ref[0]


def modulation(cvec, mod_w, mod_b):
    depth, d, n = mod_w.shape
    tn = 1024
    return pl.pallas_call(
        _mod_kernel,
        grid=(depth, n // tn),
        in_specs=[pl.BlockSpec((8, d), lambda l, j: (0, 0)),
                  pl.BlockSpec((1, d, tn), lambda l, j: (l, 0, j)),
                  pl.BlockSpec((1, 1, tn), lambda l, j: (l, 0, j))],
        out_specs=pl.BlockSpec((1, 8, tn), lambda l, j: (l, 0, j)),
        out_shape=jax.ShapeDtypeStruct((depth, 8, n), F32),
        compiler_params=_cparams("parallel", "parallel"),
        name="adaln_mod",
    )(cvec, mod_w, mod_b.reshape(depth, 1, n))


def _norm_mod_kernel(x_ref, g_ref, sh_ref, sc_ref, o_ref, *, transposed):
    x = x_ref[0]
    y = x * lax.rsqrt(jnp.mean(x * x, axis=-1, keepdims=True) + NORM_EPS)
    y = y * g_ref[...]
    y = y * (1.0 + sc_ref[0]) + sh_ref[0]
    if transposed:
        o_ref[...] = y.T.astype(o_ref.dtype)
    else:
        o_ref[0] = y.astype(o_ref.dtype)


def norm_modulate(x, g, shift, scale, transposed=False):
    b, l, d = x.shape
    tl = min(l, 512)
    nt = l // tl
    per_batch = shift.shape[0] == b and b > 1
    mod_map = (lambda i, j: (i, 0, 0)) if per_batch else (lambda i, j: (0, 0, 0))
    if transposed:
        out_spec = pl.BlockSpec((d, tl), lambda i, j: (0, i * nt + j))
        out_shape = jax.ShapeDtypeStruct((d, b * l), BF16)
    else:
        out_spec = pl.BlockSpec((1, tl, d), lambda i, j: (i, j, 0))
        out_shape = jax.ShapeDtypeStruct((b, l, d), BF16)
    return pl.pallas_call(
        functools.partial(_norm_mod_kernel, transposed=transposed),
        grid=(b, nt),
        in_specs=[pl.BlockSpec((1, tl, d), lambda i, j: (i, j, 0)),
                  pl.BlockSpec((1, d), lambda i, j: (0, 0)),
                  pl.BlockSpec((1, 1, d), mod_map),
                  pl.BlockSpec((1, 1, d), mod_map)],
        out_specs=out_spec,
        out_shape=out_shape,
        compiler_params=_cparams("parallel", "parallel"),
        name="norm_modulate_t" if transposed else "norm_modulate",
    )(x, g.reshape(1, d), shift.reshape(-1, 1, d), scale.reshape(-1, 1, d))


def _final_norm_kernel(x_ref, g_ref, o_ref):
    x = x_ref[...]
    y = x * lax.rsqrt(jnp.mean(x * x, axis=-1, keepdims=True) + NORM_EPS)
    o_ref[...] = y * g_ref[...]


def final_norm(x, g):
    b, l, d = x.shape
    x2 = x.reshape(b * l, d)
    tm = 512
    out = pl.pallas_call(
        _final_norm_kernel,
        grid=(b * l // tm,),
        in_specs=[pl.BlockSpec((tm, d), lambda i: (i, 0)), pl.BlockSpec((1, d), lambda i: (0, 0))],
        out_specs=pl.BlockSpec((tm, d), lambda i: (i, 0)),
        out_shape=jax.ShapeDtypeStruct((b * l, d), F32),
        compiler_params=_cparams("parallel"),
        name="final_norm",
    )(x2, g.reshape(1, d))
    return out.reshape(b, l, d)


def _inproj_kernel(h_ref, wqkv_ref, ws5_ref, why_ref, wg_ref, gb_ref, oqkv_ref, os5_ref, ohy_ref, og_ref):
    h = h_ref[...]
    oqkv_ref[...] = jnp.dot(h, wqkv_ref[...], preferred_element_type=F32)
    os5_ref[...] = jnp.dot(h, ws5_ref[...], preferred_element_type=F32)
    ohy_ref[...] = jnp.dot(h, why_ref[...], preferred_element_type=F32)
    og_ref[...] = jnp.dot(h, wg_ref[...], preferred_element_type=F32) + gb_ref[...]


def in_projection(h, w_parts, gate_b):
    m, d = h.shape
    tm = 256
    widths = [w.shape[1] for w in w_parts]
    w_specs = [pl.BlockSpec((d, n), lambda i: (0, 0)) for n in widths]
    return pl.pallas_call(
        _inproj_kernel,
        grid=(m // tm,),
        in_specs=[pl.BlockSpec((tm, d), lambda i: (i, 0))] + w_specs
                 + [pl.BlockSpec((1, widths[3]), lambda i: (0, 0))],
        out_specs=[pl.BlockSpec((tm, n), lambda i: (i, 0)) for n in widths],
        out_shape=[jax.ShapeDtypeStruct((m, n), F32) for n in widths],
        compiler_params=_cparams("parallel"),
        name="in_projection",
    )(h, *w_parts, gate_b.reshape(1, -1))


def _merge_kernel(att_ref, hy_ref, s5_ref, gl_ref, wa_ref, wh_ref, ws_ref, wo_ref, x_ref, g1_ref, o_ref):
    d = wo_ref.shape[0]
    g = jax.nn.sigmoid(gl_ref[0])
    m = g[:, :d] * jnp.dot(att_ref[0], wa_ref[...], preferred_element_type=F32)
    m = m + g[:, d:2 * d] * jnp.dot(hy_ref[0], wh_ref[...], preferred_element_type=F32)
    m = m + g[:, 2 * d:] * jnp.dot(s5_ref[0], ws_ref[...], preferred_element_type=F32)
    y = jnp.dot(m.astype(BF16), wo_ref[...], preferred_element_type=F32)
    o_ref[0] = x_ref[0] + g1_ref[0] * y


def branch_merge(att, hy, s5, gate_logits, wa, wh, ws, wo, x, g1):
    b, l, d = x.shape
    tl = min(l, 256)
    wd = att.shape[-1]
    per_batch = g1.shape[0] == b and b > 1
    g_map = (lambda i, j: (i, 0, 0)) if per_batch else (lambda i, j: (0, 0, 0))
    row = lambda n: pl.BlockSpec((1, tl, n), lambda i, j: (i, j, 0))
    full = lambda a: pl.BlockSpec(a.shape, lambda i, j: (0, 0))
    return pl.pallas_call(
        _merge_kernel,
        grid=(b, l // tl),
        in_specs=[row(wd), row(wd), row(wd), row(N_BRANCH * d), full(wa), full(wh), full(ws), full(wo),
                  row(d), pl.BlockSpec((1, 1, d), g_map)],
        out_specs=row(d),
        out_shape=jax.ShapeDtypeStruct((b, l, d), F32),
        compiler_params=_cparams("parallel", "parallel"),
        name="branch_merge",
    )(att, hy, s5, gate_logits, wa, wh, ws, wo, x, g1.reshape(-1, 1, d))


def _topk_rows(s, k):
    rows = []
    for _ in range(k):
        m = jnp.max(s, axis=0, keepdims=True)
        rows.append(m)
        s = jnp.where(s >= m, NEG_INF, s)
    return rows


LANES = 128


def _peer_select_kernel(ht_ref, wqt_ref, keys_ref, rank_ref, cnt_ref, e1_ref, e2_ref, qt_scr, cand_scr):
    nh = rank_ref.shape[0]
    nk = PEER_KEYS
    pairs = [(p, q) for p in range(PEER_TOPK) for q in range(PEER_TOPK) if (p + 1) * (q + 1) <= PEER_TOPK]

    def project(hh):
        r0 = pl.multiple_of(hh * 2 * nk, 2 * nk)
        qt_scr[hh % 2] = jnp.dot(wqt_ref[pl.ds(r0, 2 * nk), :], ht_ref[...],
                                 preferred_element_type=F32).astype(BF16)

    project(0)

    def head(hh, carry):
        qt = qt_scr[hh % 2]
        s1 = jnp.dot(keys_ref[2 * hh], qt[:nk], preferred_element_type=F32)
        s2 = jnp.dot(keys_ref[2 * hh + 1], qt[nk:], preferred_element_type=F32)
        project(jnp.minimum(hh + 1, nh - 1))
        for tb in range(s1.shape[1] // LANES):
            lanes = slice(tb * LANES, (tb + 1) * LANES)
            select_block(hh, tb, lanes, s1[:, lanes], s2[:, lanes])
        return carry

    def select_block(hh, tb, lanes, s1, s2):
        a = _topk_rows(s1, PEER_TOPK)
        b = []
        rest = s2
        rank = jnp.full(s2.shape, float(PEER_TOPK), F32)
        for r in range(PEER_TOPK):
            m = jnp.max(rest, axis=0, keepdims=True)
            b.append(m)
            hit = rest >= m
            rank = jnp.where(hit, float(r), rank)
            rest = jnp.where(hit, NEG_INF, rest)
        cand_scr[:, lanes] = jnp.full((cand_scr.shape[0], LANES), NEG_INF, F32)
        for r, (p, q) in enumerate(pairs):
            cand_scr[r:r + 1, lanes] = a[p] + b[q]
        cand = cand_scr[:, lanes]
        best = _topk_rows(cand, PEER_TOPK)
        top = a[0] + b[0]
        z = jnp.exp(best[0] - top)
        for r in range(1, PEER_TOPK):
            z = z + jnp.exp(best[r] - top)
        tau = best[PEER_TOPK - 1]
        reached = jnp.where(cand >= tau, 1.0, 0.0)
        cnt = jnp.zeros(s1.shape, F32)
        r = 0
        for p in range(PEER_TOPK):
            n_q = sum(1 for pp, _ in pairs if pp == p)
            cnt_p = reached[r:r + 1]
            for rr in range(r + 1, r + n_q):
                cnt_p = cnt_p + reached[rr:rr + 1]
            r += n_q
            cnt = jnp.where(s1 == a[p], cnt_p, cnt)
        rank_ref[hh, :, lanes] = rank.astype(BF16)
        cnt_ref[hh, tb] = cnt
        e1_ref[hh, tb] = 0.5 * jnp.exp(s1 - a[0])
        e2_ref[hh, :, lanes] = (jnp.exp(s2 - b[0]) / z).astype(BF16)

    lax.fori_loop(0, nh, head, 0)


def peer_select(ht, wqt, keys, tt):
    d, m = ht.shape
    nh = keys.shape[0] // 2
    out = jax.ShapeDtypeStruct((nh, PEER_KEYS, m), BF16)
    out32 = jax.ShapeDtypeStruct((nh, m // LANES, PEER_KEYS, LANES), F32)
    ospec = pl.BlockSpec((nh, PEER_KEYS, tt), lambda i: (0, 0, i))
    ospec32 = pl.BlockSpec((nh, tt // LANES, PEER_KEYS, LANES), lambda i: (0, i, 0, 0))
    return pl.pallas_call(
        _peer_select_kernel,
        grid=(m // tt,),
        in_specs=[pl.BlockSpec((d, tt), lambda i: (0, i)),
                  pl.BlockSpec(wqt.shape, lambda i: (0, 0)),
                  pl.BlockSpec(keys.shape, lambda i: (0, 0, 0))],
        out_specs=[ospec, ospec32, ospec32, ospec],
        out_shape=[out, out32, out32, out],
        scratch_shapes=[pltpu.VMEM((2, 2 * PEER_KEYS, tt), BF16), pltpu.VMEM((56, tt), F32)],
        compiler_params=_cparams("parallel"),
        name="peer_select",
    )(ht, wqt, keys)


PEER_PAIR = 2 * PEER_KEYS
PEER_LOOKAHEAD = 2


def _half_gate_gelu(half_w, x):
    c = math.sqrt(2.0 / math.pi)
    p = half_w * x
    return p + p * jnp.tanh(x * (c + (c * 0.044715) * (x * x)))


def _key_row(ref, hh, ii):
    blocks = [jnp.broadcast_to(ref[hh, tb, ii:ii + 1, :], (PEER_KEYS, LANES)) for tb in range(ref.shape[1])]
    return jnp.concatenate(blocks, axis=1).astype(BF16)


def _peer_expert_kernel(ht_ref, u_ref, vt_ref, rank_ref, cnt_ref, e1_ref, e2_ref, x_ref, g2_ref, o_ref,
                        acc_scr, wa_scr):
    k = pl.program_id(1)
    nh = rank_ref.shape[0]

    @pl.when(k == 0)
    def _():
        acc_scr[...] = jnp.zeros_like(acc_scr)

    ht = ht_ref[...]
    n_pairs = u_ref.shape[0] // PEER_PAIR

    def cols(p):
        return slice(p * PEER_PAIR, (p + 1) * PEER_PAIR)

    def first_matmul(p):
        return jnp.dot(u_ref[cols(p), :], ht, preferred_element_type=F32)

    def gated(p, act):
        w_rows = []
        for r in range(2):
            ii = 2 * p + r
            w = None
            for hh in range(nh):
                selected = rank_ref[hh] < _key_row(cnt_ref, hh, ii)
                gate = jnp.where(selected, e2_ref[hh], jnp.zeros((), BF16)) * _key_row(e1_ref, hh, ii)
                w = gate if w is None else w + gate
            w_rows.append(w)
        half_w = jnp.concatenate(w_rows, axis=0)
        return _half_gate_gelu(half_w, act.astype(BF16))

    acts = {q: first_matmul(q) for q in range(min(PEER_LOOKAHEAD, n_pairs))}
    part = None
    for p in range(n_pairs + 1):
        if p + PEER_LOOKAHEAD < n_pairs:
            acts[p + PEER_LOOKAHEAD] = first_matmul(p + PEER_LOOKAHEAD)
        if p < n_pairs:
            wa_scr[p] = gated(p, acts.pop(p))
        if p >= 1:
            d = jnp.dot(vt_ref[:, cols(p - 1)], wa_scr[p - 1], preferred_element_type=F32)
            part = d if part is None else part + d
    acc_scr[...] += part

    @pl.when(k == pl.num_programs(1) - 1)
    def _():
        o_ref[...] = x_ref[...] + g2_ref[0] * acc_scr[...].T


def peer_experts(ht, u_tab, vt_tab, sel, x, g2, tokens_per_batch, tt, ec):
    d, m = ht.shape
    e = u_tab.shape[0]
    nh = sel[0].shape[0]
    rows_i = ec // PEER_KEYS
    full_keys = pl.BlockSpec((nh, PEER_KEYS, tt), lambda i, k: (0, 0, i))
    chunk_keys = pl.BlockSpec((nh, tt // LANES, rows_i, LANES), lambda i, k: (0, i, k, 0))
    tiles_per_batch = tokens_per_batch // tt
    if g2.shape[0] > 1:
        g_map = lambda i, k: (i // tiles_per_batch, 0, 0)
    else:
        g_map = lambda i, k: (0, 0, 0)
    rank, cnt, e1, e2 = sel
    return pl.pallas_call(
        _peer_expert_kernel,
        grid=(m // tt, e // ec),
        in_specs=[pl.BlockSpec((d, tt), lambda i, k: (0, i)),
                  pl.BlockSpec((ec, d), lambda i, k: (k, 0)),
                  pl.BlockSpec((d, ec), lambda i, k: (0, k)),
                  full_keys, chunk_keys, chunk_keys, full_keys,
                  pl.BlockSpec((tt, d), lambda i, k: (i, 0)),
                  pl.BlockSpec((1, 1, d), g_map)],
        out_specs=pl.BlockSpec((tt, d), lambda i, k: (i, 0)),
        out_shape=jax.ShapeDtypeStruct((m, d), F32),
        scratch_shapes=[pltpu.VMEM((d, tt), F32), pltpu.VMEM((ec // PEER_PAIR, PEER_PAIR, tt), BF16)],
        compiler_params=_cparams("parallel", "arbitrary"),
        name="peer_experts",
    )(ht, u_tab, vt_tab, rank, cnt, e1, e2, x, g2.reshape(-1, 1, d))


def peer_block(x, g_norm, shift, scale, gate, wqt, keys, u_tab, vt_tab):
    b, l, d = x.shape
    ht = norm_modulate(x, g_norm, shift, scale, transposed=True)
    tt = min(l, 512)
    sel = peer_select(ht, wqt, keys, tt)
    out = peer_experts(ht, u_tab, vt_tab, sel, x.reshape(b * l, d), gate, l, tt, 2048)
    return out.reshape(b, l, d)


def rope_tables(l):
    rows_n = l // GRID_W
    row = jnp.repeat(jnp.arange(rows_n), GRID_W).astype(F32)
    col = jnp.tile(jnp.arange(GRID_W), rows_n).astype(F32)
    inv = jnp.power(ROPE_BASE, -jnp.arange(ROPE_PAIRS_PER_AXIS, dtype=F32) / ROPE_PAIRS_PER_AXIS)
    ang = jnp.concatenate([row[:, None] * inv, col[:, None] * inv], axis=-1)
    c, s = jnp.cos(ang), jnp.sin(ang)
    return jnp.concatenate([c, c, c, c], axis=-1), jnp.concatenate([-s, s, -s, s], axis=-1)


def _rope_pair(x, cos2, sin2):
    half = HEAD_DIM // 2
    lane = lax.broadcasted_iota(jnp.int32, x.shape, 1)
    swapped = jnp.where(lane % HEAD_DIM < half, pltpu.roll(x, 2 * HEAD_DIM - half, axis=1),
                        pltpu.roll(x, half, axis=1))
    return x * cos2 + swapped * sin2


def _qkv_heads_kernel(qkv_ref, cos_ref, sin_ref, q_ref, k_ref, v_ref, *, rope):
    scale = HEAD_DIM ** -0.5
    x = qkv_ref[0]
    pair = 2 * HEAD_DIM
    for j in range((ATT_WIDTH + KV_WIDTH) // pair):
        blk = x[:, j * pair:(j + 1) * pair]
        if rope:
            blk = _rope_pair(blk, cos_ref[...], sin_ref[...])
        for t in range(2):
            head = blk[:, t * HEAD_DIM:(t + 1) * HEAD_DIM]
            h = 2 * j + t
            if h < ATT_HEADS:
                q_ref[0, h] = (head * scale).astype(q_ref.dtype)
            else:
                k_ref[0, h - ATT_HEADS] = head.astype(k_ref.dtype)
    for t in range(ATT_KV_HEADS):
        v_ref[0, t] = x[:, COL_V + t * HEAD_DIM:COL_V + (t + 1) * HEAD_DIM].astype(v_ref.dtype)


def qkv_heads(qkv, cos2, sin2, rope):
    b, l, w = qkv.shape
    tl = min(l, 256)
    tab = pl.BlockSpec((tl, 2 * HEAD_DIM), lambda i, j: (j, 0))
    hspec = lambda n: pl.BlockSpec((1, n, tl, HEAD_DIM), lambda i, j: (i, 0, j, 0))
    return pl.pallas_call(
        functools.partial(_qkv_heads_kernel, rope=rope),
        grid=(b, l // tl),
        in_specs=[pl.BlockSpec((1, tl, w), lambda i, j: (i, j, 0)), tab, tab],
        out_specs=[hspec(ATT_HEADS), hspec(ATT_KV_HEADS), hspec(ATT_KV_HEADS)],
        out_shape=[jax.ShapeDtypeStruct((b, ATT_HEADS, l, HEAD_DIM), BF16),
                   jax.ShapeDtypeStruct((b, ATT_KV_HEADS, l, HEAD_DIM), BF16),
                   jax.ShapeDtypeStruct((b, ATT_KV_HEADS, l, HEAD_DIM), BF16)],
        compiler_params=_cparams("parallel", "parallel"),
        name="qkv_heads",
    )(qkv, cos2, sin2)


def _attention_kernel(sink_ref, q_ref, kc_ref, vc_ref, *rest, local):
    if local:
        kp_ref, ko_ref, kn_ref, vp_ref, vo_ref, vn_ref, bias_ref, o_ref = rest
    else:
        (o_ref,) = rest
    blk = ATT_BLOCK
    rows = ATT_GROUP * blk
    outs = []
    for g in range(ATT_KV_HEADS):
        q = q_ref[0, g * ATT_GROUP:(g + 1) * ATT_GROUP].reshape(rows, HEAD_DIM)
        if local:
            keys = jnp.concatenate([kc_ref[0, g], kp_ref[0, g], ko_ref[0, g], kn_ref[0, g]], axis=0)
            vals = jnp.concatenate([vc_ref[0, g], vp_ref[0, g], vo_ref[0, g], vn_ref[0, g]], axis=0)
        else:
            keys, vals = kc_ref[0, g], vc_ref[0, g]
        s = lax.dot_general(q, keys, (((1,), (1,)), ((), ())), preferred_element_type=F32)
        r = lax.broadcasted_iota(jnp.int32, (rows, 1), 0)
        if local:
            s = s + jnp.concatenate([bias_ref[0]] * ATT_GROUP, axis=0)
        sink = jnp.zeros((rows, 1), F32)
        for t in range(ATT_GROUP):
            sink = jnp.where(r // blk == t, sink_ref[g * ATT_GROUP + t], sink)
        m = jnp.maximum(jnp.max(s, axis=-1, keepdims=True), sink)
        p = jnp.exp(s - m)
        denom = jnp.sum(p, axis=-1, keepdims=True) + jnp.exp(sink - m)
        o = jnp.dot(p.astype(BF16), vals, preferred_element_type=F32) / denom
        outs += [o[t * blk:(t + 1) * blk] for t in range(ATT_GROUP)]
    o_ref[0] = jnp.concatenate(outs, axis=1).astype(o_ref.dtype)


def attention(q, k, v, kc, vc, sink, local):
    b, _, l, d = q.shape
    c = kc.shape[2]
    nb = l // ATT_BLOCK
    qspec = pl.BlockSpec((1, ATT_HEADS, ATT_BLOCK, d), lambda i, j: (i, 0, j, 0))
    cspec = pl.BlockSpec((1, ATT_KV_HEADS, c, d), lambda i, j: (i, 0, 0, 0))
    in_specs = [pl.BlockSpec(memory_space=pltpu.SMEM), qspec, cspec, cspec]
    args = [sink, q, kc, vc]
    if local:
        kv = lambda f: pl.BlockSpec((1, ATT_KV_HEADS, ATT_BLOCK, d), lambda i, j: (i, 0, f(j), 0))
        band = [kv(lambda j: jnp.maximum(j - 1, 0)), kv(lambda j: j), kv(lambda j: jnp.minimum(j + 1, nb - 1))]
        in_specs += band + band
        args += [k, k, k, v, v, v]
        qpos = jnp.arange(ATT_BLOCK)[:, None]
        kpos = jnp.arange(-ATT_BLOCK, 2 * ATT_BLOCK)[None, :]
        near = jnp.abs(kpos - qpos) <= WINDOW
        variants = [near, near & (kpos >= 0), near & (kpos < ATT_BLOCK), near & (kpos >= 0) & (kpos < ATT_BLOCK)]
        bias = jnp.stack([jnp.concatenate([jnp.zeros((ATT_BLOCK, c), F32), jnp.where(ok, 0.0, NEG_INF)], axis=1)
                          for ok in variants])
        in_specs.append(pl.BlockSpec((1, ATT_BLOCK, c + 3 * ATT_BLOCK),
                                     lambda i, j: ((j == 0) + 2 * (j == nb - 1), 0, 0)))
        args.append(bias)
    return pl.pallas_call(
        functools.partial(_attention_kernel, local=local),
        grid=(b, nb),
        in_specs=in_specs,
        out_specs=pl.BlockSpec((1, ATT_BLOCK, ATT_HEADS * d), lambda i, j: (i, j, 0)),
        out_shape=jax.ShapeDtypeStruct((b, l, ATT_HEADS * d), BF16),
        compiler_params=_cparams("parallel", "parallel"),
        name="window_attention" if local else "context_attention",
    )(*args)


def hyena_filters(L, lp):
    t = jnp.arange(L, dtype=F32)
    tn = t / L
    bands = jnp.arange(1, HY_BANDS + 1, dtype=F32)
    ang = 2.0 * math.pi * tn[:, None] * bands[None, :]
    z = jnp.concatenate([tn[:, None], jnp.cos(ang), jnp.sin(ang)], axis=-1)
    hdn = jnp.sin(lp['hy_freq1'] * (z @ lp['hy_w1'] + lp['hy_b1']))
    hdn = jnp.sin(lp['hy_freq2'] * (hdn @ lp['hy_w2'] + lp['hy_b2']))
    filt = hdn @ lp['hy_w3']
    rate = jnp.linspace(HY_FAST_RATE, HY_SLOW_RATE, HY_WIDTH, dtype=F32)
    tw = jnp.linspace(0.0, 1.0, L, dtype=F32)
    window = jnp.exp(-tw[:, None] * rate[None, :])
    return filt[:, :HY_WIDTH] * window, filt[:, HY_WIDTH:] * window


def _short_conv_kernel(z_ref, zp_ref, zn_ref, w_ref, b_ref, x0_ref, vg_ref, vgb_ref):
    i = pl.program_id(1)
    z = z_ref[0]
    tl, w3 = z.shape
    prev_row = jnp.where(i > 0, zp_ref[0, 7:8, :], 0.0)
    next_row = jnp.where(i < pl.num_programs(1) - 1, zn_ref[0, 0:1, :], 0.0)
    row = lax.broadcasted_iota(jnp.int32, (tl, w3), 0)
    z_prev = jnp.where(row == 0, prev_row, pltpu.roll(z, 1, axis=0))
    z_next = jnp.where(row == tl - 1, next_row, pltpu.roll(z, tl - 1, axis=0))
    y = b_ref[...] + z_prev * w_ref[0:1, :] + z * w_ref[1:2, :] + z_next * w_ref[2:3, :]
    w = w3 // HY_N_PROJ
    x0_ref[0] = y[:, :w]
    vg = y[:, 2 * w:] * y[:, w:2 * w]
    vg_ref[0] = vg
    vgb_ref[0] = vg.astype(BF16)


def hyena_short_conv(z, w, b):
    bsz, l, w3 = z.shape
    tl = min(l, 256)
    nb8 = tl // 8
    last8 = l // 8 - 1
    out = jax.ShapeDtypeStruct((bsz, l, w3 // HY_N_PROJ), F32)
    ospec = pl.BlockSpec((1, tl, w3 // HY_N_PROJ), lambda i, j: (i, j, 0))
    return pl.pallas_call(
        _short_conv_kernel,
        grid=(bsz, l // tl),
        in_specs=[pl.BlockSpec((1, tl, w3), lambda i, j: (i, j, 0)),
                  pl.BlockSpec((1, 8, w3), lambda i, j: (i, jnp.maximum(j * nb8 - 1, 0), 0)),
                  pl.BlockSpec((1, 8, w3), lambda i, j: (i, jnp.minimum((j + 1) * nb8, last8), 0)),
                  pl.BlockSpec((HY_SHORT, w3), lambda i, j: (0, 0)),
                  pl.BlockSpec((1, w3), lambda i, j: (0, 0))],
        out_specs=[ospec, ospec, ospec],
        out_shape=[out, out, jax.ShapeDtypeStruct(out.shape, BF16)],
        compiler_params=_cparams("parallel", "parallel"),
        name="hyena_short_conv",
    )(z, z, z, w, b.reshape(1, w3))


def dft_tables(l):
    n = 2 * l
    split = 64
    t = jnp.arange(l, dtype=jnp.int32)[None, :]
    kh = jnp.arange(l // split, dtype=jnp.int32)[:, None]
    kl = jnp.arange(split, dtype=jnp.int32)[:, None]
    ang_h = (2.0 * math.pi / n) * (((kh * split) * t) % n).astype(F32)
    ang_l = (2.0 * math.pi / n) * ((kl * t) % n).astype(F32)
    ch, sh = jnp.cos(ang_h)[:, None, :], jnp.sin(ang_h)[:, None, :]
    cl, sl = jnp.cos(ang_l)[None, :, :], jnp.sin(ang_l)[None, :, :]
    cos_t = (ch * cl - sh * sl).reshape(l, l).astype(BF16)
    sin_t = (sh * cl + ch * sl).reshape(l, l).astype(BF16)
    return cos_t, sin_t


def _filter_spectrum_kernel(c_ref, s_ref, hp_ref, hm_ref, kre_ref, kim_ref):
    kre_ref[...] = jnp.dot(c_ref[...], hp_ref[...], preferred_element_type=F32)
    kim_ref[...] = -jnp.dot(s_ref[...], hm_ref[...], preferred_element_type=F32)


def filter_spectrum(cos_t, sin_t, h_fwd, h_bwd):
    l, w = h_fwd.shape
    hb0 = h_bwd.at[0].set(0.0)
    tk = 512
    tab = pl.BlockSpec((tk, l), lambda k: (k, 0))
    full = pl.BlockSpec((l, w), lambda k: (0, 0))
    ospec = pl.BlockSpec((tk, w), lambda k: (k, 0))
    out = jax.ShapeDtypeStruct((l, w), F32)
    kre, kim = pl.pallas_call(
        _filter_spectrum_kernel,
        grid=(l // tk,),
        in_specs=[tab, tab, full, full],
        out_specs=[ospec, ospec],
        out_shape=[out, out],
        compiler_params=_cparams("parallel"),
        name="hyena_filter_spectrum",
    )(cos_t, sin_t, (h_fwd + hb0).astype(BF16), (h_fwd - hb0).astype(BF16))
    sign = jnp.where(jnp.arange(l) % 2 == 0, 1.0, -1.0)[:, None]
    k_nyq = jnp.sum(sign * (h_fwd + hb0), axis=0, keepdims=True)
    return kre, kim, k_nyq


def _long_conv_kernel(x_ref, crow_ref, srow_ref, ccol_ref, scol_ref, kre_ref, kim_ref, knyq_ref, y_ref,
                      acc_scr, xn_scr):
    k = pl.program_id(1)
    nk = pl.num_programs(1)
    x = x_ref[0]
    l = x.shape[0]
    tk = crow_ref.shape[0]

    @pl.when(k == 0)
    def _():
        acc_scr[...] = jnp.zeros_like(acc_scr)
        t_idx = lax.broadcasted_iota(jnp.int32, (8, l), 1)
        alt = jnp.where(t_idx % 2 == 0, 1.0, -1.0).astype(BF16)
        xn_scr[...] = jnp.dot(alt, x, preferred_element_type=F32)

    xre = jnp.dot(crow_ref[...], x, preferred_element_type=F32)
    xim = -jnp.dot(srow_ref[...], x, preferred_element_type=F32)
    kre, kim = kre_ref[...], kim_ref[...]
    k_idx = lax.broadcasted_iota(jnp.int32, (tk, 1), 0) + k * tk
    weight = jnp.where(k_idx == 0, 1.0, 2.0)
    yre = ((xre * kre - xim * kim) * weight).astype(BF16)
    nyim = ((xre * kim + xim * kre) * -weight).astype(BF16)
    acc_scr[...] += (jnp.dot(ccol_ref[...], yre, preferred_element_type=F32)
                     + jnp.dot(scol_ref[...], nyim, preferred_element_type=F32))

    @pl.when(k == nk - 1)
    def _():
        t_idx = lax.broadcasted_iota(jnp.int32, (l, 1), 0)
        nyq = xn_scr[0:1, :] * knyq_ref[...]
        y_ref[0] = (acc_scr[...] + jnp.where(t_idx % 2 == 0, nyq, -nyq)) * (0.5 / l)


def long_conv(x, cos_t, sin_t, kre, kim, k_nyq):
    b, l, w = x.shape
    tk = 256
    row_tab = pl.BlockSpec((tk, l), lambda i, k: (k, 0))
    col_tab = pl.BlockSpec((l, tk), lambda i, k: (0, k))
    kspec = pl.BlockSpec((tk, w), lambda i, k: (k, 0))
    return pl.pallas_call(
        _long_conv_kernel,
        grid=(b, l // tk),
        in_specs=[pl.BlockSpec((1, l, w), lambda i, k: (i, 0, 0)), row_tab, row_tab, col_tab, col_tab,
                  kspec, kspec, pl.BlockSpec((1, w), lambda i, k: (0, 0))],
        out_specs=pl.BlockSpec((1, l, w), lambda i, k: (i, 0, 0)),
        out_shape=jax.ShapeDtypeStruct((b, l, w), F32),
        scratch_shapes=[pltpu.VMEM((l, w), F32), pltpu.VMEM((8, w), F32)],
        compiler_params=_cparams("parallel", "arbitrary"),
        name="hyena_long_conv",
    )(x, cos_t, sin_t, cos_t, sin_t, kre, kim, k_nyq)


def _hyena_gate_kernel(y_ref, vg_ref, x0_ref, bias_ref, o_ref):
    o_ref[...] = ((y_ref[...] + vg_ref[...] * bias_ref[...]) * x0_ref[...]).astype(o_ref.dtype)


def hyena_gate(y, vg, x0, bias):
    b, l, w = y.shape
    m = b * l
    tm = 1024
    row = pl.BlockSpec((tm, w), lambda i: (i, 0))
    out = pl.pallas_call(
        _hyena_gate_kernel,
        grid=(m // tm,),
        in_specs=[row, row, row, pl.BlockSpec((1, w), lambda i: (0, 0))],
        out_specs=row,
        out_shape=jax.ShapeDtypeStruct((m, w), BF16),
        compiler_params=_cparams("parallel"),
        name="hyena_gate",
    )(y.reshape(m, w), vg.reshape(m, w), x0.reshape(m, w), bias.reshape(1, w))
    return out.reshape(b, l, w)


def _dense_dft_tables(l):
    n = 2 * l
    ang = 2.0 * np.pi * np.outer(np.arange(n), np.arange(n)) / n
    fwd = np.concatenate([np.cos(ang), -np.sin(ang)], axis=0)
    inv = np.concatenate([np.cos(ang[:l]), -np.sin(ang[:l])], axis=1) / n
    return jnp.asarray(fwd, F32), jnp.asarray(inv, F32)


def _dense_conv_kernel(vg_ref, x0_ref, kern_ref, f_ref, g_ref, bias_ref, o_ref):
    hp = lax.Precision.HIGHEST
    l = vg_ref.shape[1]
    n = 2 * l
    vg = vg_ref[0]
    ks = jnp.dot(f_ref[...], kern_ref[...], preferred_element_type=F32, precision=hp)
    xs = jnp.dot(f_ref[:, :l], vg, preferred_element_type=F32, precision=hp)
    kre, kim, xre, xim = ks[:n], ks[n:], xs[:n], xs[n:]
    ys = jnp.concatenate([xre * kre - xim * kim, xre * kim + xim * kre], axis=0)
    y = jnp.dot(g_ref[...], ys, preferred_element_type=F32, precision=hp)
    o_ref[0] = ((y + vg * bias_ref[...]) * x0_ref[0]).astype(o_ref.dtype)


def hyena_dense_conv(vg, x0, kern, bias):
    bsz, l, w = vg.shape
    fwd, inv = _dense_dft_tables(l)
    row = pl.BlockSpec((1, l, w), lambda i: (i, 0, 0))
    full = lambda a: pl.BlockSpec(a.shape, lambda i: (0, 0))
    return pl.pallas_call(
        _dense_conv_kernel,
        grid=(bsz,),
        in_specs=[row, row, full(kern), full(fwd), full(inv), pl.BlockSpec((1, w), lambda i: (0, 0))],
        out_specs=row,
        out_shape=jax.ShapeDtypeStruct((bsz, l, w), BF16),
        compiler_params=_cparams("parallel"),
        name="hyena_dense_conv",
    )(vg, x0, kern, fwd, inv, bias.reshape(1, w))


HY_DENSE_MAX_LEN = 512


def hyena_mixer(z, lp, tables=None):
    l = z.shape[1]
    x0, vg, vg_bf16 = hyena_short_conv(z, lp['hy_short_w'], lp['hy_short_b'])
    h_fwd, h_bwd = hyena_filters(l, lp)
    if l <= HY_DENSE_MAX_LEN:
        kern = jnp.concatenate([h_fwd, jnp.zeros_like(h_fwd[:1]), h_bwd[1:][::-1]], axis=0)
        return hyena_dense_conv(vg, x0, kern, lp['hy_bias'])
    cos_t, sin_t = tables if tables is not None else dft_tables(l)
    kre, kim, k_nyq = filter_spectrum(cos_t, sin_t, h_fwd, h_bwd)
    y = long_conv(vg_bf16, cos_t, sin_t, kre, kim, k_nyq)
    return hyena_gate(y, vg, x0, lp['hy_bias'])


S5_CHUNK = 16
S5_CK = S5_CHUNK * S5_GROUP
S5_OCT = LANES // S5_GROUP
S5_OCT_W = S5_OCT * S5_GROUP
S5_N_OCT = S5_GROUPS // S5_OCT
S5_ROW = S5_CHUNK * S5_OCT_W


def _s5_powers(lp, d):
    lam = lax.complex(jnp.minimum(lp['s5_a_re'][d], -1e-4), lp['s5_a_im'][d])
    dt = jnp.exp(lp['s5_log_dt'][d])[:, None]
    n = jnp.arange(S5_CHUNK + 1, dtype=F32)[:, None, None]
    apow = jnp.exp((lam * dt)[None] * n)
    b = lax.complex(lp['s5_b_re'][d], lp['s5_b_im'][d])
    bbar = ((apow[1] - 1.0) / lam)[..., None] * b
    cmat = lax.complex(lp['s5_c_re'][d], lp['s5_c_im'][d])
    return apow, bbar, cmat


def s5_prepare(lp):
    q = S5_CHUNK
    g_n, p_n = S5_GROUPS, S5_STATE
    hp = lax.Precision.HIGHEST
    ap_f, bb_f, c_f = _s5_powers(lp, 0)
    ap_b, bb_b, c_b = _s5_powers(lp, 1)
    k_f = jnp.einsum('gop,tgp,gpi->gtio', c_f, ap_f[:q], bb_f, precision=hp).real
    k_b = jnp.einsum('gop,tgp,gpi->gtio', c_b, ap_b[:q], bb_b, precision=hp).real
    zero = jnp.zeros_like(k_f)
    fwd_ext = jnp.concatenate([zero, k_f], axis=1)
    bwd_ext = jnp.concatenate([k_b[:, ::-1], zero], axis=1)
    m_f = jnp.stack([fwd_ext[:, q - sp:2 * q - sp] for sp in range(q)], axis=1)
    m_b = jnp.stack([bwd_ext[:, q - 1 - sp:2 * q - 1 - sp] for sp in range(q)], axis=1)
    m_intra = (m_f + m_b).transpose(0, 1, 3, 2, 4).reshape(g_n, S5_CK, S5_CK)

    ws_f = ap_f[q - 1 - jnp.arange(q)][:, :, :, None] * bb_f[None]
    ws_b = ap_b[jnp.arange(q)][:, :, :, None] * bb_b[None]
    ws = jnp.stack([ws_f.real, ws_f.imag, ws_b.real, ws_b.imag], axis=0)
    ws = ws.transpose(2, 1, 4, 0, 3).reshape(g_n, S5_CK, 4, p_n)

    ca_f = c_f[None] * ap_f[1:q + 1][:, :, None, :]
    ca_b = c_b[None] * ap_b[q - jnp.arange(q)][:, :, None, :]
    wo = jnp.stack([ca_f.real, -ca_f.imag, ca_b.real, -ca_b.imag], axis=0)
    wo = wo.transpose(2, 0, 4, 1, 3).reshape(g_n, 4, p_n, S5_CK)

    h_n = S5_GROUP
    eye = jnp.eye(S5_OCT, dtype=F32)
    m6 = m_intra.reshape(S5_N_OCT, S5_OCT, q, h_n, q, h_n)
    m_oct = jnp.einsum('ogaibc,gh->oagibhc', m6, eye).reshape(S5_N_OCT, S5_ROW, S5_ROW)
    ws6 = ws.reshape(S5_N_OCT, S5_OCT, q, h_n, 4, p_n)
    w_state = jnp.einsum('ogaicp,gh->oagichp', ws6, eye).reshape(S5_N_OCT, S5_ROW, 4 * S5_OCT * p_n)
    wo6 = wo.reshape(S5_N_OCT, S5_OCT, 4, p_n, q, h_n)
    w_out = jnp.einsum('ogcpsd,gh->ocgpshd', wo6, eye).reshape(S5_N_OCT, 4 * S5_OCT * p_n, S5_ROW)

    def flat(z):
        return jnp.stack([z.real.reshape(-1), z.imag.reshape(-1)], axis=0)

    return m_oct.astype(BF16), w_state.astype(BF16), w_out.astype(BF16), flat(ap_f[q]), flat(ap_b[q])


def _s5_chunk_rows(u_ref, n_chunks):
    steps = [u_ref[0, pl.ds(s, n_chunks, stride=S5_CHUNK), :].astype(BF16) for s in range(S5_CHUNK)]
    return jnp.concatenate(steps, axis=1)


def _s5_states_kernel(u_ref, w_ref, fre_ref, fim_ref, bre_ref, bim_ref):
    n_chunks, n = fre_ref.shape[1:]
    o = jnp.dot(_s5_chunk_rows(u_ref, n_chunks), w_ref[0], preferred_element_type=F32)
    for comp, ref in enumerate((fre_ref, fim_ref, bre_ref, bim_ref)):
        ref[0] = o[:, comp * n:(comp + 1) * n]


def s5_states(u_all, w_state):
    b, t, w = u_all.shape
    nc = t // S5_CHUNK
    pw = S5_OCT * S5_STATE
    out = jax.ShapeDtypeStruct((b, nc, S5_GROUPS * S5_STATE), F32)
    ospec = pl.BlockSpec((1, nc, pw), lambda o, i: (i, 0, o))
    return pl.pallas_call(
        _s5_states_kernel,
        grid=(S5_N_OCT, b),
        in_specs=[pl.BlockSpec((1, t, S5_OCT_W), lambda o, i: (i, 0, o)),
                  pl.BlockSpec((1, S5_ROW, 4 * pw), lambda o, i: (o, 0, 0))],
        out_specs=[ospec] * 4,
        out_shape=[out] * 4,
        compiler_params=_cparams("parallel", "parallel"),
        name="s5_states",
    )(u_all, w_state)


def _s5_scan_kernel(sre_ref, sim_ref, a_ref, hre_ref, him_ref, *, n_ctx_tiles, reverse):
    tile = 8
    n_tiles = sre_ref.shape[1] // tile
    ar = a_ref[0:1, :]
    ai = a_ref[1:2, :]
    zero = jnp.zeros((1, sre_ref.shape[2]), F32)

    def body(n, carry):
        hre, him = carry
        if reverse:
            t = jnp.where(n < n_ctx_tiles, n_ctx_tiles - 1 - n, n_tiles + n_ctx_tiles - 1 - n)
        else:
            t = n
        r0 = pl.multiple_of(t * tile, tile)
        sre = sre_ref[0, pl.ds(r0, tile), :]
        sim = sim_ref[0, pl.ds(r0, tile), :]
        in_re = [None] * tile
        in_im = [None] * tile
        for r in (range(tile - 1, -1, -1) if reverse else range(tile)):
            in_re[r], in_im[r] = hre, him
            hre, him = ar * hre - ai * him + sre[r:r + 1], ar * him + ai * hre + sim[r:r + 1]
        hre_ref[0, pl.ds(r0, tile), :] = jnp.concatenate(in_re, axis=0)
        him_ref[0, pl.ds(r0, tile), :] = jnp.concatenate(in_im, axis=0)
        return hre, him

    lax.fori_loop(0, n_tiles, body, (zero, zero))


def s5_scan(sre, sim, a_chunk, n_ctx_chunks, reverse):
    b, nc, w = sre.shape
    tl = 512
    spec = pl.BlockSpec((1, nc, tl), lambda i, j: (i, 0, j))
    out = jax.ShapeDtypeStruct((b, nc, w), F32)
    return pl.pallas_call(
        functools.partial(_s5_scan_kernel, n_ctx_tiles=n_ctx_chunks // 8, reverse=reverse),
        grid=(b, w // tl),
        in_specs=[spec, spec, pl.BlockSpec((2, tl), lambda i, j: (0, j))],
        out_specs=[spec, spec],
        out_shape=[out, out],
        compiler_params=_cparams("parallel", "parallel"),
        name="s5_scan_bwd" if reverse else "s5_scan_fwd",
    )(sre, sim, a_chunk)


def _s5_outputs_kernel(u_ref, m_ref, fre_ref, fim_ref, bre_ref, bim_ref, wo_ref, y_ref):
    n_chunks = fre_ref.shape[1]
    carried = jnp.concatenate([r[0].astype(BF16) for r in (fre_ref, fim_ref, bre_ref, bim_ref)], axis=1)
    y = (jnp.dot(_s5_chunk_rows(u_ref, n_chunks), m_ref[0], preferred_element_type=F32)
         + jnp.dot(carried, wo_ref[0], preferred_element_type=F32))
    for s in range(S5_CHUNK):
        y_ref[0, pl.ds(s, n_chunks, stride=S5_CHUNK), :] = y[:, s * S5_OCT_W:(s + 1) * S5_OCT_W]


def s5_outputs(u_all, m_oct, carried, w_out):
    b, t, w = u_all.shape
    nc = t // S5_CHUNK
    pw = S5_OCT * S5_STATE
    hspec = pl.BlockSpec((1, nc, pw), lambda o, i: (i, 0, o))
    lane_blk = pl.BlockSpec((1, t, S5_OCT_W), lambda o, i: (i, 0, o))
    mat = pl.BlockSpec((1, S5_ROW, S5_ROW), lambda o, i: (o, 0, 0))
    return pl.pallas_call(
        _s5_outputs_kernel,
        grid=(S5_N_OCT, b),
        in_specs=[lane_blk, mat, hspec, hspec, hspec, hspec, mat],
        out_specs=lane_blk,
        out_shape=jax.ShapeDtypeStruct((b, t, w), F32),
        compiler_params=_cparams("parallel", "parallel"),
        name="s5_outputs",
    )(u_all, m_oct, *carried, w_out)


def _s5_readout_kernel(y_ref, u_ref, d_ref, w_ref, b_ref, o_ref):
    y = _gelu_tanh(y_ref[...] + d_ref[...] * u_ref[...])
    z = jnp.dot(y.astype(BF16), w_ref[...], preferred_element_type=F32) + b_ref[...]
    o_ref[...] = (y * jax.nn.sigmoid(z)).astype(o_ref.dtype)


def s5_readout(y, u, d, glu_w, glu_b):
    m, w = y.shape
    tm = min(m, 512)
    row = pl.BlockSpec((tm, w), lambda i: (i, 0))
    vec = pl.BlockSpec((1, w), lambda i: (0, 0))
    return pl.pallas_call(
        _s5_readout_kernel,
        grid=(m // tm,),
        in_specs=[row, row, vec, pl.BlockSpec((w, w), lambda i: (0, 0)), vec],
        out_specs=row,
        out_shape=jax.ShapeDtypeStruct((m, w), BF16),
        compiler_params=_cparams("parallel"),
        name="s5_readout",
    )(y, u, d.reshape(1, w), glu_w, glu_b.reshape(1, w))


def s5_mixer(u, uc, lp, glu_w):
    b, l, w = u.shape
    c = uc.shape[1]
    m_oct, w_state, w_out, a_f, a_b = s5_prepare(lp)
    u_all = jnp.concatenate([uc, u], axis=1)
    fre, fim, bre, bim = s5_states(u_all, w_state)
    hf = s5_scan(fre, fim, a_f, c // S5_CHUNK, False)
    hb = s5_scan(bre, bim, a_b, c // S5_CHUNK, True)
    y_all = s5_outputs(u_all, m_oct, (*hf, *hb), w_out)
    out = s5_readout(y_all.reshape(b * (c + l), w), u_all.reshape(b * (c + l), w), lp['s5_d'], glu_w,
                     lp['s5_glu_b'])
    out = out.reshape(b, c + l, w)
    return out[:, c:], out[:, :c]


def kernel(x, c, ctx, c_ctx, mod_w, mod_b, norm1_g, norm2_g, in_w, gate_b, attn_sink, hy_short_w, hy_short_b,
           hy_w1, hy_b1, hy_freq1, hy_w2, hy_b2, hy_freq2, hy_w3, hy_bias, s5_a_re, s5_a_im, s5_log_dt,
           s5_b_re, s5_b_im, s5_c_re, s5_c_im, s5_d, s5_glu_w, s5_glu_b, br_attn_w, br_hyena_w, br_s5_w,
           out_w, peer_wq, peer_keys, peer_u, peer_v, final_g):
    B, L, D = x.shape
    C = ctx.shape[1]
    depth = mod_w.shape[0]
    rope_cos, rope_sin = rope_tables(L)
    conv_tables = dft_tables(L) if L > HY_DENSE_MAX_LEN else None

    cvec = jnp.zeros((8, D), F32).at[:B].set(c).at[B].set(c_ctx)
    mod_all = modulation(cvec, mod_w, mod_b)

    xc = ctx
    for layer in range(depth):
        need_ctx = layer < depth - 1
        lp = dict(attn_sink=attn_sink[layer],
                  hy_short_w=hy_short_w[layer], hy_short_b=hy_short_b[layer],
                  hy_w1=hy_w1[layer], hy_b1=hy_b1[layer], hy_freq1=hy_freq1[layer],
                  hy_w2=hy_w2[layer], hy_b2=hy_b2[layer], hy_freq2=hy_freq2[layer],
                  hy_w3=hy_w3[layer], hy_bias=hy_bias[layer],
                  s5_a_re=s5_a_re[layer], s5_a_im=s5_a_im[layer], s5_log_dt=s5_log_dt[layer],
                  s5_b_re=s5_b_re[layer], s5_b_im=s5_b_im[layer],
                  s5_c_re=s5_c_re[layer], s5_c_im=s5_c_im[layer], s5_d=s5_d[layer],
                  s5_glu_w=s5_glu_w[layer], s5_glu_b=s5_glu_b[layer])
        mod = mod_all[layer]
        ml = [mod[:B, i * D:(i + 1) * D] for i in range(N_MOD)]
        mc = [mod[B:B + 1, i * D:(i + 1) * D] for i in range(N_MOD)]

        w_in = in_w[layer].astype(BF16)
        w_parts = [w_in[:, COL_Q:COL_S5], w_in[:, COL_S5:COL_HY], w_in[:, COL_HY:COL_GATE], w_in[:, COL_GATE:]]
        wa = br_attn_w[layer].astype(BF16)
        wh = br_hyena_w[layer].astype(BF16)
        ws = br_s5_w[layer].astype(BF16)
        wo = out_w[layer].astype(BF16)
        wqt = peer_wq[layer].T.astype(BF16)
        keys = peer_keys[layer].reshape(2 * PEER_HEADS, PEER_KEYS, PEER_HALF).astype(BF16)
        u_tab = peer_u[layer].astype(BF16)
        vt_tab = peer_v[layer].T.astype(BF16)

        h = norm_modulate(x, norm1_g[layer], ml[0], ml[1])
        hc = norm_modulate(xc, norm1_g[layer], mc[0], mc[1])
        qkv, u_s5, z_hy, gates = in_projection(h.reshape(B * L, D), w_parts, gate_b[layer])
        qkv_c, uc_s5, zc_hy, gates_c = in_projection(hc.reshape(B * C, D), w_parts, gate_b[layer])
        qkv = qkv.reshape(B, L, -1)
        qkv_c = qkv_c.reshape(B, C, -1)
        u_s5 = u_s5.reshape(B, L, -1)
        uc_s5 = uc_s5.reshape(B, C, -1)

        q, k, v = qkv_heads(qkv, rope_cos, rope_sin, True)
        qc, kc, vc = qkv_heads(qkv_c, rope_cos, rope_sin, False)
        att = attention(q, k, v, kc, vc, lp['attn_sink'], True)
        hy = hyena_mixer(z_hy.reshape(B, L, -1), lp, conv_tables)
        s5, s5c = s5_mixer(u_s5, uc_s5, lp, s5_glu_w[layer].astype(BF16))

        x = branch_merge(att, hy, s5, gates.reshape(B, L, -1), wa, wh, ws, wo, x, ml[2])
        x = peer_block(x, norm2_g[layer], ml[3], ml[4], ml[5], wqt, keys, u_tab, vt_tab)

        if need_ctx:
            att_c = attention(qc, None, None, kc, vc, lp['attn_sink'], False)
            hy_c = hyena_mixer(zc_hy.reshape(B, C, -1), lp)
            xc = branch_merge(att_c, hy_c, s5c, gates_c.reshape(B, C, -1), wa, wh, ws, wo, xc, mc[2])
            xc = peer_block(xc, norm2_g[layer], mc[3], mc[4], mc[5], wqt, keys, u_tab, vt_tab)
    return final_norm(x, final_g)
```

```python
import functools
import math

import jax
import jax.numpy as jnp
import numpy as np
from jax import lax
from jax.experimental import pallas as pl
from jax.experimental.pallas import tpu as pltpu

F32 = jnp.float32
BF16 = jnp.bfloat16

NORM_EPS = 1e-6
NEG_INF = -1e30
N_MOD = 6

GRID_W = 64
ATT_HEADS = 8
ATT_KV_HEADS = 2
ATT_GROUP = ATT_HEADS // ATT_KV_HEADS
HEAD_DIM = 64
ATT_WIDTH = ATT_HEADS * HEAD_DIM
KV_WIDTH = ATT_KV_HEADS * HEAD_DIM
WINDOW = 128
ATT_BLOCK = 128
ROPE_BASE = 10000.0
ROPE_PAIRS_PER_AXIS = HEAD_DIM // 4

HY_WIDTH = 512
HY_N_PROJ = 3
HY_SHORT = 3
HY_BANDS = 16
HY_DECAY_TARGET = 1e-2
HY_FAST_RATE = -math.log(HY_DECAY_TARGET) / 0.3
HY_SLOW_RATE = -math.log(HY_DECAY_TARGET) / 1.5

S5_WIDTH = 512
S5_GROUP = 16
S5_GROUPS = S5_WIDTH // S5_GROUP
S5_STATE = 64

N_BRANCH = 3

PEER_HEADS = 8
PEER_KEYS = 128
PEER_HALF = 128
PEER_TOPK = 16

COL_Q = 0
COL_K = COL_Q + ATT_WIDTH
COL_V = COL_K + KV_WIDTH
COL_S5 = COL_V + KV_WIDTH
COL_HY = COL_S5 + S5_WIDTH
COL_GATE = COL_HY + HY_N_PROJ * HY_WIDTH

VMEM_LIMIT_V7X = 56 * 1024 * 1024


def _cparams(*sem):
    return pltpu.CompilerParams(dimension_semantics=sem, vmem_limit_bytes=VMEM_LIMIT_V7X)


def _gelu_tanh(x):
    return 0.5 * x * (1.0 + jnp.tanh(math.sqrt(2.0 / math.pi) * (x + 0.044715 * (x * x * x))))


def _mod_kernel(s_ref, w_ref, b_ref, o_ref):
    s = s_ref[...]
    s = (s * jax.nn.sigmoid(s)).astype(BF16)
    o_ref[0] = jnp.dot(s, w_ref[0].astype(BF16), preferred_element_type=F32) + b_ref[0]


def modulation(cvec, mod_w, mod_b):
    depth, d, n = mod_w.shape
    tn = 1024
    return pl.pallas_call(
        _mod_kernel,
        grid=(depth, n // tn),
        in_specs=[pl.BlockSpec((8, d), lambda l, j: (0, 0)),
                  pl.BlockSpec((1, d, tn), lambda l, j: (l, 0, j)),
                  pl.BlockSpec((1, 1, tn), lambda l, j: (l, 0, j))],
        out_specs=pl.BlockSpec((1, 8, tn), lambda l, j: (l, 0, j)),
        out_shape=jax.ShapeDtypeStruct((depth, 8, n), F32),
        compiler_params=_cparams("parallel", "parallel"),
        name="adaln_mod",
    )(cvec, mod_w, mod_b.reshape(depth, 1, n))


def _norm_mod_kernel(x_ref, g_ref, sh_ref, sc_ref, o_ref, *, transposed):
    x = x_ref[0]
    y = x * lax.rsqrt(jnp.mean(x * x, axis=-1, keepdims=True) + NORM_EPS)
    y = y * g_ref[...]
    y = y * (1.0 + sc_ref[0]) + sh_ref[0]
    if transposed:
        o_ref[...] = y.T.astype(o_ref.dtype)
    else:
        o_ref[0] = y.astype(o_ref.dtype)


def norm_modulate(x, g, shift, scale, transposed=False):
    b, l, d = x.shape
    tl = min(l, 512)
    nt = l // tl
    per_batch = shift.shape[0] == b and b > 1
    mod_map = (lambda i, j: (i, 0, 0)) if per_batch else (lambda i, j: (0, 0, 0))
    if transposed:
        out_spec = pl.BlockSpec((d, tl), lambda i, j: (0, i * nt + j))
        out_shape = jax.ShapeDtypeStruct((d, b * l), BF16)
    else:
        out_spec = pl.BlockSpec((1, tl, d), lambda i, j: (i, j, 0))
        out_shape = jax.ShapeDtypeStruct((b, l, d), BF16)
    return pl.pallas_call(
        functools.partial(_norm_mod_kernel, transposed=transposed),
        grid=(b, nt),
        in_specs=[pl.BlockSpec((1, tl, d), lambda i, j: (i, j, 0)),
                  pl.BlockSpec((1, d), lambda i, j: (0, 0)),
                  pl.BlockSpec((1, 1, d), mod_map),
                  pl.BlockSpec((1, 1, d), mod_map)],
        out_specs=out_spec,
        out_shape=out_shape,
        compiler_params=_cparams("parallel", "parallel"),
        name="norm_modulate_t" if transposed else "norm_modulate",
    )(x, g.reshape(1, d), shift.reshape(-1, 1, d), scale.reshape(-1, 1, d))


def _final_norm_kernel(x_ref, g_ref, o_ref):
    x = x_ref[...]
    y = x * lax.rsqrt(jnp.mean(x * x, axis=-1, keepdims=True) + NORM_EPS)
    o_ref[...] = y * g_ref[...]


def final_norm(x, g):
    b, l, d = x.shape
    x2 = x.reshape(b * l, d)
    tm = 512
    out = pl.pallas_call(
        _final_norm_kernel,
        grid=(b * l // tm,),
        in_specs=[pl.BlockSpec((tm, d), lambda i: (i, 0)), pl.BlockSpec((1, d), lambda i: (0, 0))],
        out_specs=pl.BlockSpec((tm, d), lambda i: (i, 0)),
        out_shape=jax.ShapeDtypeStruct((b * l, d), F32),
        compiler_params=_cparams("parallel"),
        name="final_norm",
    )(x2, g.reshape(1, d))
    return out.reshape(b, l, d)


def _inproj_kernel(h_ref, wqkv_ref, ws5_ref, why_ref, wg_ref, gb_ref, oqkv_ref, os5_ref, ohy_ref, og_ref):
    h = h_ref[...]
    oqkv_ref[...] = jnp.dot(h, wqkv_ref[...], preferred_element_type=F32)
    os5_ref[...] = jnp.dot(h, ws5_ref[...], preferred_element_type=F32)
    ohy_ref[...] = jnp.dot(h, why_ref[...], preferred_element_type=F32)
    og_ref[...] = jnp.dot(h, wg_ref[...], preferred_element_type=F32) + gb_ref[...]


def in_projection(h, w_parts, gate_b):
    m, d = h.shape
    tm = 256
    widths = [w.shape[1] for w in w_parts]
    w_specs = [pl.BlockSpec((d, n), lambda i: (0, 0)) for n in widths]
    return pl.pallas_call(
        _inproj_kernel,
        grid=(m // tm,),
        in_specs=[pl.BlockSpec((tm, d), lambda i: (i, 0))] + w_specs
                 + [pl.BlockSpec((1, widths[3]), lambda i: (0, 0))],
        out_specs=[pl.BlockSpec((tm, n), lambda i: (i, 0)) for n in widths],
        out_shape=[jax.ShapeDtypeStruct((m, n), F32) for n in widths],
        compiler_params=_cparams("parallel"),
        name="in_projection",
    )(h, *w_parts, gate_b.reshape(1, -1))


def _merge_kernel(att_ref, hy_ref, s5_ref, gl_ref, wa_ref, wh_ref, ws_ref, wo_ref, x_ref, g1_ref, o_ref):
    d = wo_ref.shape[0]
    g = jax.nn.sigmoid(gl_ref[0])
    m = g[:, :d] * jnp.dot(att_ref[0], wa_ref[...], preferred_element_type=F32)
    m = m + g[:, d:2 * d] * jnp.dot(hy_ref[0], wh_ref[...], preferred_element_type=F32)
    m = m + g[:, 2 * d:] * jnp.dot(s5_ref[0], ws_ref[...], preferred_element_type=F32)
    y = jnp.dot(m.astype(BF16), wo_ref[...], preferred_element_type=F32)
    o_ref[0] = x_ref[0] + g1_ref[0] * y


def branch_merge(att, hy, s5, gate_logits, wa, wh, ws, wo, x, g1):
    b, l, d = x.shape
    tl = min(l, 256)
    wd = att.shape[-1]
    per_batch = g1.shape[0] == b and b > 1
    g_map = (lambda i, j: (i, 0, 0)) if per_batch else (lambda i, j: (0, 0, 0))
    row = lambda n: pl.BlockSpec((1, tl, n), lambda i, j: (i, j, 0))
    full = lambda a: pl.BlockSpec(a.shape, lambda i, j: (0, 0))
    return pl.pallas_call(
        _merge_kernel,
        grid=(b, l // tl),
        in_specs=[row(wd), row(wd), row(wd), row(N_BRANCH * d), full(wa), full(wh), full(ws), full(wo),
                  row(d), pl.BlockSpec((1, 1, d), g_map)],
        out_specs=row(d),
        out_shape=jax.ShapeDtypeStruct((b, l, d), F32),
        compiler_params=_cparams("parallel", "parallel"),
        name="branch_merge",
    )(att, hy, s5, gate_logits, wa, wh, ws, wo, x, g1.reshape(-1, 1, d))


def _topk_rows(s, k):
    rows = []
    for _ in range(k):
        m = jnp.max(s, axis=0, keepdims=True)
        rows.append(m)
        s = jnp.where(s >= m, NEG_INF, s)
    return rows


LANES = 128


def _peer_select_kernel(ht_ref, wqt_ref, keys_ref, rank_ref, cnt_ref, e1_ref, e2_ref, qt_scr, cand_scr):
    nh = rank_ref.shape[0]
    nk = PEER_KEYS
    pairs = [(p, q) for p in range(PEER_TOPK) for q in range(PEER_TOPK) if (p + 1) * (q + 1) <= PEER_TOPK]

    def project(hh):
        r0 = pl.multiple_of(hh * 2 * nk, 2 * nk)
        qt_scr[hh % 2] = jnp.dot(wqt_ref[pl.ds(r0, 2 * nk), :], ht_ref[...],
                                 preferred_element_type=F32).astype(BF16)

    project(0)

    def head(hh, carry):
        qt = qt_scr[hh % 2]
        s1 = jnp.dot(keys_ref[2 * hh], qt[:nk], preferred_element_type=F32)
        s2 = jnp.dot(keys_ref[2 * hh + 1], qt[nk:], preferred_element_type=F32)
        project(jnp.minimum(hh + 1, nh - 1))
        for tb in range(s1.shape[1] // LANES):
            lanes = slice(tb * LANES, (tb + 1) * LANES)
            select_block(hh, tb, lanes, s1[:, lanes], s2[:, lanes])
        return carry

    def select_block(hh, tb, lanes, s1, s2):
        a = _topk_rows(s1, PEER_TOPK)
        b = []
        rest = s2
        rank = jnp.full(s2.shape, float(PEER_TOPK), F32)
        for r in range(PEER_TOPK):
            m = jnp.max(rest, axis=0, keepdims=True)
            b.append(m)
            hit = rest >= m
            rank = jnp.where(hit, float(r), rank)
            rest = jnp.where(hit, NEG_INF, rest)
        cand_scr[:, lanes] = jnp.full((cand_scr.shape[0], LANES), NEG_INF, F32)
        for r, (p, q) in enumerate(pairs):
            cand_scr[r:r + 1, lanes] = a[p] + b[q]
        cand = cand_scr[:, lanes]
        best = _topk_rows(cand, PEER_TOPK)
        top = a[0] + b[0]
        z = jnp.exp(best[0] - top)
        for r in range(1, PEER_TOPK):
            z = z + jnp.exp(best[r] - top)
        tau = best[PEER_TOPK - 1]
        reached = jnp.where(cand >= tau, 1.0, 0.0)
        cnt = jnp.zeros(s1.shape, F32)
        r = 0
        for p in range(PEER_TOPK):
            n_q = sum(1 for pp, _ in pairs if pp == p)
            cnt_p = reached[r:r + 1]
            for rr in range(r + 1, r + n_q):
                cnt_p = cnt_p + reached[rr:rr + 1]
            r += n_q
            cnt = jnp.where(s1 == a[p], cnt_p, cnt)
        rank_ref[hh, :, lanes] = rank.astype(BF16)
        cnt_ref[hh, tb] = cnt
        e1_ref[hh, tb] = 0.5 * jnp.exp(s1 - a[0])
        e2_ref[hh, :, lanes] = (jnp.exp(s2 - b[0]) / z).astype(BF16)

    lax.fori_loop(0, nh, head, 0)


def peer_select(ht, wqt, keys, tt):
    d, m = ht.shape
    nh = keys.shape[0] // 2
    out = jax.ShapeDtypeStruct((nh, PEER_KEYS, m), BF16)
    out32 = jax.ShapeDtypeStruct((nh, m // LANES, PEER_KEYS, LANES), F32)
    ospec = pl.BlockSpec((nh, PEER_KEYS, tt), lambda i: (0, 0, i))
    ospec32 = pl.BlockSpec((nh, tt // LANES, PEER_KEYS, LANES), lambda i: (0, i, 0, 0))
    return pl.pallas_call(
        _peer_select_kernel,
        grid=(m // tt,),
        in_specs=[pl.BlockSpec((d, tt), lambda i: (0, i)),
                  pl.BlockSpec(wqt.shape, lambda i: (0, 0)),
                  pl.BlockSpec(keys.shape, lambda i: (0, 0, 0))],
        out_specs=[ospec, ospec32, ospec32, ospec],
        out_shape=[out, out32, out32, out],
        scratch_shapes=[pltpu.VMEM((2, 2 * PEER_KEYS, tt), BF16), pltpu.VMEM((56, tt), F32)],
        compiler_params=_cparams("parallel"),
        name="peer_select",
    )(ht, wqt, keys)


PEER_PAIR = 2 * PEER_KEYS
PEER_LOOKAHEAD = 2


def _half_gate_gelu(half_w, x):
    c = math.sqrt(2.0 / math.pi)
    p = half_w * x
    return p + p * jnp.tanh(x * (c + (c * 0.044715) * (x * x)))


def _key_row(ref, hh, ii):
    blocks = [jnp.broadcast_to(ref[hh, tb, ii:ii + 1, :], (PEER_KEYS, LANES)) for tb in range(ref.shape[1])]
    return jnp.concatenate(blocks, axis=1).astype(BF16)


def _peer_expert_kernel(ht_ref, u_ref, vt_ref, rank_ref, cnt_ref, e1_ref, e2_ref, x_ref, g2_ref, o_ref,
                        acc_scr, wa_scr):
    k = pl.program_id(1)
    nh = rank_ref.shape[0]

    @pl.when(k == 0)
    def _():
        acc_scr[...] = jnp.zeros_like(acc_scr)

    ht = ht_ref[...]
    n_pairs = u_ref.shape[0] // PEER_PAIR

    def cols(p):
        return slice(p * PEER_PAIR, (p + 1) * PEER_PAIR)

    def first_matmul(p):
        return jnp.dot(u_ref[cols(p), :], ht, preferred_element_type=F32)

    def gated(p, act):
        w_rows = []
        for r in range(2):
            ii = 2 * p + r
            w = None
            for hh in range(nh):
                selected = rank_ref[hh] < _key_row(cnt_ref, hh, ii)
                gate = jnp.where(selected, e2_ref[hh], jnp.zeros((), BF16)) * _key_row(e1_ref, hh, ii)
                w = gate if w is None else w + gate
            w_rows.append(w)
        half_w = jnp.concatenate(w_rows, axis=0)
        return _half_gate_gelu(half_w, act.astype(BF16))

    acts = {q: first_matmul(q) for q in range(min(PEER_LOOKAHEAD, n_pairs))}
    part = None
    for p in range(n_pairs + 1):
        if p + PEER_LOOKAHEAD < n_pairs:
            acts[p + PEER_LOOKAHEAD] = first_matmul(p + PEER_LOOKAHEAD)
        if p < n_pairs:
            wa_scr[p] = gated(p, acts.pop(p))
        if p >= 1:
            d = jnp.dot(vt_ref[:, cols(p - 1)], wa_scr[p - 1], preferred_element_type=F32)
            part = d if part is None else part + d
    acc_scr[...] += part

    @pl.when(k == pl.num_programs(1) - 1)
    def _():
        o_ref[...] = x_ref[...] + g2_ref[0] * acc_scr[...].T


def peer_experts(ht, u_tab, vt_tab, sel, x, g2, tokens_per_batch, tt, ec):
    d, m = ht.shape
    e = u_tab.shape[0]
    nh = sel[0].shape[0]
    rows_i = ec // PEER_KEYS
    full_keys = pl.BlockSpec((nh, PEER_KEYS, tt), lambda i, k: (0, 0, i))
    chunk_keys = pl.BlockSpec((nh, tt // LANES, rows_i, LANES), lambda i, k: (0, i, k, 0))
    tiles_per_batch = tokens_per_batch // tt
    if g2.shape[0] > 1:
        g_map = lambda i, k: (i // tiles_per_batch, 0, 0)
    else:
        g_map = lambda i, k: (0, 0, 0)
    rank, cnt, e1, e2 = sel
    return pl.pallas_call(
        _peer_expert_kernel,
        grid=(m // tt, e // ec),
        in_specs=[pl.BlockSpec((d, tt), lambda i, k: (0, i)),
                  pl.BlockSpec((ec, d), lambda i, k: (k, 0)),
                  pl.BlockSpec((d, ec), lambda i, k: (0, k)),
                  full_keys, chunk_keys, chunk_keys, full_keys,
                  pl.BlockSpec((tt, d), lambda i, k: (i, 0)),
                  pl.BlockSpec((1, 1, d), g_map)],
        out_specs=pl.BlockSpec((tt, d), lambda i, k: (i, 0)),
        out_shape=jax.ShapeDtypeStruct((m, d), F32),
        scratch_shapes=[pltpu.VMEM((d, tt), F32), pltpu.VMEM((ec // PEER_PAIR, PEER_PAIR, tt), BF16)],
        compiler_params=_cparams("parallel", "arbitrary"),
        name="peer_experts",
    )(ht, u_tab, vt_tab, rank, cnt, e1, e2, x, g2.reshape(-1, 1, d))


def peer_block(x, g_norm, shift, scale, gate, wqt, keys, u_tab, vt_tab):
    b, l, d = x.shape
    ht = norm_modulate(x, g_norm, shift, scale, transposed=True)
    tt = min(l, 512)
    sel = peer_select(ht, wqt, keys, tt)
    out = peer_experts(ht, u_tab, vt_tab, sel, x.reshape(b * l, d), gate, l, tt, 2048)
    return out.reshape(b, l, d)


def rope_tables(l):
    rows_n = l // GRID_W
    row = jnp.repeat(jnp.arange(rows_n), GRID_W).astype(F32)
    col = jnp.tile(jnp.arange(GRID_W), rows_n).astype(F32)
    inv = jnp.power(ROPE_BASE, -jnp.arange(ROPE_PAIRS_PER_AXIS, dtype=F32) / ROPE_PAIRS_PER_AXIS)
    ang = jnp.concatenate([row[:, None] * inv, col[:, None] * inv], axis=-1)
    c, s = jnp.cos(ang), jnp.sin(ang)
    return jnp.concatenate([c, c, c, c], axis=-1), jnp.concatenate([-s, s, -s, s], axis=-1)


def _rope_pair(x, cos2, sin2):
    half = HEAD_DIM // 2
    lane = lax.broadcasted_iota(jnp.int32, x.shape, 1)
    swapped = jnp.where(lane % HEAD_DIM < half, pltpu.roll(x, 2 * HEAD_DIM - half, axis=1),
                        pltpu.roll(x, half, axis=1))
    return x * cos2 + swapped * sin2


def _qkv_heads_kernel(qkv_ref, cos_ref, sin_ref, q_ref, k_ref, v_ref, *, rope):
    scale = HEAD_DIM ** -0.5
    x = qkv_ref[0]
    pair = 2 * HEAD_DIM
    for j in range((ATT_WIDTH + KV_WIDTH) // pair):
        blk = x[:, j * pair:(j + 1) * pair]
        if rope:
            blk = _rope_pair(blk, cos_ref[...], sin_ref[...])
        for t in range(2):
            head = blk[:, t * HEAD_DIM:(t + 1) * HEAD_DIM]
            h = 2 * j + t
            if h < ATT_HEADS:
                q_ref[0, h] = (head * scale).astype(q_ref.dtype)
            else:
                k_ref[0, h - ATT_HEADS] = head.astype(k_ref.dtype)
    for t in range(ATT_KV_HEADS):
        v_ref[0, t] = x[:, COL_V + t * HEAD_DIM:COL_V + (t + 1) * HEAD_DIM].astype(v_ref.dtype)


def qkv_heads(qkv, cos2, sin2, rope):
    b, l, w = qkv.shape
    tl = min(l, 256)
    tab = pl.BlockSpec((tl, 2 * HEAD_DIM), lambda i, j: (j, 0))
    hspec = lambda n: pl.BlockSpec((1, n, tl, HEAD_DIM), lambda i, j: (i, 0, j, 0))
    return pl.pallas_call(
        functools.partial(_qkv_heads_kernel, rope=rope),
        grid=(b, l // tl),
        in_specs=[pl.BlockSpec((1, tl, w), lambda i, j: (i, j, 0)), tab, tab],
        out_specs=[hspec(ATT_HEADS), hspec(ATT_KV_HEADS), hspec(ATT_KV_HEADS)],
        out_shape=[jax.ShapeDtypeStruct((b, ATT_HEADS, l, HEAD_DIM), BF16),
                   jax.ShapeDtypeStruct((b, ATT_KV_HEADS, l, HEAD_DIM), BF16),
                   jax.ShapeDtypeStruct((b, ATT_KV_HEADS, l, HEAD_DIM), BF16)],
        compiler_params=_cparams("parallel", "parallel"),
        name="qkv_heads",
    )(qkv, cos2, sin2)


def _attention_kernel(sink_ref, q_ref, kc_ref, vc_ref, *rest, local):
    if local:
        kp_ref, ko_ref, kn_ref, vp_ref, vo_ref, vn_ref, bias_ref, o_ref = rest
    else:
        (o_ref,) = rest
    blk = ATT_BLOCK
    rows = ATT_GROUP * blk
    outs = []
    for g in range(ATT_KV_HEADS):
        q = q_ref[0, g * ATT_GROUP:(g + 1) * ATT_GROUP].reshape(rows, HEAD_DIM)
        if local:
            keys = jnp.concatenate([kc_ref[0, g], kp_ref[0, g], ko_ref[0, g], kn_ref[0, g]], axis=0)
            vals = jnp.concatenate([vc_ref[0, g], vp_ref[0, g], vo_ref[0, g], vn_ref[0, g]], axis=0)
        else:
            keys, vals = kc_ref[0, g], vc_ref[0, g]
        s = lax.dot_general(q, keys, (((1,), (1,)), ((), ())), preferred_element_type=F32)
        r = lax.broadcasted_iota(jnp.int32, (rows, 1), 0)
        if local:
            s = s + jnp.concatenate([bias_ref[0]] * ATT_GROUP, axis=0)
        sink = jnp.zeros((rows, 1), F32)
        for t in range(ATT_GROUP):
            sink = jnp.where(r // blk == t, sink_ref[g * ATT_GROUP + t], sink)
        m = jnp.maximum(jnp.max(s, axis=-1, keepdims=True), sink)
        p = jnp.exp(s - m)
        denom = jnp.sum(p, axis=-1, keepdims=True) + jnp.exp(sink - m)
        o = jnp.dot(p.astype(BF16), vals, preferred_element_type=F32) / denom
        outs += [o[t * blk:(t + 1) * blk] for t in range(ATT_GROUP)]
    o_ref[0] = jnp.concatenate(outs, axis=1).astype(o_ref.dtype)


def attention(q, k, v, kc, vc, sink, local):
    b, _, l, d = q.shape
    c = kc.shape[2]
    nb = l // ATT_BLOCK
    qspec = pl.BlockSpec((1, ATT_HEADS, ATT_BLOCK, d), lambda i, j: (i, 0, j, 0))
    cspec = pl.BlockSpec((1, ATT_KV_HEADS, c, d), lambda i, j: (i, 0, 0, 0))
    in_specs = [pl.BlockSpec(memory_space=pltpu.SMEM), qspec, cspec, cspec]
    args = [sink, q, kc, vc]
    if local:
        kv = lambda f: pl.BlockSpec((1, ATT_KV_HEADS, ATT_BLOCK, d), lambda i, j: (i, 0, f(j), 0))
        band = [kv(lambda j: jnp.maximum(j - 1, 0)), kv(lambda j: j), kv(lambda j: jnp.minimum(j + 1, nb - 1))]
        in_specs += band + band
        args += [k, k, k, v, v, v]
        qpos = jnp.arange(ATT_BLOCK)[:, None]
        kpos = jnp.arange(-ATT_BLOCK, 2 * ATT_BLOCK)[None, :]
        near = jnp.abs(kpos - qpos) <= WINDOW
        variants = [near, near & (kpos >= 0), near & (kpos < ATT_BLOCK), near & (kpos >= 0) & (kpos < ATT_BLOCK)]
        bias = jnp.stack([jnp.concatenate([jnp.zeros((ATT_BLOCK, c), F32), jnp.where(ok, 0.0, NEG_INF)], axis=1)
                          for ok in variants])
        in_specs.append(pl.BlockSpec((1, ATT_BLOCK, c + 3 * ATT_BLOCK),
                                     lambda i, j: ((j == 0) + 2 * (j == nb - 1), 0, 0)))
        args.append(bias)
    return pl.pallas_call(
        functools.partial(_attention_kernel, local=local),
        grid=(b, nb),
        in_specs=in_specs,
        out_specs=pl.BlockSpec((1, ATT_BLOCK, ATT_HEADS * d), lambda i, j: (i, j, 0)),
        out_shape=jax.ShapeDtypeStruct((b, l, ATT_HEADS * d), BF16),
        compiler_params=_cparams("parallel", "parallel"),
        name="window_attention" if local else "context_attention",
    )(*args)


def hyena_filters(L, lp):
    t = jnp.arange(L, dtype=F32)
    tn = t / L
    bands = jnp.arange(1, HY_BANDS + 1, dtype=F32)
    ang = 2.0 * math.pi * tn[:, None] * bands[None, :]
    z = jnp.concatenate([tn[:, None], jnp.cos(ang), jnp.sin(ang)], axis=-1)
    hdn = jnp.sin(lp['hy_freq1'] * (z @ lp['hy_w1'] + lp['hy_b1']))
    hdn = jnp.sin(lp['hy_freq2'] * (hdn @ lp['hy_w2'] + lp['hy_b2']))
    filt = hdn @ lp['hy_w3']
    rate = jnp.linspace(HY_FAST_RATE, HY_SLOW_RATE, HY_WIDTH, dtype=F32)
    tw = jnp.linspace(0.0, 1.0, L, dtype=F32)
    window = jnp.exp(-tw[:, None] * rate[None, :])
    return filt[:, :HY_WIDTH] * window, filt[:, HY_WIDTH:] * window


def _short_conv_kernel(z_ref, zp_ref, zn_ref, w_ref, b_ref, x0_ref, vg_ref, vgb_ref):
    i = pl.program_id(1)
    z = z_ref[0]
    tl, w3 = z.shape
    prev_row = jnp.where(i > 0, zp_ref[0, 7:8, :], 0.0)
    next_row = jnp.where(i < pl.num_programs(1) - 1, zn_ref[0, 0:1, :], 0.0)
    row = lax.broadcasted_iota(jnp.int32, (tl, w3), 0)
    z_prev = jnp.where(row == 0, prev_row, pltpu.roll(z, 1, axis=0))
    z_next = jnp.where(row == tl - 1, next_row, pltpu.roll(z, tl - 1, axis=0))
    y = b_ref[...] + z_prev * w_ref[0:1, :] + z * w_ref[1:2, :] + z_next * w_ref[2:3, :]
    w = w3 // HY_N_PROJ
    x0_ref[0] = y[:, :w]
    vg = y[:, 2 * w:] * y[:, w:2 * w]
    vg_ref[0] = vg
    vgb_ref[0] = vg.astype(BF16)


def hyena_short_conv(z, w, b):
    bsz, l, w3 = z.shape
    tl = min(l, 256)
    nb8 = tl // 8
    last8 = l // 8 - 1
    out = jax.ShapeDtypeStruct((bsz, l, w3 // HY_N_PROJ), F32)
    ospec = pl.BlockSpec((1, tl, w3 // HY_N_PROJ), lambda i, j: (i, j, 0))
    return pl.pallas_call(
        _short_conv_kernel,
        grid=(bsz, l // tl),
        in_specs=[pl.BlockSpec((1, tl, w3), lambda i, j: (i, j, 0)),
                  pl.BlockSpec((1, 8, w3), lambda i, j: (i, jnp.maximum(j * nb8 - 1, 0), 0)),
                  pl.BlockSpec((1, 8, w3), lambda i, j: (i, jnp.minimum((j + 1) * nb8, last8), 0)),
                  pl.BlockSpec((HY_SHORT, w3), lambda i, j: (0, 0)),
                  pl.BlockSpec((1, w3), lambda i, j: (0, 0))],
        out_specs=[ospec, ospec, ospec],
        out_shape=[out, out, jax.ShapeDtypeStruct(out.shape, BF16)],
        compiler_params=_cparams("parallel", "parallel"),
        name="hyena_short_conv",
    )(z, z, z, w, b.reshape(1, w3))


def dft_tables(l):
    n = 2 * l
    split = 64
    t = jnp.arange(l, dtype=jnp.int32)[None, :]
    kh = jnp.arange(l // split, dtype=jnp.int32)[:, None]
    kl = jnp.arange(split, dtype=jnp.int32)[:, None]
    ang_h = (2.0 * math.pi / n) * (((kh * split) * t) % n).astype(F32)
    ang_l = (2.0 * math.pi / n) * ((kl * t) % n).astype(F32)
    ch, sh = jnp.cos(ang_h)[:, None, :], jnp.sin(ang_h)[:, None, :]
    cl, sl = jnp.cos(ang_l)[None, :, :], jnp.sin(ang_l)[None, :, :]
    cos_t = (ch * cl - sh * sl).reshape(l, l).astype(BF16)
    sin_t = (sh * cl + ch * sl).reshape(l, l).astype(BF16)
    return cos_t, sin_t


def _filter_spectrum_kernel(c_ref, s_ref, hp_ref, hm_ref, kre_ref, kim_ref):
    kre_ref[...] = jnp.dot(c_ref[...], hp_ref[...], preferred_element_type=F32)
    kim_ref[...] = -jnp.dot(s_ref[...], hm_ref[...], preferred_element_type=F32)


def filter_spectrum(cos_t, sin_t, h_fwd, h_bwd):
    l, w = h_fwd.shape
    hb0 = h_bwd.at[0].set(0.0)
    tk = 512
    tab = pl.BlockSpec((tk, l), lambda k: (k, 0))
    full = pl.BlockSpec((l, w), lambda k: (0, 0))
    ospec = pl.BlockSpec((tk, w), lambda k: (k, 0))
    out = jax.ShapeDtypeStruct((l, w), F32)
    kre, kim = pl.pallas_call(
        _filter_spectrum_kernel,
        grid=(l // tk,),
        in_specs=[tab, tab, full, full],
        out_specs=[ospec, ospec],
        out_shape=[out, out],
        compiler_params=_cparams("parallel"),
        name="hyena_filter_spectrum",
    )(cos_t, sin_t, (h_fwd + hb0).astype(BF16), (h_fwd - hb0).astype(BF16))
    sign = jnp.where(jnp.arange(l) % 2 == 0, 1.0, -1.0)[:, None]
    k_nyq = jnp.sum(sign * (h_fwd + hb0), axis=0, keepdims=True)
    return kre, kim, k_nyq


def _long_conv_kernel(x_ref, crow_ref, srow_ref, ccol_ref, scol_ref, kre_ref, kim_ref, knyq_ref, y_ref,
                      acc_scr, xn_scr):
    k = pl.program_id(1)
    nk = pl.num_programs(1)
    x = x_ref[0]
    l = x.shape[0]
    tk = crow_ref.shape[0]

    @pl.when(k == 0)
    def _():
        acc_scr[...] = jnp.zeros_like(acc_scr)
        t_idx = lax.broadcasted_iota(jnp.int32, (8, l), 1)
        alt = jnp.where(t_idx % 2 == 0, 1.0, -1.0).astype(BF16)
        xn_scr[...] = jnp.dot(alt, x, preferred_element_type=F32)

    xre = jnp.dot(crow_ref[...], x, preferred_element_type=F32)
    xim = -jnp.dot(srow_ref[...], x, preferred_element_type=F32)
    kre, kim = kre_ref[...], kim_ref[...]
    k_idx = lax.broadcasted_iota(jnp.int32, (tk, 1), 0) + k * tk
    weight = jnp.where(k_idx == 0, 1.0, 2.0)
    yre = ((xre * kre - xim * kim) * weight).astype(BF16)
    nyim = ((xre * kim + xim * kre) * -weight).astype(BF16)
    acc_scr[...] += (jnp.dot(ccol_ref[...], yre, preferred_element_type=F32)
                     + jnp.dot(scol_ref[...], nyim, preferred_element_type=F32))

    @pl.when(k == nk - 1)
    def _():
        t_idx = lax.broadcasted_iota(jnp.int32, (l, 1), 0)
        nyq = xn_scr[0:1, :] * knyq_ref[...]
        y_ref[0] = (acc_scr[...] + jnp.where(t_idx % 2 == 0, nyq, -nyq)) * (0.5 / l)


def long_conv(x, cos_t, sin_t, kre, kim, k_nyq):
    b, l, w = x.shape
    tk = 256
    row_tab = pl.BlockSpec((tk, l), lambda i, k: (k, 0))
    col_tab = pl.BlockSpec((l, tk), lambda i, k: (0, k))
    kspec = pl.BlockSpec((tk, w), lambda i, k: (k, 0))
    return pl.pallas_call(
        _long_conv_kernel,
        grid=(b, l // tk),
        in_specs=[pl.BlockSpec((1, l, w), lambda i, k: (i, 0, 0)), row_tab, row_tab, col_tab, col_tab,
                  kspec, kspec, pl.BlockSpec((1, w), lambda i, k: (0, 0))],
        out_specs=pl.BlockSpec((1, l, w), lambda i, k: (i, 0, 0)),
        out_shape=jax.ShapeDtypeStruct((b, l, w), F32),
        scratch_shapes=[pltpu.VMEM((l, w), F32), pltpu.VMEM((8, w), F32)],
        compiler_params=_cparams("parallel", "arbitrary"),
        name="hyena_long_conv",
    )(x, cos_t, sin_t, cos_t, sin_t, kre, kim, k_nyq)


def _hyena_gate_kernel(y_ref, vg_ref, x0_ref, bias_ref, o_ref):
    o_ref[...] = ((y_ref[...] + vg_ref[...] * bias_ref[...]) * x0_ref[...]).astype(o_ref.dtype)


def hyena_gate(y, vg, x0, bias):
    b, l, w = y.shape
    m = b * l
    tm = 1024
    row = pl.BlockSpec((tm, w), lambda i: (i, 0))
    out = pl.pallas_call(
        _hyena_gate_kernel,
        grid=(m // tm,),
        in_specs=[row, row, row, pl.BlockSpec((1, w), lambda i: (0, 0))],
        out_specs=row,
        out_shape=jax.ShapeDtypeStruct((m, w), BF16),
        compiler_params=_cparams("parallel"),
        name="hyena_gate",
    )(y.reshape(m, w), vg.reshape(m, w), x0.reshape(m, w), bias.reshape(1, w))
    return out.reshape(b, l, w)


def _dense_dft_tables(l):
    n = 2 * l
    ang = 2.0 * np.pi * np.outer(np.arange(n), np.arange(n)) / n
    fwd = np.concatenate([np.cos(ang), -np.sin(ang)], axis=0)
    inv = np.concatenate([np.cos(ang[:l]), -np.sin(ang[:l])], axis=1) / n
    return jnp.asarray(fwd, F32), jnp.asarray(inv, F32)


def _dense_conv_kernel(vg_ref, x0_ref, kern_ref, f_ref, g_ref, bias_ref, o_ref):
    hp = lax.Precision.HIGHEST
    l = vg_ref.shape[1]
    n = 2 * l
    vg = vg_ref[0]
    ks = jnp.dot(f_ref[...], kern_ref[...], preferred_element_type=F32, precision=hp)
    xs = jnp.dot(f_ref[:, :l], vg, preferred_element_type=F32, precision=hp)
    kre, kim, xre, xim = ks[:n], ks[n:], xs[:n], xs[n:]
    ys = jnp.concatenate([xre * kre - xim * kim, xre * kim + xim * kre], axis=0)
    y = jnp.dot(g_ref[...], ys, preferred_element_type=F32, precision=hp)
    o_ref[0] = ((y + vg * bias_ref[...]) * x0_ref[0]).astype(o_ref.dtype)


def hyena_dense_conv(vg, x0, kern, bias):
    bsz, l, w = vg.shape
    fwd, inv = _dense_dft_tables(l)
    row = pl.BlockSpec((1, l, w), lambda i: (i, 0, 0))
    full = lambda a: pl.BlockSpec(a.shape, lambda i: (0, 0))
    return pl.pallas_call(
        _dense_conv_kernel,
        grid=(bsz,),
        in_specs=[row, row, full(kern), full(fwd), full(inv), pl.BlockSpec((1, w), lambda i: (0, 0))],
        out_specs=row,
        out_shape=jax.ShapeDtypeStruct((bsz, l, w), BF16),
        compiler_params=_cparams("parallel"),
        name="hyena_dense_conv",
    )(vg, x0, kern, fwd, inv, bias.reshape(1, w))


HY_DENSE_MAX_LEN = 512


def hyena_mixer(z, lp, tables=None):
    l = z.shape[1]
    x0, vg, vg_bf16 = hyena_short_conv(z, lp['hy_short_w'], lp['hy_short_b'])
    h_fwd, h_bwd = hyena_filters(l, lp)
    if l <= HY_DENSE_MAX_LEN:
        kern = jnp.concatenate([h_fwd, jnp.zeros_like(h_fwd[:1]), h_bwd[1:][::-1]], axis=0)
        return hyena_dense_conv(vg, x0, kern, lp['hy_bias'])
    cos_t, sin_t = tables if tables is not None else dft_tables(l)
    kre, kim, k_nyq = filter_spectrum(cos_t, sin_t, h_fwd, h_bwd)
    y = long_conv(vg_bf16, cos_t, sin_t, kre, kim, k_nyq)
    return hyena_gate(y, vg, x0, lp['hy_bias'])


S5_CHUNK = 16
S5_CK = S5_CHUNK * S5_GROUP
S5_OCT = LANES // S5_GROUP
S5_OCT_W = S5_OCT * S5_GROUP
S5_N_OCT = S5_GROUPS // S5_OCT
S5_ROW = S5_CHUNK * S5_OCT_W


def _s5_powers(lp, d):
    lam = lax.complex(jnp.minimum(lp['s5_a_re'][d], -1e-4), lp['s5_a_im'][d])
    dt = jnp.exp(lp['s5_log_dt'][d])[:, None]
    n = jnp.arange(S5_CHUNK + 1, dtype=F32)[:, None, None]
    apow = jnp.exp((lam * dt)[None] * n)
    b = lax.complex(lp['s5_b_re'][d], lp['s5_b_im'][d])
    bbar = ((apow[1] - 1.0) / lam)[..., None] * b
    cmat = lax.complex(lp['s5_c_re'][d], lp['s5_c_im'][d])
    return apow, bbar, cmat


def _block_diag_kernel(x_ref, o_ref, *, rb, cb):
    n_g, rc, cc = x_ref.shape
    out_c = o_ref.shape[2]
    col = lax.broadcasted_iota(jnp.int32, (cc, out_c), 1)
    src = lax.broadcasted_iota(jnp.int32, (cc, out_c), 0)
    for g in range(n_g):
        dest = (src // cb) * (n_g * cb) + g * cb + src % cb
        place = jnp.where(col == dest, 1.0, 0.0).astype(BF16)
        wide = jnp.dot(x_ref[g], place, preferred_element_type=F32).astype(BF16)
        for r in range(rc // rb):
            r0 = (r * n_g + g) * rb
            o_ref[0, r0:r0 + rb, :] = wide[r * rb:(r + 1) * rb, :]


def s5_block_diag(x, rb, cb):
    g_n, rc, cc = x.shape
    out_r, out_c = rc * S5_OCT, cc * S5_OCT
    return pl.pallas_call(
        functools.partial(_block_diag_kernel, rb=rb, cb=cb),
        grid=(g_n // S5_OCT,),
        in_specs=[pl.BlockSpec((S5_OCT, rc, cc), lambda o: (o, 0, 0))],
        out_specs=pl.BlockSpec((1, out_r, out_c), lambda o: (o, 0, 0)),
        out_shape=jax.ShapeDtypeStruct((g_n // S5_OCT, out_r, out_c), BF16),
        compiler_params=_cparams("parallel"),
        name="s5_block_diag",
    )(x)


def s5_prepare(lp):
    q = S5_CHUNK
    g_n, p_n = S5_GROUPS, S5_STATE
    hp = lax.Precision.HIGHEST
    ap_f, bb_f, c_f = _s5_powers(lp, 0)
    ap_b, bb_b, c_b = _s5_powers(lp, 1)
    k_f = jnp.einsum('gop,tgp,gpi->gtio', c_f, ap_f[:q], bb_f, precision=hp).real
    k_b = jnp.einsum('gop,tgp,gpi->gtio', c_b, ap_b[:q], bb_b, precision=hp).real
    zero = jnp.zeros_like(k_f)
    fwd_ext = jnp.concatenate([zero, k_f], axis=1)
    bwd_ext = jnp.concatenate([k_b[:, ::-1], zero], axis=1)
    m_f = jnp.stack([fwd_ext[:, q - sp:2 * q - sp] for sp in range(q)], axis=1)
    m_b = jnp.stack([bwd_ext[:, q - 1 - sp:2 * q - 1 - sp] for sp in range(q)], axis=1)
    m_intra = (m_f + m_b).transpose(0, 1, 3, 2, 4).reshape(g_n, S5_CK, S5_CK)

    ws_f = ap_f[q - 1 - jnp.arange(q)][:, :, :, None] * bb_f[None]
    ws_b = ap_b[jnp.arange(q)][:, :, :, None] * bb_b[None]
    ws = jnp.stack([ws_f.real, ws_f.imag, ws_b.real, ws_b.imag], axis=0)
    ws = ws.transpose(2, 1, 4, 0, 3).reshape(g_n, S5_CK, 4, p_n)

    ca_f = c_f[None] * ap_f[1:q + 1][:, :, None, :]
    ca_b = c_b[None] * ap_b[q - jnp.arange(q)][:, :, None, :]
    wo = jnp.stack([ca_f.real, -ca_f.imag, ca_b.real, -ca_b.imag], axis=0)
    wo = wo.transpose(2, 0, 4, 1, 3).reshape(g_n, 4, p_n, S5_CK)

    h_n = S5_GROUP
    m_oct = s5_block_diag(m_intra.astype(BF16), h_n, h_n)
    w_state = s5_block_diag(ws.reshape(g_n, S5_CK, 4 * p_n).astype(BF16), h_n, p_n)
    w_out = s5_block_diag(wo.reshape(g_n, 4 * p_n, S5_CK).astype(BF16), p_n, h_n)

    def flat(z):
        return jnp.stack([z.real.reshape(-1), z.imag.reshape(-1)], axis=0)

    return m_oct, w_state, w_out, flat(ap_f[q]), flat(ap_b[q])


def _s5_chunk_rows(u_ref, n_chunks):
    steps = [u_ref[0, pl.ds(s, n_chunks, stride=S5_CHUNK), :].astype(BF16) for s in range(S5_CHUNK)]
    return jnp.concatenate(steps, axis=1)


def _s5_states_kernel(u_ref, w_ref, fre_ref, fim_ref, bre_ref, bim_ref):
    n_chunks, n = fre_ref.shape[1:]
    o = jnp.dot(_s5_chunk_rows(u_ref, n_chunks), w_ref[0], preferred_element_type=F32)
    for comp, ref in enumerate((fre_ref, fim_ref, bre_ref, bim_ref)):
        ref[0] = o[:, comp * n:(comp + 1) * n]


def s5_states(u_all, w_state):
    b, t, w = u_all.shape
    nc = t // S5_CHUNK
    pw = S5_OCT * S5_STATE
    out = jax.ShapeDtypeStruct((b, nc, S5_GROUPS * S5_STATE), F32)
    ospec = pl.BlockSpec((1, nc, pw), lambda o, i: (i, 0, o))
    return pl.pallas_call(
        _s5_states_kernel,
        grid=(S5_N_OCT, b),
        in_specs=[pl.BlockSpec((1, t, S5_OCT_W), lambda o, i: (i, 0, o)),
                  pl.BlockSpec((1, S5_ROW, 4 * pw), lambda o, i: (o, 0, 0))],
        out_specs=[ospec] * 4,
        out_shape=[out] * 4,
        compiler_params=_cparams("parallel", "parallel"),
        name="s5_states",
    )(u_all, w_state)


def _s5_scan_kernel(sre_ref, sim_ref, a_ref, hre_ref, him_ref, *, n_ctx_tiles, reverse):
    tile = 8
    n_tiles = sre_ref.shape[1] // tile
    ar = a_ref[0:1, :]
    ai = a_ref[1:2, :]
    zero = jnp.zeros((1, sre_ref.shape[2]), F32)

    def body(n, carry):
        hre, him = carry
        if reverse:
            t = jnp.where(n < n_ctx_tiles, n_ctx_tiles - 1 - n, n_tiles + n_ctx_tiles - 1 - n)
        else:
            t = n
        r0 = pl.multiple_of(t * tile, tile)
        sre = sre_ref[0, pl.ds(r0, tile), :]
        sim = sim_ref[0, pl.ds(r0, tile), :]
        in_re = [None] * tile
        in_im = [None] * tile
        for r in (range(tile - 1, -1, -1) if reverse else range(tile)):
            in_re[r], in_im[r] = hre, him
            hre, him = ar * hre - ai * him + sre[r:r + 1], ar * him + ai * hre + sim[r:r + 1]
        hre_ref[0, pl.ds(r0, tile), :] = jnp.concatenate(in_re, axis=0)
        him_ref[0, pl.ds(r0, tile), :] = jnp.concatenate(in_im, axis=0)
        return hre, him

    lax.fori_loop(0, n_tiles, body, (zero, zero))


def s5_scan(sre, sim, a_chunk, n_ctx_chunks, reverse):
    b, nc, w = sre.shape
    tl = 512
    spec = pl.BlockSpec((1, nc, tl), lambda i, j: (i, 0, j))
    out = jax.ShapeDtypeStruct((b, nc, w), F32)
    return pl.pallas_call(
        functools.partial(_s5_scan_kernel, n_ctx_tiles=n_ctx_chunks // 8, reverse=reverse),
        grid=(b, w // tl),
        in_specs=[spec, spec, pl.BlockSpec((2, tl), lambda i, j: (0, j))],
        out_specs=[spec, spec],
        out_shape=[out, out],
        compiler_params=_cparams("parallel", "parallel"),
        name="s5_scan_bwd" if reverse else "s5_scan_fwd",
    )(sre, sim, a_chunk)


def _s5_outputs_kernel(u_ref, m_ref, fre_ref, fim_ref, bre_ref, bim_ref, wo_ref, y_ref):
    n_chunks = fre_ref.shape[1]
    carried = jnp.concatenate([r[0].astype(BF16) for r in (fre_ref, fim_ref, bre_ref, bim_ref)], axis=1)
    y = (jnp.dot(_s5_chunk_rows(u_ref, n_chunks), m_ref[0], preferred_element_type=F32)
         + jnp.dot(carried, wo_ref[0], preferred_element_type=F32))
    for s in range(S5_CHUNK):
        y_ref[0, pl.ds(s, n_chunks, stride=S5_CHUNK), :] = y[:, s * S5_OCT_W:(s + 1) * S5_OCT_W]


def s5_outputs(u_all, m_oct, carried, w_out):
    b, t, w = u_all.shape
    nc = t // S5_CHUNK
    pw = S5_OCT * S5_STATE
    hspec = pl.BlockSpec((1, nc, pw), lambda o, i: (i, 0, o))
    lane_blk = pl.BlockSpec((1, t, S5_OCT_W), lambda o, i: (i, 0, o))
    mat = pl.BlockSpec((1, S5_ROW, S5_ROW), lambda o, i: (o, 0, 0))
    return pl.pallas_call(
        _s5_outputs_kernel,
        grid=(S5_N_OCT, b),
        in_specs=[lane_blk, mat, hspec, hspec, hspec, hspec, mat],
        out_specs=lane_blk,
        out_shape=jax.ShapeDtypeStruct((b, t, w), F32),
        compiler_params=_cparams("parallel", "parallel"),
        name="s5_outputs",
    )(u_all, m_oct, *carried, w_out)


def _s5_readout_kernel(y_ref, u_ref, d_ref, w_ref, b_ref, o_ref):
    y = _gelu_tanh(y_ref[...] + d_ref[...] * u_ref[...])
    z = jnp.dot(y.astype(BF16), w_ref[...], preferred_element_type=F32) + b_ref[...]
    o_ref[...] = (y * jax.nn.sigmoid(z)).astype(o_ref.dtype)


def s5_readout(y, u, d, glu_w, glu_b):
    m, w = y.shape
    tm = min(m, 512)
    row = pl.BlockSpec((tm, w), lambda i: (i, 0))
    vec = pl.BlockSpec((1, w), lambda i: (0, 0))
    return pl.pallas_call(
        _s5_readout_kernel,
        grid=(m // tm,),
        in_specs=[row, row, vec, pl.BlockSpec((w, w), lambda i: (0, 0)), vec],
        out_specs=row,
        out_shape=jax.ShapeDtypeStruct((m, w), BF16),
        compiler_params=_cparams("parallel"),
        name="s5_readout",
    )(y, u, d.reshape(1, w), glu_w, glu_b.reshape(1, w))


def s5_mixer(u, uc, lp, glu_w):
    b, l, w = u.shape
    c = uc.shape[1]
    m_oct, w_state, w_out, a_f, a_b = s5_prepare(lp)
    u_all = jnp.concatenate([uc, u], axis=1)
    fre, fim, bre, bim = s5_states(u_all, w_state)
    hf = s5_scan(fre, fim, a_f, c // S5_CHUNK, False)
    hb = s5_scan(bre, bim, a_b, c // S5_CHUNK, True)
    y_all = s5_outputs(u_all, m_oct, (*hf, *hb), w_out)
    out = s5_readout(y_all.reshape(b * (c + l), w), u_all.reshape(b * (c + l), w), lp['s5_d'], glu_w,
                     lp['s5_glu_b'])
    out = out.reshape(b, c + l, w)
    return out[:, c:], out[:, :c]


def kernel(x, c, ctx, c_ctx, mod_w, mod_b, norm1_g, norm2_g, in_w, gate_b, attn_sink, hy_short_w, hy_short_b,
           hy_w1, hy_b1, hy_freq1, hy_w2, hy_b2, hy_freq2, hy_w3, hy_bias, s5_a_re, s5_a_im, s5_log_dt,
           s5_b_re, s5_b_im, s5_c_re, s5_c_im, s5_d, s5_glu_w, s5_glu_b, br_attn_w, br_hyena_w, br_s5_w,
           out_w, peer_wq, peer_keys, peer_u, peer_v, final_g):
    B, L, D = x.shape
    C = ctx.shape[1]
    depth = mod_w.shape[0]
    rope_cos, rope_sin = rope_tables(L)
    conv_tables = dft_tables(L) if L > HY_DENSE_MAX_LEN else None

    cvec = jnp.zeros((8, D), F32).at[:B].set(c).at[B].set(c_ctx)
    mod_all = modulation(cvec, mod_w, mod_b)

    xc = ctx
    for layer in range(depth):
        need_ctx = layer < depth - 1
        lp = dict(attn_sink=attn_sink[layer],
                  hy_short_w=hy_short_w[layer], hy_short_b=hy_short_b[layer],
                  hy_w1=hy_w1[layer], hy_b1=hy_b1[layer], hy_freq1=hy_freq1[layer],
                  hy_w2=hy_w2[layer], hy_b2=hy_b2[layer], hy_freq2=hy_freq2[layer],
                  hy_w3=hy_w3[layer], hy_bias=hy_bias[layer],
                  s5_a_re=s5_a_re[layer], s5_a_im=s5_a_im[layer], s5_log_dt=s5_log_dt[layer],
                  s5_b_re=s5_b_re[layer], s5_b_im=s5_b_im[layer],
                  s5_c_re=s5_c_re[layer], s5_c_im=s5_c_im[layer], s5_d=s5_d[layer],
                  s5_glu_w=s5_glu_w[layer], s5_glu_b=s5_glu_b[layer])
        mod = mod_all[layer]
        ml = [mod[:B, i * D:(i + 1) * D] for i in range(N_MOD)]
        mc = [mod[B:B + 1, i * D:(i + 1) * D] for i in range(N_MOD)]

        w_in = in_w[layer].astype(BF16)
        w_parts = [w_in[:, COL_Q:COL_S5], w_in[:, COL_S5:COL_HY], w_in[:, COL_HY:COL_GATE], w_in[:, COL_GATE:]]
        wa = br_attn_w[layer].astype(BF16)
        wh = br_hyena_w[layer].astype(BF16)
        ws = br_s5_w[layer].astype(BF16)
        wo = out_w[layer].astype(BF16)
        wqt = peer_wq[layer].T.astype(BF16)
        keys = peer_keys[layer].reshape(2 * PEER_HEADS, PEER_KEYS, PEER_HALF).astype(BF16)
        u_tab = peer_u[layer].astype(BF16)
        vt_tab = peer_v[layer].T.astype(BF16)

        h = norm_modulate(x, norm1_g[layer], ml[0], ml[1])
        hc = norm_modulate(xc, norm1_g[layer], mc[0], mc[1])
        qkv, u_s5, z_hy, gates = in_projection(h.reshape(B * L, D), w_parts, gate_b[layer])
        qkv_c, uc_s5, zc_hy, gates_c = in_projection(hc.reshape(B * C, D), w_parts, gate_b[layer])
        qkv = qkv.reshape(B, L, -1)
        qkv_c = qkv_c.reshape(B, C, -1)
        u_s5 = u_s5.reshape(B, L, -1)
        uc_s5 = uc_s5.reshape(B, C, -1)

        q, k, v = qkv_heads(qkv, rope_cos, rope_sin, True)
        qc, kc, vc = qkv_heads(qkv_c, rope_cos, rope_sin, False)
        att = attention(q, k, v, kc, vc, lp['attn_sink'], True)
        hy = hyena_mixer(z_hy.reshape(B, L, -1), lp, conv_tables)
        s5, s5c = s5_mixer(u_s5, uc_s5, lp, s5_glu_w[layer].astype(BF16))

        x = branch_merge(att, hy, s5, gates.reshape(B, L, -1), wa, wh, ws, wo, x, ml[2])
        x = peer_block(x, norm2_g[layer], ml[3], ml[4], ml[5], wqt, keys, u_tab, vt_tab)

        if need_ctx:
            att_c = attention(qc, None, None, kc, vc, lp['attn_sink'], False)
            hy_c = hyena_mixer(zc_hy.reshape(B, C, -1), lp)
            xc = branch_merge(att_c, hy_c, s5c, gates_c.reshape(B, C, -1), wa, wh, ws, wo, xc, mc[2])
            xc = peer_block(xc, norm2_g[layer], mc[3], mc[4], mc[5], wqt, keys, u_tab, vt_tab)
    return final_norm(x, final_g)
```

```python
import functools
import math

import jax
import jax.numpy as jnp
import numpy as np
from jax import lax
from jax.experimental import pallas as pl
from jax.experimental.pallas import tpu as pltpu

F32 = jnp.float32
BF16 = jnp.bfloat16

NORM_EPS = 1e-6
NEG_INF = -1e30
N_MOD = 6

GRID_W = 64
ATT_HEADS = 8
ATT_KV_HEADS = 2
ATT_GROUP = ATT_HEADS // ATT_KV_HEADS
HEAD_DIM = 64
ATT_WIDTH = ATT_HEADS * HEAD_DIM
KV_WIDTH = ATT_KV_HEADS * HEAD_DIM
WINDOW = 128
ATT_BLOCK = 128
ROPE_BASE = 10000.0
ROPE_PAIRS_PER_AXIS = HEAD_DIM // 4

HY_WIDTH = 512
HY_N_PROJ = 3
HY_SHORT = 3
HY_BANDS = 16
HY_DECAY_TARGET = 1e-2
HY_FAST_RATE = -math.log(HY_DECAY_TARGET) / 0.3
HY_SLOW_RATE = -math.log(HY_DECAY_TARGET) / 1.5

S5_WIDTH = 512
S5_GROUP = 16
S5_GROUPS = S5_WIDTH // S5_GROUP
S5_STATE = 64

N_BRANCH = 3

PEER_HEADS = 8
PEER_KEYS = 128
PEER_HALF = 128
PEER_TOPK = 16

COL_Q = 0
COL_K = COL_Q + ATT_WIDTH
COL_V = COL_K + KV_WIDTH
COL_S5 = COL_V + KV_WIDTH
COL_HY = COL_S5 + S5_WIDTH
COL_GATE = COL_HY + HY_N_PROJ * HY_WIDTH

VMEM_LIMIT_V7X = 56 * 1024 * 1024


def _cparams(*sem):
    return pltpu.CompilerParams(dimension_semantics=sem, vmem_limit_bytes=VMEM_LIMIT_V7X)


def _gelu_tanh(x):
    return 0.5 * x * (1.0 + jnp.tanh(math.sqrt(2.0 / math.pi) * (x + 0.044715 * (x * x * x))))


def _mod_kernel(s_ref, w_ref, b_ref, o_ref):
    s = s_ref[...]
    s = (s * jax.nn.sigmoid(s)).astype(BF16)
    o_ref[0] = jnp.dot(s, w_ref[0].astype(BF16), preferred_element_type=F32) + b_ref[0]


def modulation(cvec, mod_w, mod_b):
    depth, d, n = mod_w.shape
    tn = 1024
    return pl.pallas_call(
        _mod_kernel,
        grid=(depth, n // tn),
        in_specs=[pl.BlockSpec((8, d), lambda l, j: (0, 0)),
                  pl.BlockSpec((1, d, tn), lambda l, j: (l, 0, j)),
                  pl.BlockSpec((1, 1, tn), lambda l, j: (l, 0, j))],
        out_specs=pl.BlockSpec((1, 8, tn), lambda l, j: (l, 0, j)),
        out_shape=jax.ShapeDtypeStruct((depth, 8, n), F32),
        compiler_params=_cparams("parallel", "parallel"),
        name="adaln_mod",
    )(cvec, mod_w, mod_b.reshape(depth, 1, n))


def _norm_mod_kernel(x_ref, g_ref, sh_ref, sc_ref, o_ref, *, transposed):
    x = x_ref[0]
    y = x * lax.rsqrt(jnp.mean(x * x, axis=-1, keepdims=True) + NORM_EPS)
    y = y * g_ref[...]
    y = y * (1.0 + sc_ref[0]) + sh_ref[0]
    if transposed:
        o_ref[...] = y.T.astype(o_ref.dtype)
    else:
        o_ref[0] = y.astype(o_ref.dtype)


def norm_modulate(x, g, shift, scale, transposed=False):
    b, l, d = x.shape
    tl = min(l, 512)
    nt = l // tl
    per_batch = shift.shape[0] == b and b > 1
    mod_map = (lambda i, j: (i, 0, 0)) if per_batch else (lambda i, j: (0, 0, 0))
    if transposed:
        out_spec = pl.BlockSpec((d, tl), lambda i, j: (0, i * nt + j))
        out_shape = jax.ShapeDtypeStruct((d, b * l), BF16)
    else:
        out_spec = pl.BlockSpec((1, tl, d), lambda i, j: (i, j, 0))
        out_shape = jax.ShapeDtypeStruct((b, l, d), BF16)
    return pl.pallas_call(
        functools.partial(_norm_mod_kernel, transposed=transposed),
        grid=(b, nt),
        in_specs=[pl.BlockSpec((1, tl, d), lambda i, j: (i, j, 0)),
                  pl.BlockSpec((1, d), lambda i, j: (0, 0)),
                  pl.BlockSpec((1, 1, d), mod_map),
                  pl.BlockSpec((1, 1, d), mod_map)],
        out_specs=out_spec,
        out_shape=out_shape,
        compiler_params=_cparams("parallel", "parallel"),
        name="norm_modulate_t" if transposed else "norm_modulate",
    )(x, g.reshape(1, d), shift.reshape(-1, 1, d), scale.reshape(-1, 1, d))


def _final_norm_kernel(x_ref, g_ref, o_ref):
    x = x_ref[...]
    y = x * lax.rsqrt(jnp.mean(x * x, axis=-1, keepdims=True) + NORM_EPS)
    o_ref[...] = y * g_ref[...]


def final_norm(x, g):
    b, l, d = x.shape
    x2 = x.reshape(b * l, d)
    tm = 512
    out = pl.pallas_call(
        _final_norm_kernel,
        grid=(b * l // tm,),
        in_specs=[pl.BlockSpec((tm, d), lambda i: (i, 0)), pl.BlockSpec((1, d), lambda i: (0, 0))],
        out_specs=pl.BlockSpec((tm, d), lambda i: (i, 0)),
        out_shape=jax.ShapeDtypeStruct((b * l, d), F32),
        compiler_params=_cparams("parallel"),
        name="final_norm",
    )(x2, g.reshape(1, d))
    return out.reshape(b, l, d)


def _inproj_kernel(h_ref, wqkv_ref, ws5_ref, why_ref, wg_ref, gb_ref, oqkv_ref, os5_ref, ohy_ref, og_ref):
    h = h_ref[...]
    oqkv_ref[...] = jnp.dot(h, wqkv_ref[...], preferred_element_type=F32)
    os5_ref[...] = jnp.dot(h, ws5_ref[...], preferred_element_type=F32)
    ohy_ref[...] = jnp.dot(h, why_ref[...], preferred_element_type=F32)
    og_ref[...] = (jnp.dot(h, wg_ref[...], preferred_element_type=F32) + gb_ref[...]).astype(og_ref.dtype)


def in_projection(h, w_parts, gate_b):
    m, d = h.shape
    tm = 256
    widths = [w.shape[1] for w in w_parts]
    w_specs = [pl.BlockSpec((d, n), lambda i: (0, 0)) for n in widths]
    return pl.pallas_call(
        _inproj_kernel,
        grid=(m // tm,),
        in_specs=[pl.BlockSpec((tm, d), lambda i: (i, 0))] + w_specs
                 + [pl.BlockSpec((1, widths[3]), lambda i: (0, 0))],
        out_specs=[pl.BlockSpec((tm, n), lambda i: (i, 0)) for n in widths],
        out_shape=[jax.ShapeDtypeStruct((m, n), F32) for n in widths[:3]]
                  + [jax.ShapeDtypeStruct((m, widths[3]), BF16)],
        compiler_params=_cparams("parallel"),
        name="in_projection",
    )(h, *w_parts, gate_b.reshape(1, -1))


def _merge_kernel(att_ref, hy_ref, s5_ref, gl_ref, wa_ref, wh_ref, ws_ref, wo_ref, x_ref, g1_ref, o_ref):
    d = wo_ref.shape[0]
    g = jax.nn.sigmoid(gl_ref[0].astype(F32))
    m = g[:, :d] * jnp.dot(att_ref[0], wa_ref[...], preferred_element_type=F32)
    m = m + g[:, d:2 * d] * jnp.dot(hy_ref[0], wh_ref[...], preferred_element_type=F32)
    m = m + g[:, 2 * d:] * jnp.dot(s5_ref[0], ws_ref[...], preferred_element_type=F32)
    y = jnp.dot(m.astype(BF16), wo_ref[...], preferred_element_type=F32)
    o_ref[0] = x_ref[0] + g1_ref[0] * y


def branch_merge(att, hy, s5, gate_logits, wa, wh, ws, wo, x, g1):
    b, l, d = x.shape
    tl = min(l, 256)
    wd = att.shape[-1]
    per_batch = g1.shape[0] == b and b > 1
    g_map = (lambda i, j: (i, 0, 0)) if per_batch else (lambda i, j: (0, 0, 0))
    row = lambda n: pl.BlockSpec((1, tl, n), lambda i, j: (i, j, 0))
    full = lambda a: pl.BlockSpec(a.shape, lambda i, j: (0, 0))
    return pl.pallas_call(
        _merge_kernel,
        grid=(b, l // tl),
        in_specs=[row(wd), row(wd), row(wd), row(N_BRANCH * d), full(wa), full(wh), full(ws), full(wo),
                  row(d), pl.BlockSpec((1, 1, d), g_map)],
        out_specs=row(d),
        out_shape=jax.ShapeDtypeStruct((b, l, d), F32),
        compiler_params=_cparams("parallel", "parallel"),
        name="branch_merge",
    )(att, hy, s5, gate_logits, wa, wh, ws, wo, x, g1.reshape(-1, 1, d))


def _topk_rows(s, k):
    rows = []
    for _ in range(k):
        m = jnp.max(s, axis=0, keepdims=True)
        rows.append(m)
        s = jnp.where(s >= m, NEG_INF, s)
    return rows


LANES = 128


def _peer_select_kernel(ht_ref, wqt_ref, keys_ref, rank_ref, cnt_ref, e1_ref, e2_ref, qt_scr, cand_scr):
    nh = rank_ref.shape[0]
    nk = PEER_KEYS
    pairs = [(p, q) for p in range(PEER_TOPK) for q in range(PEER_TOPK) if (p + 1) * (q + 1) <= PEER_TOPK]

    def project(hh):
        r0 = pl.multiple_of(hh * 2 * nk, 2 * nk)
        qt_scr[hh % 2] = jnp.dot(wqt_ref[pl.ds(r0, 2 * nk), :], ht_ref[...],
                                 preferred_element_type=F32).astype(BF16)

    project(0)

    def head(hh, carry):
        qt = qt_scr[hh % 2]
        s1 = jnp.dot(keys_ref[2 * hh], qt[:nk], preferred_element_type=F32)
        s2 = jnp.dot(keys_ref[2 * hh + 1], qt[nk:], preferred_element_type=F32)
        project(jnp.minimum(hh + 1, nh - 1))
        for tb in range(s1.shape[1] // LANES):
            lanes = slice(tb * LANES, (tb + 1) * LANES)
            select_block(hh, tb, lanes, s1[:, lanes], s2[:, lanes])
        return carry

    def select_block(hh, tb, lanes, s1, s2):
        a = _topk_rows(s1, PEER_TOPK)
        b = []
        rest = s2
        rank = jnp.full(s2.shape, float(PEER_TOPK), F32)
        for r in range(PEER_TOPK):
            m = jnp.max(rest, axis=0, keepdims=True)
            b.append(m)
            hit = rest >= m
            rank = jnp.where(hit, float(r), rank)
            rest = jnp.where(hit, NEG_INF, rest)
        cand_scr[:, lanes] = jnp.full((cand_scr.shape[0], LANES), NEG_INF, F32)
        for r, (p, q) in enumerate(pairs):
            cand_scr[r:r + 1, lanes] = a[p] + b[q]
        cand = cand_scr[:, lanes]
        best = _topk_rows(cand, PEER_TOPK)
        top = a[0] + b[0]
        z = jnp.exp(best[0] - top)
        for r in range(1, PEER_TOPK):
            z = z + jnp.exp(best[r] - top)
        tau = best[PEER_TOPK - 1]
        reached = jnp.where(cand >= tau, 1.0, 0.0)
        cnt = jnp.zeros(s1.shape, F32)
        r = 0
        for p in range(PEER_TOPK):
            n_q = sum(1 for pp, _ in pairs if pp == p)
            cnt_p = reached[r:r + 1]
            for rr in range(r + 1, r + n_q):
                cnt_p = cnt_p + reached[rr:rr + 1]
            r += n_q
            cnt = jnp.where(s1 == a[p], cnt_p, cnt)
        rank_ref[hh, :, lanes] = rank.astype(BF16)
        cnt_ref[hh, tb] = cnt
        e1_ref[hh, tb] = 0.5 * jnp.exp(s1 - a[0])
        e2_ref[hh, :, lanes] = (jnp.exp(s2 - b[0]) / z).astype(BF16)

    lax.fori_loop(0, nh, head, 0)


def peer_select(ht, wqt, keys, tt):
    d, m = ht.shape
    nh = keys.shape[0] // 2
    out = jax.ShapeDtypeStruct((nh, PEER_KEYS, m), BF16)
    out32 = jax.ShapeDtypeStruct((nh, m // LANES, PEER_KEYS, LANES), F32)
    ospec = pl.BlockSpec((nh, PEER_KEYS, tt), lambda i: (0, 0, i))
    ospec32 = pl.BlockSpec((nh, tt // LANES, PEER_KEYS, LANES), lambda i: (0, i, 0, 0))
    return pl.pallas_call(
        _peer_select_kernel,
        grid=(m // tt,),
        in_specs=[pl.BlockSpec((d, tt), lambda i: (0, i)),
                  pl.BlockSpec(wqt.shape, lambda i: (0, 0)),
                  pl.BlockSpec(keys.shape, lambda i: (0, 0, 0))],
        out_specs=[ospec, ospec32, ospec32, ospec],
        out_shape=[out, out32, out32, out],
        scratch_shapes=[pltpu.VMEM((2, 2 * PEER_KEYS, tt), BF16), pltpu.VMEM((56, tt), F32)],
        compiler_params=_cparams("parallel"),
        name="peer_select",
    )(ht, wqt, keys)


PEER_PAIR = 2 * PEER_KEYS
PEER_LOOKAHEAD = 2


def _half_gate_gelu(half_w, x):
    c = math.sqrt(2.0 / math.pi)
    p = half_w * x
    return p + p * jnp.tanh(x * (c + (c * 0.044715) * (x * x)))


def _key_row(ref, hh, ii):
    blocks = [jnp.broadcast_to(ref[hh, tb, ii:ii + 1, :], (PEER_KEYS, LANES)) for tb in range(ref.shape[1])]
    return jnp.concatenate(blocks, axis=1).astype(BF16)


def _peer_expert_kernel(ht_ref, u_ref, vt_ref, rank_ref, cnt_ref, e1_ref, e2_ref, x_ref, g2_ref, o_ref,
                        acc_scr, wa_scr):
    k = pl.program_id(1)
    nh = rank_ref.shape[0]

    @pl.when(k == 0)
    def _():
        acc_scr[...] = jnp.zeros_like(acc_scr)

    ht = ht_ref[...]
    n_pairs = u_ref.shape[0] // PEER_PAIR

    def cols(p):
        return slice(p * PEER_PAIR, (p + 1) * PEER_PAIR)

    def first_matmul(p):
        return jnp.dot(u_ref[cols(p), :], ht, preferred_element_type=F32)

    def gated(p, act):
        w_rows = []
        for r in range(2):
            ii = 2 * p + r
            w = None
            for hh in range(nh):
                selected = rank_ref[hh] < _key_row(cnt_ref, hh, ii)
                gate = jnp.where(selected, e2_ref[hh], jnp.zeros((), BF16)) * _key_row(e1_ref, hh, ii)
                w = gate if w is None else w + gate
            w_rows.append(w)
        half_w = jnp.concatenate(w_rows, axis=0)
        return _half_gate_gelu(half_w, act.astype(BF16))

    acts = {q: first_matmul(q) for q in range(min(PEER_LOOKAHEAD, n_pairs))}
    part = None
    for p in range(n_pairs + 1):
        if p + PEER_LOOKAHEAD < n_pairs:
            acts[p + PEER_LOOKAHEAD] = first_matmul(p + PEER_LOOKAHEAD)
        if p < n_pairs:
            wa_scr[p] = gated(p, acts.pop(p))
        if p >= 1:
            d = jnp.dot(vt_ref[:, cols(p - 1)], wa_scr[p - 1], preferred_element_type=F32)
            part = d if part is None else part + d
    acc_scr[...] += part

    @pl.when(k == pl.num_programs(1) - 1)
    def _():
        o_ref[...] = x_ref[...] + g2_ref[0] * acc_scr[...].T


def peer_experts(ht, u_tab, vt_tab, sel, x, g2, tokens_per_batch, tt, ec):
    d, m = ht.shape
    e = u_tab.shape[0]
    nh = sel[0].shape[0]
    rows_i = ec // PEER_KEYS
    once = pl.Buffered(1)
    full_keys = pl.BlockSpec((nh, PEER_KEYS, tt), lambda i, k: (0, 0, i), pipeline_mode=once)
    chunk_keys = pl.BlockSpec((nh, tt // LANES, rows_i, LANES), lambda i, k: (0, i, k, 0))
    tiles_per_batch = tokens_per_batch // tt
    if g2.shape[0] > 1:
        g_map = lambda i, k: (i // tiles_per_batch, 0, 0)
    else:
        g_map = lambda i, k: (0, 0, 0)
    rank, cnt, e1, e2 = sel
    return pl.pallas_call(
        _peer_expert_kernel,
        grid=(m // tt, e // ec),
        in_specs=[pl.BlockSpec((d, tt), lambda i, k: (0, i), pipeline_mode=once),
                  pl.BlockSpec((ec, d), lambda i, k: (k, 0)),
                  pl.BlockSpec((d, ec), lambda i, k: (0, k)),
                  full_keys, chunk_keys, chunk_keys, full_keys,
                  pl.BlockSpec((tt, d), lambda i, k: (i, 0), pipeline_mode=once),
                  pl.BlockSpec((1, 1, d), g_map)],
        out_specs=pl.BlockSpec((tt, d), lambda i, k: (i, 0)),
        out_shape=jax.ShapeDtypeStruct((m, d), F32),
        scratch_shapes=[pltpu.VMEM((d, tt), F32), pltpu.VMEM((ec // PEER_PAIR, PEER_PAIR, tt), BF16)],
        compiler_params=_cparams("parallel", "arbitrary"),
        name="peer_experts",
    )(ht, u_tab, vt_tab, rank, cnt, e1, e2, x, g2.reshape(-1, 1, d))


def peer_block(x, g_norm, shift, scale, gate, wqt, keys, u_tab, vt_tab):
    b, l, d = x.shape
    ht = norm_modulate(x, g_norm, shift, scale, transposed=True)
    tt = min(l, 512)
    sel = peer_select(ht, wqt, keys, tt)
    out = peer_experts(ht, u_tab, vt_tab, sel, x.reshape(b * l, d), gate, l, tt, 4096)
    return out.reshape(b, l, d)


def rope_tables(l):
    rows_n = l // GRID_W
    row = jnp.repeat(jnp.arange(rows_n), GRID_W).astype(F32)
    col = jnp.tile(jnp.arange(GRID_W), rows_n).astype(F32)
    inv = jnp.power(ROPE_BASE, -jnp.arange(ROPE_PAIRS_PER_AXIS, dtype=F32) / ROPE_PAIRS_PER_AXIS)
    ang = jnp.concatenate([row[:, None] * inv, col[:, None] * inv], axis=-1)
    c, s = jnp.cos(ang), jnp.sin(ang)
    return jnp.concatenate([c, c, c, c], axis=-1), jnp.concatenate([-s, s, -s, s], axis=-1)


def _rope_pair(x, cos2, sin2):
    half = HEAD_DIM // 2
    lane = lax.broadcasted_iota(jnp.int32, x.shape, 1)
    swapped = jnp.where(lane % HEAD_DIM < half, pltpu.roll(x, 2 * HEAD_DIM - half, axis=1),
                        pltpu.roll(x, half, axis=1))
    return x * cos2 + swapped * sin2


def _qkv_heads_kernel(qkv_ref, cos_ref, sin_ref, q_ref, k_ref, v_ref, *, rope):
    scale = HEAD_DIM ** -0.5
    x = qkv_ref[0]
    pair = 2 * HEAD_DIM
    for j in range((ATT_WIDTH + KV_WIDTH) // pair):
        blk = x[:, j * pair:(j + 1) * pair]
        if rope:
            blk = _rope_pair(blk, cos_ref[...], sin_ref[...])
        for t in range(2):
            head = blk[:, t * HEAD_DIM:(t + 1) * HEAD_DIM]
            h = 2 * j + t
            if h < ATT_HEADS:
                q_ref[0, h] = (head * scale).astype(q_ref.dtype)
            else:
                k_ref[0, h - ATT_HEADS] = head.astype(k_ref.dtype)
    for t in range(ATT_KV_HEADS):
        v_ref[0, t] = x[:, COL_V + t * HEAD_DIM:COL_V + (t + 1) * HEAD_DIM].astype(v_ref.dtype)


def qkv_heads(qkv, cos2, sin2, rope):
    b, l, w = qkv.shape
    tl = min(l, 256)
    tab = pl.BlockSpec((tl, 2 * HEAD_DIM), lambda i, j: (j, 0))
    hspec = lambda n: pl.BlockSpec((1, n, tl, HEAD_DIM), lambda i, j: (i, 0, j, 0))
    return pl.pallas_call(
        functools.partial(_qkv_heads_kernel, rope=rope),
        grid=(b, l // tl),
        in_specs=[pl.BlockSpec((1, tl, w), lambda i, j: (i, j, 0)), tab, tab],
        out_specs=[hspec(ATT_HEADS), hspec(ATT_KV_HEADS), hspec(ATT_KV_HEADS)],
        out_shape=[jax.ShapeDtypeStruct((b, ATT_HEADS, l, HEAD_DIM), BF16),
                   jax.ShapeDtypeStruct((b, ATT_KV_HEADS, l, HEAD_DIM), BF16),
                   jax.ShapeDtypeStruct((b, ATT_KV_HEADS, l, HEAD_DIM), BF16)],
        compiler_params=_cparams("parallel", "parallel"),
        name="qkv_heads",
    )(qkv, cos2, sin2)


def _attention_kernel(sink_ref, q_ref, kc_ref, vc_ref, *rest, local):
    if local:
        kp_ref, ko_ref, kn_ref, vp_ref, vo_ref, vn_ref, bias_ref, o_ref = rest
    else:
        (o_ref,) = rest
    blk = ATT_BLOCK
    rows = ATT_GROUP * blk
    outs = []
    for g in range(ATT_KV_HEADS):
        q = q_ref[0, g * ATT_GROUP:(g + 1) * ATT_GROUP].reshape(rows, HEAD_DIM)
        if local:
            keys = jnp.concatenate([kc_ref[0, g], kp_ref[0, g], ko_ref[0, g], kn_ref[0, g]], axis=0)
            vals = jnp.concatenate([vc_ref[0, g], vp_ref[0, g], vo_ref[0, g], vn_ref[0, g]], axis=0)
        else:
            keys, vals = kc_ref[0, g], vc_ref[0, g]
        s = lax.dot_general(q, keys, (((1,), (1,)), ((), ())), preferred_element_type=F32)
        r = lax.broadcasted_iota(jnp.int32, (rows, 1), 0)
        if local:
            s = s + jnp.concatenate([bias_ref[0]] * ATT_GROUP, axis=0)
        sink = jnp.zeros((rows, 1), F32)
        for t in range(ATT_GROUP):
            sink = jnp.where(r // blk == t, sink_ref[g * ATT_GROUP + t], sink)
        m = jnp.maximum(jnp.max(s, axis=-1, keepdims=True), sink)
        p = jnp.exp(s - m)
        denom = jnp.sum(p, axis=-1, keepdims=True) + jnp.exp(sink - m)
        o = jnp.dot(p.astype(BF16), vals, preferred_element_type=F32) / denom
        outs += [o[t * blk:(t + 1) * blk] for t in range(ATT_GROUP)]
    o_ref[0] = jnp.concatenate(outs, axis=1).astype(o_ref.dtype)


def attention(q, k, v, kc, vc, sink, local):
    b, _, l, d = q.shape
    c = kc.shape[2]
    nb = l // ATT_BLOCK
    qspec = pl.BlockSpec((1, ATT_HEADS, ATT_BLOCK, d), lambda i, j: (i, 0, j, 0))
    cspec = pl.BlockSpec((1, ATT_KV_HEADS, c, d), lambda i, j: (i, 0, 0, 0))
    in_specs = [pl.BlockSpec(memory_space=pltpu.SMEM), qspec, cspec, cspec]
    args = [sink, q, kc, vc]
    if local:
        kv = lambda f: pl.BlockSpec((1, ATT_KV_HEADS, ATT_BLOCK, d), lambda i, j: (i, 0, f(j), 0))
        band = [kv(lambda j: jnp.maximum(j - 1, 0)), kv(lambda j: j), kv(lambda j: jnp.minimum(j + 1, nb - 1))]
        in_specs += band + band
        args += [k, k, k, v, v, v]
        qpos = jnp.arange(ATT_BLOCK)[:, None]
        kpos = jnp.arange(-ATT_BLOCK, 2 * ATT_BLOCK)[None, :]
        near = jnp.abs(kpos - qpos) <= WINDOW
        variants = [near, near & (kpos >= 0), near & (kpos < ATT_BLOCK), near & (kpos >= 0) & (kpos < ATT_BLOCK)]
        bias = jnp.stack([jnp.concatenate([jnp.zeros((ATT_BLOCK, c), F32), jnp.where(ok, 0.0, NEG_INF)], axis=1)
                          for ok in variants])
        in_specs.append(pl.BlockSpec((1, ATT_BLOCK, c + 3 * ATT_BLOCK),
                                     lambda i, j: ((j == 0) + 2 * (j == nb - 1), 0, 0)))
        args.append(bias)
    return pl.pallas_call(
        functools.partial(_attention_kernel, local=local),
        grid=(b, nb),
        in_specs=in_specs,
        out_specs=pl.BlockSpec((1, ATT_BLOCK, ATT_HEADS * d), lambda i, j: (i, j, 0)),
        out_shape=jax.ShapeDtypeStruct((b, l, ATT_HEADS * d), BF16),
        compiler_params=_cparams("parallel", "parallel"),
        name="window_attention" if local else "context_attention",
    )(*args)


def hyena_filters(L, lp):
    t = jnp.arange(L, dtype=F32)
    tn = t / L
    bands = jnp.arange(1, HY_BANDS + 1, dtype=F32)
    ang = 2.0 * math.pi * tn[:, None] * bands[None, :]
    z = jnp.concatenate([tn[:, None], jnp.cos(ang), jnp.sin(ang)], axis=-1)
    hdn = jnp.sin(lp['hy_freq1'] * (z @ lp['hy_w1'] + lp['hy_b1']))
    hdn = jnp.sin(lp['hy_freq2'] * (hdn @ lp['hy_w2'] + lp['hy_b2']))
    filt = hdn @ lp['hy_w3']
    rate = jnp.linspace(HY_FAST_RATE, HY_SLOW_RATE, HY_WIDTH, dtype=F32)
    tw = jnp.linspace(0.0, 1.0, L, dtype=F32)
    window = jnp.exp(-tw[:, None] * rate[None, :])
    return filt[:, :HY_WIDTH] * window, filt[:, HY_WIDTH:] * window


def _short_conv_kernel(z_ref, zp_ref, zn_ref, w_ref, b_ref, x0_ref, vg_ref, vgb_ref):
    i = pl.program_id(1)
    z = z_ref[0]
    tl, w3 = z.shape
    prev_row = jnp.where(i > 0, zp_ref[0, 7:8, :], 0.0)
    next_row = jnp.where(i < pl.num_programs(1) - 1, zn_ref[0, 0:1, :], 0.0)
    row = lax.broadcasted_iota(jnp.int32, (tl, w3), 0)
    z_prev = jnp.where(row == 0, prev_row, pltpu.roll(z, 1, axis=0))
    z_next = jnp.where(row == tl - 1, next_row, pltpu.roll(z, tl - 1, axis=0))
    y = b_ref[...] + z_prev * w_ref[0:1, :] + z * w_ref[1:2, :] + z_next * w_ref[2:3, :]
    w = w3 // HY_N_PROJ
    x0_ref[0] = y[:, :w]
    vg = y[:, 2 * w:] * y[:, w:2 * w]
    vg_ref[0] = vg
    vgb_ref[0] = vg.astype(BF16)


def hyena_short_conv(z, w, b):
    bsz, l, w3 = z.shape
    tl = min(l, 256)
    nb8 = tl // 8
    last8 = l // 8 - 1
    out = jax.ShapeDtypeStruct((bsz, l, w3 // HY_N_PROJ), F32)
    ospec = pl.BlockSpec((1, tl, w3 // HY_N_PROJ), lambda i, j: (i, j, 0))
    return pl.pallas_call(
        _short_conv_kernel,
        grid=(bsz, l // tl),
        in_specs=[pl.BlockSpec((1, tl, w3), lambda i, j: (i, j, 0)),
                  pl.BlockSpec((1, 8, w3), lambda i, j: (i, jnp.maximum(j * nb8 - 1, 0), 0)),
                  pl.BlockSpec((1, 8, w3), lambda i, j: (i, jnp.minimum((j + 1) * nb8, last8), 0)),
                  pl.BlockSpec((HY_SHORT, w3), lambda i, j: (0, 0)),
                  pl.BlockSpec((1, w3), lambda i, j: (0, 0))],
        out_specs=[ospec, ospec, ospec],
        out_shape=[out, out, jax.ShapeDtypeStruct(out.shape, BF16)],
        compiler_params=_cparams("parallel", "parallel"),
        name="hyena_short_conv",
    )(z, z, z, w, b.reshape(1, w3))


def dft_tables(l):
    n = 2 * l
    split = 64
    t = jnp.arange(l, dtype=jnp.int32)[None, :]
    kh = jnp.arange(l // split, dtype=jnp.int32)[:, None]
    kl = jnp.arange(split, dtype=jnp.int32)[:, None]
    ang_h = (2.0 * math.pi / n) * (((kh * split) * t) % n).astype(F32)
    ang_l = (2.0 * math.pi / n) * ((kl * t) % n).astype(F32)
    ch, sh = jnp.cos(ang_h)[:, None, :], jnp.sin(ang_h)[:, None, :]
    cl, sl = jnp.cos(ang_l)[None, :, :], jnp.sin(ang_l)[None, :, :]
    cos_t = (ch * cl - sh * sl).reshape(l, l).astype(BF16)
    sin_t = (sh * cl + ch * sl).reshape(l, l).astype(BF16)
    return cos_t, sin_t


def _filter_spectrum_kernel(c_ref, s_ref, hp_ref, hm_ref, kre_ref, kim_ref):
    kre_ref[...] = jnp.dot(c_ref[...], hp_ref[...], preferred_element_type=F32)
    kim_ref[...] = -jnp.dot(s_ref[...], hm_ref[...], preferred_element_type=F32)


def filter_spectrum(cos_t, sin_t, h_fwd, h_bwd):
    l, w = h_fwd.shape
    hb0 = h_bwd.at[0].set(0.0)
    tk = 512
    tab = pl.BlockSpec((tk, l), lambda k: (k, 0))
    full = pl.BlockSpec((l, w), lambda k: (0, 0))
    ospec = pl.BlockSpec((tk, w), lambda k: (k, 0))
    out = jax.ShapeDtypeStruct((l, w), F32)
    kre, kim = pl.pallas_call(
        _filter_spectrum_kernel,
        grid=(l // tk,),
        in_specs=[tab, tab, full, full],
        out_specs=[ospec, ospec],
        out_shape=[out, out],
        compiler_params=_cparams("parallel"),
        name="hyena_filter_spectrum",
    )(cos_t, sin_t, (h_fwd + hb0).astype(BF16), (h_fwd - hb0).astype(BF16))
    sign = jnp.where(jnp.arange(l) % 2 == 0, 1.0, -1.0)[:, None]
    k_nyq = jnp.sum(sign * (h_fwd + hb0), axis=0, keepdims=True)
    return kre, kim, k_nyq


def _long_conv_kernel(x_ref, crow_ref, srow_ref, ccol_ref, scol_ref, kre_ref, kim_ref, knyq_ref, y_ref,
                      acc_scr, xn_scr):
    k = pl.program_id(1)
    nk = pl.num_programs(1)
    x = x_ref[0]
    l = x.shape[0]
    tk = crow_ref.shape[0]

    @pl.when(k == 0)
    def _():
        acc_scr[...] = jnp.zeros_like(acc_scr)
        t_idx = lax.broadcasted_iota(jnp.int32, (8, l), 1)
        alt = jnp.where(t_idx % 2 == 0, 1.0, -1.0).astype(BF16)
        xn_scr[...] = jnp.dot(alt, x, preferred_element_type=F32)

    xre = jnp.dot(crow_ref[...], x, preferred_element_type=F32)
    xim = -jnp.dot(srow_ref[...], x, preferred_element_type=F32)
    kre, kim = kre_ref[...], kim_ref[...]
    k_idx = lax.broadcasted_iota(jnp.int32, (tk, 1), 0) + k * tk
    weight = jnp.where(k_idx == 0, 1.0, 2.0)
    yre = ((xre * kre - xim * kim) * weight).astype(BF16)
    nyim = ((xre * kim + xim * kre) * -weight).astype(BF16)
    acc_scr[...] += (jnp.dot(ccol_ref[...], yre, preferred_element_type=F32)
                     + jnp.dot(scol_ref[...], nyim, preferred_element_type=F32))

    @pl.when(k == nk - 1)
    def _():
        t_idx = lax.broadcasted_iota(jnp.int32, (l, 1), 0)
        nyq = xn_scr[0:1, :] * knyq_ref[...]
        y_ref[0] = (acc_scr[...] + jnp.where(t_idx % 2 == 0, nyq, -nyq)) * (0.5 / l)


def long_conv(x, cos_t, sin_t, kre, kim, k_nyq):
    b, l, w = x.shape
    tk = 256
    row_tab = pl.BlockSpec((tk, l), lambda i, k: (k, 0))
    col_tab = pl.BlockSpec((l, tk), lambda i, k: (0, k))
    kspec = pl.BlockSpec((tk, w), lambda i, k: (k, 0))
    return pl.pallas_call(
        _long_conv_kernel,
        grid=(b, l // tk),
        in_specs=[pl.BlockSpec((1, l, w), lambda i, k: (i, 0, 0)), row_tab, row_tab, col_tab, col_tab,
                  kspec, kspec, pl.BlockSpec((1, w), lambda i, k: (0, 0))],
        out_specs=pl.BlockSpec((1, l, w), lambda i, k: (i, 0, 0)),
        out_shape=jax.ShapeDtypeStruct((b, l, w), F32),
        scratch_shapes=[pltpu.VMEM((l, w), F32), pltpu.VMEM((8, w), F32)],
        compiler_params=_cparams("parallel", "arbitrary"),
        name="hyena_long_conv",
    )(x, cos_t, sin_t, cos_t, sin_t, kre, kim, k_nyq)


def _hyena_gate_kernel(y_ref, vg_ref, x0_ref, bias_ref, o_ref):
    o_ref[...] = ((y_ref[...] + vg_ref[...] * bias_ref[...]) * x0_ref[...]).astype(o_ref.dtype)


def hyena_gate(y, vg, x0, bias):
    b, l, w = y.shape
    m = b * l
    tm = 1024
    row = pl.BlockSpec((tm, w), lambda i: (i, 0))
    out = pl.pallas_call(
        _hyena_gate_kernel,
        grid=(m // tm,),
        in_specs=[row, row, row, pl.BlockSpec((1, w), lambda i: (0, 0))],
        out_specs=row,
        out_shape=jax.ShapeDtypeStruct((m, w), BF16),
        compiler_params=_cparams("parallel"),
        name="hyena_gate",
    )(y.reshape(m, w), vg.reshape(m, w), x0.reshape(m, w), bias.reshape(1, w))
    return out.reshape(b, l, w)


def _dense_dft_tables(l):
    n = 2 * l
    ang = 2.0 * np.pi * np.outer(np.arange(n), np.arange(n)) / n
    fwd = np.concatenate([np.cos(ang), -np.sin(ang)], axis=0)
    inv = np.concatenate([np.cos(ang[:l]), -np.sin(ang[:l])], axis=1) / n
    return jnp.asarray(fwd, F32), jnp.asarray(inv, F32)


def _dense_conv_kernel(vg_ref, x0_ref, kern_ref, f_ref, g_ref, bias_ref, o_ref):
    hp = lax.Precision.HIGHEST
    l = vg_ref.shape[1]
    n = 2 * l
    vg = vg_ref[0]
    ks = jnp.dot(f_ref[...], kern_ref[...], preferred_element_type=F32, precision=hp)
    xs = jnp.dot(f_ref[:, :l], vg, preferred_element_type=F32, precision=hp)
    kre, kim, xre, xim = ks[:n], ks[n:], xs[:n], xs[n:]
    ys = jnp.concatenate([xre * kre - xim * kim, xre * kim + xim * kre], axis=0)
    y = jnp.dot(g_ref[...], ys, preferred_element_type=F32, precision=hp)
    o_ref[0] = ((y + vg * bias_ref[...]) * x0_ref[0]).astype(o_ref.dtype)


def hyena_dense_conv(vg, x0, kern, bias):
    bsz, l, w = vg.shape
    fwd, inv = _dense_dft_tables(l)
    row = pl.BlockSpec((1, l, w), lambda i: (i, 0, 0))
    full = lambda a: pl.BlockSpec(a.shape, lambda i: (0, 0))
    return pl.pallas_call(
        _dense_conv_kernel,
        grid=(bsz,),
        in_specs=[row, row, full(kern), full(fwd), full(inv), pl.BlockSpec((1, w), lambda i: (0, 0))],
        out_specs=row,
        out_shape=jax.ShapeDtypeStruct((bsz, l, w), BF16),
        compiler_params=_cparams("parallel"),
        name="hyena_dense_conv",
    )(vg, x0, kern, fwd, inv, bias.reshape(1, w))


HY_DENSE_MAX_LEN = 512


def hyena_mixer(z, lp, tables=None):
    l = z.shape[1]
    x0, vg, vg_bf16 = hyena_short_conv(z, lp['hy_short_w'], lp['hy_short_b'])
    h_fwd, h_bwd = hyena_filters(l, lp)
    if l <= HY_DENSE_MAX_LEN:
        kern = jnp.concatenate([h_fwd, jnp.zeros_like(h_fwd[:1]), h_bwd[1:][::-1]], axis=0)
        return hyena_dense_conv(vg, x0, kern, lp['hy_bias'])
    cos_t, sin_t = tables if tables is not None else dft_tables(l)
    kre, kim, k_nyq = filter_spectrum(cos_t, sin_t, h_fwd, h_bwd)
    y = long_conv(vg_bf16, cos_t, sin_t, kre, kim, k_nyq)
    return hyena_gate(y, vg, x0, lp['hy_bias'])


S5_CHUNK = 16
S5_CK = S5_CHUNK * S5_GROUP
S5_OCT = LANES // S5_GROUP
S5_OCT_W = S5_OCT * S5_GROUP
S5_N_OCT = S5_GROUPS // S5_OCT
S5_ROW = S5_CHUNK * S5_OCT_W


def _s5_powers(lp, d):
    lam = lax.complex(jnp.minimum(lp['s5_a_re'][d], -1e-4), lp['s5_a_im'][d])
    dt = jnp.exp(lp['s5_log_dt'][d])[:, None]
    n = jnp.arange(S5_CHUNK + 1, dtype=F32)[:, None, None]
    apow = jnp.exp((lam * dt)[None] * n)
    b = lax.complex(lp['s5_b_re'][d], lp['s5_b_im'][d])
    bbar = ((apow[1] - 1.0) / lam)[..., None] * b
    cmat = lax.complex(lp['s5_c_re'][d], lp['s5_c_im'][d])
    return apow, bbar, cmat


def _block_diag_kernel(x_ref, o_ref, *, rb, cb):
    n_g, rc, cc = x_ref.shape
    out_c = o_ref.shape[2]
    col = lax.broadcasted_iota(jnp.int32, (cc, out_c), 1)
    src = lax.broadcasted_iota(jnp.int32, (cc, out_c), 0)
    for g in range(n_g):
        dest = (src // cb) * (n_g * cb) + g * cb + src % cb
        place = jnp.where(col == dest, 1.0, 0.0).astype(BF16)
        wide = jnp.dot(x_ref[g], place, preferred_element_type=F32).astype(BF16)
        for r in range(rc // rb):
            r0 = (r * n_g + g) * rb
            o_ref[0, r0:r0 + rb, :] = wide[r * rb:(r + 1) * rb, :]


def s5_block_diag(x, rb, cb):
    g_n, rc, cc = x.shape
    out_r, out_c = rc * S5_OCT, cc * S5_OCT
    return pl.pallas_call(
        functools.partial(_block_diag_kernel, rb=rb, cb=cb),
        grid=(g_n // S5_OCT,),
        in_specs=[pl.BlockSpec((S5_OCT, rc, cc), lambda o: (o, 0, 0))],
        out_specs=pl.BlockSpec((1, out_r, out_c), lambda o: (o, 0, 0)),
        out_shape=jax.ShapeDtypeStruct((g_n // S5_OCT, out_r, out_c), BF16),
        compiler_params=_cparams("parallel"),
        name="s5_block_diag",
    )(x)


def s5_prepare(lp):
    q = S5_CHUNK
    g_n, p_n = S5_GROUPS, S5_STATE
    hp = lax.Precision.HIGHEST
    ap_f, bb_f, c_f = _s5_powers(lp, 0)
    ap_b, bb_b, c_b = _s5_powers(lp, 1)
    k_f = jnp.einsum('gop,tgp,gpi->gtio', c_f, ap_f[:q], bb_f, precision=hp).real
    k_b = jnp.einsum('gop,tgp,gpi->gtio', c_b, ap_b[:q], bb_b, precision=hp).real
    zero = jnp.zeros_like(k_f)
    fwd_ext = jnp.concatenate([zero, k_f], axis=1)
    bwd_ext = jnp.concatenate([k_b[:, ::-1], zero], axis=1)
    m_f = jnp.stack([fwd_ext[:, q - sp:2 * q - sp] for sp in range(q)], axis=1)
    m_b = jnp.stack([bwd_ext[:, q - 1 - sp:2 * q - 1 - sp] for sp in range(q)], axis=1)
    m_intra = (m_f + m_b).transpose(0, 1, 3, 2, 4).reshape(g_n, S5_CK, S5_CK)

    ws_f = ap_f[q - 1 - jnp.arange(q)][:, :, :, None] * bb_f[None]
    ws_b = ap_b[jnp.arange(q)][:, :, :, None] * bb_b[None]
    ws = jnp.stack([ws_f.real, ws_f.imag, ws_b.real, ws_b.imag], axis=0)
    ws = ws.transpose(2, 1, 4, 0, 3).reshape(g_n, S5_CK, 4, p_n)

    ca_f = c_f[None] * ap_f[1:q + 1][:, :, None, :]
    ca_b = c_b[None] * ap_b[q - jnp.arange(q)][:, :, None, :]
    wo = jnp.stack([ca_f.real, -ca_f.imag, ca_b.real, -ca_b.imag], axis=0)
    wo = wo.transpose(2, 0, 4, 1, 3).reshape(g_n, 4, p_n, S5_CK)

    h_n = S5_GROUP
    m_oct = s5_block_diag(m_intra.astype(BF16), h_n, h_n)
    w_state = s5_block_diag(ws.reshape(g_n, S5_CK, 4 * p_n).astype(BF16), h_n, p_n)
    w_out = s5_block_diag(wo.reshape(g_n, 4 * p_n, S5_CK).astype(BF16), p_n, h_n)

    def flat(z):
        return jnp.stack([z.real.reshape(-1), z.imag.reshape(-1)], axis=0)

    return m_oct, w_state, w_out, flat(ap_f[q]), flat(ap_b[q])


def _s5_chunk_rows(u_ref, n_chunks):
    steps = [u_ref[0, pl.ds(s, n_chunks, stride=S5_CHUNK), :].astype(BF16) for s in range(S5_CHUNK)]
    return jnp.concatenate(steps, axis=1)


def _s5_states_kernel(u_ref, w_ref, fre_ref, fim_ref, bre_ref, bim_ref):
    n_chunks, n = fre_ref.shape[1:]
    o = jnp.dot(_s5_chunk_rows(u_ref, n_chunks), w_ref[0], preferred_element_type=F32)
    for comp, ref in enumerate((fre_ref, fim_ref, bre_ref, bim_ref)):
        ref[0] = o[:, comp * n:(comp + 1) * n]


def s5_states(u_all, w_state):
    b, t, w = u_all.shape
    nc = t // S5_CHUNK
    pw = S5_OCT * S5_STATE
    out = jax.ShapeDtypeStruct((b, nc, S5_GROUPS * S5_STATE), F32)
    ospec = pl.BlockSpec((1, nc, pw), lambda o, i: (i, 0, o))
    return pl.pallas_call(
        _s5_states_kernel,
        grid=(S5_N_OCT, b),
        in_specs=[pl.BlockSpec((1, t, S5_OCT_W), lambda o, i: (i, 0, o)),
                  pl.BlockSpec((1, S5_ROW, 4 * pw), lambda o, i: (o, 0, 0))],
        out_specs=[ospec] * 4,
        out_shape=[out] * 4,
        compiler_params=_cparams("parallel", "parallel"),
        name="s5_states",
    )(u_all, w_state)


def _s5_scan_kernel(sre_ref, sim_ref, a_ref, hre_ref, him_ref, *, n_ctx_tiles, reverse):
    tile = 8
    n_tiles = sre_ref.shape[1] // tile
    ar = a_ref[0:1, :]
    ai = a_ref[1:2, :]
    zero = jnp.zeros((1, sre_ref.shape[2]), F32)

    def body(n, carry):
        hre, him = carry
        if reverse:
            t = jnp.where(n < n_ctx_tiles, n_ctx_tiles - 1 - n, n_tiles + n_ctx_tiles - 1 - n)
        else:
            t = n
        r0 = pl.multiple_of(t * tile, tile)
        sre = sre_ref[0, pl.ds(r0, tile), :]
        sim = sim_ref[0, pl.ds(r0, tile), :]
        in_re = [None] * tile
        in_im = [None] * tile
        for r in (range(tile - 1, -1, -1) if reverse else range(tile)):
            in_re[r], in_im[r] = hre, him
            hre, him = ar * hre - ai * him + sre[r:r + 1], ar * him + ai * hre + sim[r:r + 1]
        hre_ref[0, pl.ds(r0, tile), :] = jnp.concatenate(in_re, axis=0)
        him_ref[0, pl.ds(r0, tile), :] = jnp.concatenate(in_im, axis=0)
        return hre, him

    lax.fori_loop(0, n_tiles, body, (zero, zero))


def s5_scan(sre, sim, a_chunk, n_ctx_chunks, reverse):
    b, nc, w = sre.shape
    tl = 512
    spec = pl.BlockSpec((1, nc, tl), lambda i, j: (i, 0, j))
    out = jax.ShapeDtypeStruct((b, nc, w), F32)
    return pl.pallas_call(
        functools.partial(_s5_scan_kernel, n_ctx_tiles=n_ctx_chunks // 8, reverse=reverse),
        grid=(b, w // tl),
        in_specs=[spec, spec, pl.BlockSpec((2, tl), lambda i, j: (0, j))],
        out_specs=[spec, spec],
        out_shape=[out, out],
        compiler_params=_cparams("parallel", "parallel"),
        name="s5_scan_bwd" if reverse else "s5_scan_fwd",
    )(sre, sim, a_chunk)


def _s5_outputs_kernel(u_ref, m_ref, fre_ref, fim_ref, bre_ref, bim_ref, wo_ref, y_ref):
    n_chunks = fre_ref.shape[1]
    carried = jnp.concatenate([r[0].astype(BF16) for r in (fre_ref, fim_ref, bre_ref, bim_ref)], axis=1)
    y = (jnp.dot(_s5_chunk_rows(u_ref, n_chunks), m_ref[0], preferred_element_type=F32)
         + jnp.dot(carried, wo_ref[0], preferred_element_type=F32))
    for s in range(S5_CHUNK):
        y_ref[0, pl.ds(s, n_chunks, stride=S5_CHUNK), :] = y[:, s * S5_OCT_W:(s + 1) * S5_OCT_W]


def s5_outputs(u_all, m_oct, carried, w_out):
    b, t, w = u_all.shape
    nc = t // S5_CHUNK
    pw = S5_OCT * S5_STATE
    hspec = pl.BlockSpec((1, nc, pw), lambda o, i: (i, 0, o))
    lane_blk = pl.BlockSpec((1, t, S5_OCT_W), lambda o, i: (i, 0, o))
    mat = pl.BlockSpec((1, S5_ROW, S5_ROW), lambda o, i: (o, 0, 0))
    return pl.pallas_call(
        _s5_outputs_kernel,
        grid=(S5_N_OCT, b),
        in_specs=[lane_blk, mat, hspec, hspec, hspec, hspec, mat],
        out_specs=lane_blk,
        out_shape=jax.ShapeDtypeStruct((b, t, w), F32),
        compiler_params=_cparams("parallel", "parallel"),
        name="s5_outputs",
    )(u_all, m_oct, *carried, w_out)


def _s5_readout_kernel(y_ref, u_ref, d_ref, w_ref, b_ref, o_ref):
    y = _gelu_tanh(y_ref[...] + d_ref[...] * u_ref[...])
    z = jnp.dot(y.astype(BF16), w_ref[...], preferred_element_type=F32) + b_ref[...]
    o_ref[...] = (y * jax.nn.sigmoid(z)).astype(o_ref.dtype)


def s5_readout(y, u, d, glu_w, glu_b):
    m, w = y.shape
    tm = min(m, 512)
    row = pl.BlockSpec((tm, w), lambda i: (i, 0))
    vec = pl.BlockSpec((1, w), lambda i: (0, 0))
    return pl.pallas_call(
        _s5_readout_kernel,
        grid=(m // tm,),
        in_specs=[row, row, vec, pl.BlockSpec((w, w), lambda i: (0, 0)), vec],
        out_specs=row,
        out_shape=jax.ShapeDtypeStruct((m, w), BF16),
        compiler_params=_cparams("parallel"),
        name="s5_readout",
    )(y, u, d.reshape(1, w), glu_w, glu_b.reshape(1, w))


def s5_mixer(u, uc, lp, glu_w):
    b, l, w = u.shape
    c = uc.shape[1]
    m_oct, w_state, w_out, a_f, a_b = s5_prepare(lp)
    u_all = jnp.concatenate([uc, u], axis=1)
    fre, fim, bre, bim = s5_states(u_all, w_state)
    hf = s5_scan(fre, fim, a_f, c // S5_CHUNK, False)
    hb = s5_scan(bre, bim, a_b, c // S5_CHUNK, True)
    y_all = s5_outputs(u_all, m_oct, (*hf, *hb), w_out)
    out = s5_readout(y_all.reshape(b * (c + l), w), u_all.reshape(b * (c + l), w), lp['s5_d'], glu_w,
                     lp['s5_glu_b'])
    out = out.reshape(b, c + l, w)
    return out[:, c:], out[:, :c]


def kernel(x, c, ctx, c_ctx, mod_w, mod_b, norm1_g, norm2_g, in_w, gate_b, attn_sink, hy_short_w, hy_short_b,
           hy_w1, hy_b1, hy_freq1, hy_w2, hy_b2, hy_freq2, hy_w3, hy_bias, s5_a_re, s5_a_im, s5_log_dt,
           s5_b_re, s5_b_im, s5_c_re, s5_c_im, s5_d, s5_glu_w, s5_glu_b, br_attn_w, br_hyena_w, br_s5_w,
           out_w, peer_wq, peer_keys, peer_u, peer_v, final_g):
    B, L, D = x.shape
    C = ctx.shape[1]
    depth = mod_w.shape[0]
    rope_cos, rope_sin = rope_tables(L)
    conv_tables = dft_tables(L) if L > HY_DENSE_MAX_LEN else None

    cvec = jnp.zeros((8, D), F32).at[:B].set(c).at[B].set(c_ctx)
    mod_all = modulation(cvec, mod_w, mod_b)

    xc = ctx
    for layer in range(depth):
        need_ctx = layer < depth - 1
        lp = dict(attn_sink=attn_sink[layer],
                  hy_short_w=hy_short_w[layer], hy_short_b=hy_short_b[layer],
                  hy_w1=hy_w1[layer], hy_b1=hy_b1[layer], hy_freq1=hy_freq1[layer],
                  hy_w2=hy_w2[layer], hy_b2=hy_b2[layer], hy_freq2=hy_freq2[layer],
                  hy_w3=hy_w3[layer], hy_bias=hy_bias[layer],
                  s5_a_re=s5_a_re[layer], s5_a_im=s5_a_im[layer], s5_log_dt=s5_log_dt[layer],
                  s5_b_re=s5_b_re[layer], s5_b_im=s5_b_im[layer],
                  s5_c_re=s5_c_re[layer], s5_c_im=s5_c_im[layer], s5_d=s5_d[layer],
                  s5_glu_w=s5_glu_w[layer], s5_glu_b=s5_glu_b[layer])
        mod = mod_all[layer]
        ml = [mod[:B, i * D:(i + 1) * D] for i in range(N_MOD)]
        mc = [mod[B:B + 1, i * D:(i + 1) * D] for i in range(N_MOD)]

        w_in = in_w[layer].astype(BF16)
        w_parts = [w_in[:, COL_Q:COL_S5], w_in[:, COL_S5:COL_HY], w_in[:, COL_HY:COL_GATE], w_in[:, COL_GATE:]]
        wa = br_attn_w[layer].astype(BF16)
        wh = br_hyena_w[layer].astype(BF16)
        ws = br_s5_w[layer].astype(BF16)
        wo = out_w[layer].astype(BF16)
        wqt = peer_wq[layer].T.astype(BF16)
        keys = peer_keys[layer].reshape(2 * PEER_HEADS, PEER_KEYS, PEER_HALF).astype(BF16)
        u_tab = peer_u[layer].astype(BF16)
        vt_tab = peer_v[layer].T.astype(BF16)

        h = norm_modulate(x, norm1_g[layer], ml[0], ml[1])
        hc = norm_modulate(xc, norm1_g[layer], mc[0], mc[1])
        qkv, u_s5, z_hy, gates = in_projection(h.reshape(B * L, D), w_parts, gate_b[layer])
        qkv_c, uc_s5, zc_hy, gates_c = in_projection(hc.reshape(B * C, D), w_parts, gate_b[layer])
        qkv = qkv.reshape(B, L, -1)
        qkv_c = qkv_c.reshape(B, C, -1)
        u_s5 = u_s5.reshape(B, L, -1)
        uc_s5 = uc_s5.reshape(B, C, -1)

        q, k, v = qkv_heads(qkv, rope_cos, rope_sin, True)
        qc, kc, vc = qkv_heads(qkv_c, rope_cos, rope_sin, False)
        att = attention(q, k, v, kc, vc, lp['attn_sink'], True)
        hy = hyena_mixer(z_hy.reshape(B, L, -1), lp, conv_tables)
        s5, s5c = s5_mixer(u_s5, uc_s5, lp, s5_glu_w[layer].astype(BF16))

        x = branch_merge(att, hy, s5, gates.reshape(B, L, -1), wa, wh, ws, wo, x, ml[2])
        x = peer_block(x, norm2_g[layer], ml[3], ml[4], ml[5], wqt, keys, u_tab, vt_tab)

        if need_ctx:
            att_c = attention(qc, None, None, kc, vc, lp['attn_sink'], False)
            hy_c = hyena_mixer(zc_hy.reshape(B, C, -1), lp)
            xc = branch_merge(att_c, hy_c, s5c, gates_c.reshape(B, C, -1), wa, wh, ws, wo, xc, mc[2])
            xc = peer_block(xc, norm2_g[layer], mc[3], mc[4], mc[5], wqt, keys, u_tab, vt_tab)
    return final_norm(x, final_g)
```

```python
import functools
import math

import jax
import jax.numpy as jnp
import numpy as np
from jax import lax
from jax.experimental import pallas as pl
from jax.experimental.pallas import tpu as pltpu

F32 = jnp.float32
BF16 = jnp.bfloat16

NORM_EPS = 1e-6
NEG_INF = -1e30
N_MOD = 6

GRID_W = 64
ATT_HEADS = 8
ATT_KV_HEADS = 2
ATT_GROUP = ATT_HEADS // ATT_KV_HEADS
HEAD_DIM = 64
ATT_WIDTH = ATT_HEADS * HEAD_DIM
KV_WIDTH = ATT_KV_HEADS * HEAD_DIM
WINDOW = 128
ATT_BLOCK = 128
ROPE_BASE = 10000.0
ROPE_PAIRS_PER_AXIS = HEAD_DIM // 4

HY_WIDTH = 512
HY_N_PROJ = 3
HY_SHORT = 3
HY_BANDS = 16
HY_DECAY_TARGET = 1e-2
HY_FAST_RATE = -math.log(HY_DECAY_TARGET) / 0.3
HY_SLOW_RATE = -math.log(HY_DECAY_TARGET) / 1.5

S5_WIDTH = 512
S5_GROUP = 16
S5_GROUPS = S5_WIDTH // S5_GROUP
S5_STATE = 64

N_BRANCH = 3

PEER_HEADS = 8
PEER_KEYS = 128
PEER_HALF = 128
PEER_TOPK = 16

COL_Q = 0
COL_K = COL_Q + ATT_WIDTH
COL_V = COL_K + KV_WIDTH
COL_S5 = COL_V + KV_WIDTH
COL_HY = COL_S5 + S5_WIDTH
COL_GATE = COL_HY + HY_N_PROJ * HY_WIDTH

VMEM_LIMIT_V7X = 56 * 1024 * 1024


def _cparams(*sem):
    return pltpu.CompilerParams(dimension_semantics=sem, vmem_limit_bytes=VMEM_LIMIT_V7X)


def _gelu_tanh(x):
    return 0.5 * x * (1.0 + jnp.tanh(math.sqrt(2.0 / math.pi) * (x + 0.044715 * (x * x * x))))


def _mod_kernel(s_ref, w_ref, b_ref, o_ref):
    s = s_ref[...]
    s = (s * jax.nn.sigmoid(s)).astype(BF16)
    o_ref[0] = jnp.dot(s, w_ref[0].astype(BF16), preferred_element_type=F32) + b_ref[0]


def modulation(cvec, mod_w, mod_b):
    depth, d, n = mod_w.shape
    tn = 1024
    return pl.pallas_call(
        _mod_kernel,
        grid=(depth, n // tn),
        in_specs=[pl.BlockSpec((8, d), lambda l, j: (0, 0)),
                  pl.BlockSpec((1, d, tn), lambda l, j: (l, 0, j)),
                  pl.BlockSpec((1, 1, tn), lambda l, j: (l, 0, j))],
        out_specs=pl.BlockSpec((1, 8, tn), lambda l, j: (l, 0, j)),
        out_shape=jax.ShapeDtypeStruct((depth, 8, n), F32),
        compiler_params=_cparams("parallel", "parallel"),
        name="adaln_mod",
    )(cvec, mod_w, mod_b.reshape(depth, 1, n))


def _norm_mod_kernel(x_ref, g_ref, sh_ref, sc_ref, o_ref, *, transposed):
    x = x_ref[0]
    y = x * lax.rsqrt(jnp.mean(x * x, axis=-1, keepdims=True) + NORM_EPS)
    y = y * g_ref[...]
    y = y * (1.0 + sc_ref[0]) + sh_ref[0]
    if transposed:
        o_ref[...] = y.T.astype(o_ref.dtype)
    else:
        o_ref[0] = y.astype(o_ref.dtype)


def norm_modulate(x, g, shift, scale, transposed=False):
    b, l, d = x.shape
    tl = min(l, 512)
    nt = l // tl
    per_batch = shift.shape[0] == b and b > 1
    mod_map = (lambda i, j: (i, 0, 0)) if per_batch else (lambda i, j: (0, 0, 0))
    if transposed:
        out_spec = pl.BlockSpec((d, tl), lambda i, j: (0, i * nt + j))
        out_shape = jax.ShapeDtypeStruct((d, b * l), BF16)
    else:
        out_spec = pl.BlockSpec((1, tl, d), lambda i, j: (i, j, 0))
        out_shape = jax.ShapeDtypeStruct((b, l, d), BF16)
    return pl.pallas_call(
        functools.partial(_norm_mod_kernel, transposed=transposed),
        grid=(b, nt),
        in_specs=[pl.BlockSpec((1, tl, d), lambda i, j: (i, j, 0)),
                  pl.BlockSpec((1, d), lambda i, j: (0, 0)),
                  pl.BlockSpec((1, 1, d), mod_map),
                  pl.BlockSpec((1, 1, d), mod_map)],
        out_specs=out_spec,
        out_shape=out_shape,
        compiler_params=_cparams("parallel", "parallel"),
        name="norm_modulate_t" if transposed else "norm_modulate",
    )(x, g.reshape(1, d), shift.reshape(-1, 1, d), scale.reshape(-1, 1, d))


def _final_norm_kernel(x_ref, g_ref, o_ref):
    x = x_ref[...]
    y = x * lax.rsqrt(jnp.mean(x * x, axis=-1, keepdims=True) + NORM_EPS)
    o_ref[...] = y * g_ref[...]


def final_norm(x, g):
    b, l, d = x.shape
    x2 = x.reshape(b * l, d)
    tm = 512
    out = pl.pallas_call(
        _final_norm_kernel,
        grid=(b * l // tm,),
        in_specs=[pl.BlockSpec((tm, d), lambda i: (i, 0)), pl.BlockSpec((1, d), lambda i: (0, 0))],
        out_specs=pl.BlockSpec((tm, d), lambda i: (i, 0)),
        out_shape=jax.ShapeDtypeStruct((b * l, d), F32),
        compiler_params=_cparams("parallel"),
        name="final_norm",
    )(x2, g.reshape(1, d))
    return out.reshape(b, l, d)


def _inproj_kernel(h_ref, wqkv_ref, ws5_ref, why_ref, wg_ref, gb_ref, oqkv_ref, os5_ref, ohy_ref, og_ref):
    h = h_ref[...]
    oqkv_ref[...] = jnp.dot(h, wqkv_ref[...], preferred_element_type=F32)
    os5_ref[...] = jnp.dot(h, ws5_ref[...], preferred_element_type=F32)
    ohy_ref[...] = jnp.dot(h, why_ref[...], preferred_element_type=F32)
    og_ref[...] = (jnp.dot(h, wg_ref[...], preferred_element_type=F32) + gb_ref[...]).astype(og_ref.dtype)


def in_projection(h, w_parts, gate_b):
    m, d = h.shape
    tm = 256
    widths = [w.shape[1] for w in w_parts]
    w_specs = [pl.BlockSpec((d, n), lambda i: (0, 0)) for n in widths]
    return pl.pallas_call(
        _inproj_kernel,
        grid=(m // tm,),
        in_specs=[pl.BlockSpec((tm, d), lambda i: (i, 0))] + w_specs
                 + [pl.BlockSpec((1, widths[3]), lambda i: (0, 0))],
        out_specs=[pl.BlockSpec((tm, n), lambda i: (i, 0)) for n in widths],
        out_shape=[jax.ShapeDtypeStruct((m, n), F32) for n in widths[:3]]
                  + [jax.ShapeDtypeStruct((m, widths[3]), BF16)],
        compiler_params=_cparams("parallel"),
        name="in_projection",
    )(h, *w_parts, gate_b.reshape(1, -1))


def _merge_kernel(att_ref, hy_ref, s5_ref, gl_ref, wa_ref, wh_ref, ws_ref, wo_ref, x_ref, g1_ref, o_ref):
    d = wo_ref.shape[0]
    g = jax.nn.sigmoid(gl_ref[0].astype(F32))
    m = g[:, :d] * jnp.dot(att_ref[0], wa_ref[...], preferred_element_type=F32)
    m = m + g[:, d:2 * d] * jnp.dot(hy_ref[0], wh_ref[...], preferred_element_type=F32)
    m = m + g[:, 2 * d:] * jnp.dot(s5_ref[0], ws_ref[...], preferred_element_type=F32)
    y = jnp.dot(m.astype(BF16), wo_ref[...], preferred_element_type=F32)
    o_ref[0] = x_ref[0] + g1_ref[0] * y


def branch_merge(att, hy, s5, gate_logits, wa, wh, ws, wo, x, g1):
    b, l, d = x.shape
    tl = min(l, 256)
    wd = att.shape[-1]
    per_batch = g1.shape[0] == b and b > 1
    g_map = (lambda i, j: (i, 0, 0)) if per_batch else (lambda i, j: (0, 0, 0))
    row = lambda n: pl.BlockSpec((1, tl, n), lambda i, j: (i, j, 0))
    full = lambda a: pl.BlockSpec(a.shape, lambda i, j: (0, 0))
    return pl.pallas_call(
        _merge_kernel,
        grid=(b, l // tl),
        in_specs=[row(wd), row(wd), row(wd), row(N_BRANCH * d), full(wa), full(wh), full(ws), full(wo),
                  row(d), pl.BlockSpec((1, 1, d), g_map)],
        out_specs=row(d),
        out_shape=jax.ShapeDtypeStruct((b, l, d), F32),
        compiler_params=_cparams("parallel", "parallel"),
        name="branch_merge",
    )(att, hy, s5, gate_logits, wa, wh, ws, wo, x, g1.reshape(-1, 1, d))


def _topk_rows(s, k):
    rows = []
    for _ in range(k):
        m = jnp.max(s, axis=0, keepdims=True)
        rows.append(m)
        s = jnp.where(s >= m, NEG_INF, s)
    return rows


LANES = 128


def _peer_select_kernel(ht_ref, wqt_ref, keys_ref, rank_ref, cnt_ref, e1_ref, e2_ref, qt_scr, cand_scr):
    nh = rank_ref.shape[0]
    nk = PEER_KEYS
    pairs = [(p, q) for p in range(PEER_TOPK) for q in range(PEER_TOPK) if (p + 1) * (q + 1) <= PEER_TOPK]

    def project(hh):
        r0 = pl.multiple_of(hh * 2 * nk, 2 * nk)
        qt_scr[hh % 2] = jnp.dot(wqt_ref[pl.ds(r0, 2 * nk), :], ht_ref[...],
                                 preferred_element_type=F32).astype(BF16)

    project(0)

    def head(hh, carry):
        qt = qt_scr[hh % 2]
        s1 = jnp.dot(keys_ref[2 * hh], qt[:nk], preferred_element_type=F32)
        s2 = jnp.dot(keys_ref[2 * hh + 1], qt[nk:], preferred_element_type=F32)
        project(jnp.minimum(hh + 1, nh - 1))
        for tb in range(s1.shape[1] // LANES):
            lanes = slice(tb * LANES, (tb + 1) * LANES)
            select_block(hh, tb, lanes, s1[:, lanes], s2[:, lanes])
        return carry

    def select_block(hh, tb, lanes, s1, s2):
        a = _topk_rows(s1, PEER_TOPK)
        b = []
        rest = s2
        rank = jnp.full(s2.shape, float(PEER_TOPK), F32)
        for r in range(PEER_TOPK):
            m = jnp.max(rest, axis=0, keepdims=True)
            b.append(m)
            hit = rest >= m
            rank = jnp.where(hit, float(r), rank)
            rest = jnp.where(hit, NEG_INF, rest)
        cand_scr[:, lanes] = jnp.full((cand_scr.shape[0], LANES), NEG_INF, F32)
        for r, (p, q) in enumerate(pairs):
            cand_scr[r:r + 1, lanes] = a[p] + b[q]
        cand = cand_scr[:, lanes]
        best = _topk_rows(cand, PEER_TOPK)
        top = a[0] + b[0]
        z = jnp.exp(best[0] - top)
        for r in range(1, PEER_TOPK):
            z = z + jnp.exp(best[r] - top)
        tau = best[PEER_TOPK - 1]
        reached = jnp.where(cand >= tau, 1.0, 0.0)
        cnt = jnp.zeros(s1.shape, F32)
        r = 0
        for p in range(PEER_TOPK):
            n_q = sum(1 for pp, _ in pairs if pp == p)
            cnt_p = reached[r:r + 1]
            for rr in range(r + 1, r + n_q):
                cnt_p = cnt_p + reached[rr:rr + 1]
            r += n_q
            cnt = jnp.where(s1 == a[p], cnt_p, cnt)
        rank_ref[hh, :, lanes] = rank.astype(BF16)
        cnt_ref[hh, tb] = cnt
        e1_ref[hh, tb] = 0.5 * jnp.exp(s1 - a[0])
        e2_ref[hh, :, lanes] = (jnp.exp(s2 - b[0]) / z).astype(BF16)

    lax.fori_loop(0, nh, head, 0)


def peer_select(ht, wqt, keys, tt):
    d, m = ht.shape
    nh = keys.shape[0] // 2
    out = jax.ShapeDtypeStruct((nh, PEER_KEYS, m), BF16)
    out32 = jax.ShapeDtypeStruct((nh, m // LANES, PEER_KEYS, LANES), F32)
    ospec = pl.BlockSpec((nh, PEER_KEYS, tt), lambda i: (0, 0, i))
    ospec32 = pl.BlockSpec((nh, tt // LANES, PEER_KEYS, LANES), lambda i: (0, i, 0, 0))
    return pl.pallas_call(
        _peer_select_kernel,
        grid=(m // tt,),
        in_specs=[pl.BlockSpec((d, tt), lambda i: (0, i)),
                  pl.BlockSpec(wqt.shape, lambda i: (0, 0)),
                  pl.BlockSpec(keys.shape, lambda i: (0, 0, 0))],
        out_specs=[ospec, ospec32, ospec32, ospec],
        out_shape=[out, out32, out32, out],
        scratch_shapes=[pltpu.VMEM((2, 2 * PEER_KEYS, tt), BF16), pltpu.VMEM((56, tt), F32)],
        compiler_params=_cparams("parallel"),
        name="peer_select",
    )(ht, wqt, keys)


PEER_PAIR = 2 * PEER_KEYS
PEER_LOOKAHEAD = 2


def _half_gate_gelu(half_w, x):
    c = math.sqrt(2.0 / math.pi)
    p = half_w * x
    return p + p * jnp.tanh(x * (c + (c * 0.044715) * (x * x)))


def _key_row(ref, hh, ii):
    blocks = [jnp.broadcast_to(ref[hh, tb, ii:ii + 1, :], (PEER_KEYS, LANES)) for tb in range(ref.shape[1])]
    return jnp.concatenate(blocks, axis=1).astype(BF16)


def _peer_expert_kernel(ht_ref, u_ref, vt_ref, rank_ref, cnt_ref, e1_ref, e2_ref, x_ref, g2_ref, o_ref,
                        acc_scr, wa_scr):
    k = pl.program_id(1)
    nh = rank_ref.shape[0]

    @pl.when(k == 0)
    def _():
        acc_scr[...] = jnp.zeros_like(acc_scr)

    ht = ht_ref[...]
    n_pairs = u_ref.shape[0] // PEER_PAIR

    def cols(p):
        return slice(p * PEER_PAIR, (p + 1) * PEER_PAIR)

    def first_matmul(p):
        return jnp.dot(u_ref[cols(p), :], ht, preferred_element_type=F32)

    def gated(p, act):
        w_rows = []
        for r in range(2):
            ii = 2 * p + r
            w = None
            for hh in range(nh):
                selected = rank_ref[hh] < _key_row(cnt_ref, hh, ii)
                gate = jnp.where(selected, e2_ref[hh], jnp.zeros((), BF16)) * _key_row(e1_ref, hh, ii)
                w = gate if w is None else w + gate
            w_rows.append(w)
        half_w = jnp.concatenate(w_rows, axis=0)
        return _half_gate_gelu(half_w, act.astype(BF16))

    acts = {q: first_matmul(q) for q in range(min(PEER_LOOKAHEAD, n_pairs))}
    part = None
    for p in range(n_pairs + 1):
        if p + PEER_LOOKAHEAD < n_pairs:
            acts[p + PEER_LOOKAHEAD] = first_matmul(p + PEER_LOOKAHEAD)
        if p < n_pairs:
            wa_scr[p] = gated(p, acts.pop(p))
        if p >= 1:
            d = jnp.dot(vt_ref[:, cols(p - 1)], wa_scr[p - 1], preferred_element_type=F32)
            part = d if part is None else part + d
    acc_scr[...] += part

    @pl.when(k == pl.num_programs(1) - 1)
    def _():
        o_ref[...] = x_ref[...] + g2_ref[0] * acc_scr[...].T


def peer_experts(ht, u_tab, vt_tab, sel, x, g2, tokens_per_batch, tt, ec):
    d, m = ht.shape
    e = u_tab.shape[0]
    nh = sel[0].shape[0]
    rows_i = ec // PEER_KEYS
    full_keys = pl.BlockSpec((nh, PEER_KEYS, tt), lambda i, k: (0, 0, i))
    chunk_keys = pl.BlockSpec((nh, tt // LANES, rows_i, LANES), lambda i, k: (0, i, k, 0))
    tiles_per_batch = tokens_per_batch // tt
    if g2.shape[0] > 1:
        g_map = lambda i, k: (i // tiles_per_batch, 0, 0)
    else:
        g_map = lambda i, k: (0, 0, 0)
    rank, cnt, e1, e2 = sel
    return pl.pallas_call(
        _peer_expert_kernel,
        grid=(m // tt, e // ec),
        in_specs=[pl.BlockSpec((d, tt), lambda i, k: (0, i)),
                  pl.BlockSpec((ec, d), lambda i, k: (k, 0)),
                  pl.BlockSpec((d, ec), lambda i, k: (0, k)),
                  full_keys, chunk_keys, chunk_keys, full_keys,
                  pl.BlockSpec((tt, d), lambda i, k: (i, 0)),
                  pl.BlockSpec((1, 1, d), g_map)],
        out_specs=pl.BlockSpec((tt, d), lambda i, k: (i, 0)),
        out_shape=jax.ShapeDtypeStruct((m, d), F32),
        scratch_shapes=[pltpu.VMEM((d, tt), F32), pltpu.VMEM((ec // PEER_PAIR, PEER_PAIR, tt), BF16)],
        compiler_params=_cparams("parallel", "arbitrary"),
        name="peer_experts",
    )(ht, u_tab, vt_tab, rank, cnt, e1, e2, x, g2.reshape(-1, 1, d))


def peer_block(x, g_norm, shift, scale, gate, wqt, keys, u_tab, vt_tab):
    b, l, d = x.shape
    ht = norm_modulate(x, g_norm, shift, scale, transposed=True)
    tt = min(l, 512)
    sel = peer_select(ht, wqt, keys, tt)
    out = peer_experts(ht, u_tab, vt_tab, sel, x.reshape(b * l, d), gate, l, tt, 2048)
    return out.reshape(b, l, d)


def rope_tables(l):
    rows_n = l // GRID_W
    row = jnp.repeat(jnp.arange(rows_n), GRID_W).astype(F32)
    col = jnp.tile(jnp.arange(GRID_W), rows_n).astype(F32)
    inv = jnp.power(ROPE_BASE, -jnp.arange(ROPE_PAIRS_PER_AXIS, dtype=F32) / ROPE_PAIRS_PER_AXIS)
    ang = jnp.concatenate([row[:, None] * inv, col[:, None] * inv], axis=-1)
    c, s = jnp.cos(ang), jnp.sin(ang)
    return jnp.concatenate([c, c, c, c], axis=-1), jnp.concatenate([-s, s, -s, s], axis=-1)


def _rope_pair(x, cos2, sin2):
    half = HEAD_DIM // 2
    lane = lax.broadcasted_iota(jnp.int32, x.shape, 1)
    swapped = jnp.where(lane % HEAD_DIM < half, pltpu.roll(x, 2 * HEAD_DIM - half, axis=1),
                        pltpu.roll(x, half, axis=1))
    return x * cos2 + swapped * sin2


def _qkv_heads_kernel(qkv_ref, cos_ref, sin_ref, q_ref, k_ref, v_ref, *, rope):
    scale = HEAD_DIM ** -0.5
    x = qkv_ref[0]
    pair = 2 * HEAD_DIM
    for j in range((ATT_WIDTH + KV_WIDTH) // pair):
        blk = x[:, j * pair:(j + 1) * pair]
        if rope:
            blk = _rope_pair(blk, cos_ref[...], sin_ref[...])
        for t in range(2):
            head = blk[:, t * HEAD_DIM:(t + 1) * HEAD_DIM]
            h = 2 * j + t
            if h < ATT_HEADS:
                q_ref[0, h] = (head * scale).astype(q_ref.dtype)
            else:
                k_ref[0, h - ATT_HEADS] = head.astype(k_ref.dtype)
    for t in range(ATT_KV_HEADS):
        v_ref[0, t] = x[:, COL_V + t * HEAD_DIM:COL_V + (t + 1) * HEAD_DIM].astype(v_ref.dtype)


def qkv_heads(qkv, cos2, sin2, rope):
    b, l, w = qkv.shape
    tl = min(l, 256)
    tab = pl.BlockSpec((tl, 2 * HEAD_DIM), lambda i, j: (j, 0))
    hspec = lambda n: pl.BlockSpec((1, n, tl, HEAD_DIM), lambda i, j: (i, 0, j, 0))
    return pl.pallas_call(
        functools.partial(_qkv_heads_kernel, rope=rope),
        grid=(b, l // tl),
        in_specs=[pl.BlockSpec((1, tl, w), lambda i, j: (i, j, 0)), tab, tab],
        out_specs=[hspec(ATT_HEADS), hspec(ATT_KV_HEADS), hspec(ATT_KV_HEADS)],
        out_shape=[jax.ShapeDtypeStruct((b, ATT_HEADS, l, HEAD_DIM), BF16),
                   jax.ShapeDtypeStruct((b, ATT_KV_HEADS, l, HEAD_DIM), BF16),
                   jax.ShapeDtypeStruct((b, ATT_KV_HEADS, l, HEAD_DIM), BF16)],
        compiler_params=_cparams("parallel", "parallel"),
        name="qkv_heads",
    )(qkv, cos2, sin2)


def _attention_kernel(sink_ref, q_ref, kc_ref, vc_ref, *rest, local):
    if local:
        kp_ref, ko_ref, kn_ref, vp_ref, vo_ref, vn_ref, bias_ref, o_ref = rest
    else:
        (o_ref,) = rest
    blk = ATT_BLOCK
    rows = ATT_GROUP * blk
    outs = []
    for g in range(ATT_KV_HEADS):
        q = q_ref[0, g * ATT_GROUP:(g + 1) * ATT_GROUP].reshape(rows, HEAD_DIM)
        if local:
            keys = jnp.concatenate([kc_ref[0, g], kp_ref[0, g], ko_ref[0, g], kn_ref[0, g]], axis=0)
            vals = jnp.concatenate([vc_ref[0, g], vp_ref[0, g], vo_ref[0, g], vn_ref[0, g]], axis=0)
        else:
            keys, vals = kc_ref[0, g], vc_ref[0, g]
        s = lax.dot_general(q, keys, (((1,), (1,)), ((), ())), preferred_element_type=F32)
        r = lax.broadcasted_iota(jnp.int32, (rows, 1), 0)
        if local:
            s = s + jnp.concatenate([bias_ref[0]] * ATT_GROUP, axis=0)
        sink = jnp.zeros((rows, 1), F32)
        for t in range(ATT_GROUP):
            sink = jnp.where(r // blk == t, sink_ref[g * ATT_GROUP + t], sink)
        m = jnp.maximum(jnp.max(s, axis=-1, keepdims=True), sink)
        p = jnp.exp(s - m)
        denom = jnp.sum(p, axis=-1, keepdims=True) + jnp.exp(sink - m)
        o = jnp.dot(p.astype(BF16), vals, preferred_element_type=F32) / denom
        outs += [o[t * blk:(t + 1) * blk] for t in range(ATT_GROUP)]
    o_ref[0] = jnp.concatenate(outs, axis=1).astype(o_ref.dtype)


def attention(q, k, v, kc, vc, sink, local):
    b, _, l, d = q.shape
    c = kc.shape[2]
    nb = l // ATT_BLOCK
    qspec = pl.BlockSpec((1, ATT_HEADS, ATT_BLOCK, d), lambda i, j: (i, 0, j, 0))
    cspec = pl.BlockSpec((1, ATT_KV_HEADS, c, d), lambda i, j: (i, 0, 0, 0))
    in_specs = [pl.BlockSpec(memory_space=pltpu.SMEM), qspec, cspec, cspec]
    args = [sink, q, kc, vc]
    if local:
        kv = lambda f: pl.BlockSpec((1, ATT_KV_HEADS, ATT_BLOCK, d), lambda i, j: (i, 0, f(j), 0))
        band = [kv(lambda j: jnp.maximum(j - 1, 0)), kv(lambda j: j), kv(lambda j: jnp.minimum(j + 1, nb - 1))]
        in_specs += band + band
        args += [k, k, k, v, v, v]
        qpos = jnp.arange(ATT_BLOCK)[:, None]
        kpos = jnp.arange(-ATT_BLOCK, 2 * ATT_BLOCK)[None, :]
        near = jnp.abs(kpos - qpos) <= WINDOW
        variants = [near, near & (kpos >= 0), near & (kpos < ATT_BLOCK), near & (kpos >= 0) & (kpos < ATT_BLOCK)]
        bias = jnp.stack([jnp.concatenate([jnp.zeros((ATT_BLOCK, c), F32), jnp.where(ok, 0.0, NEG_INF)], axis=1)
                          for ok in variants])
        in_specs.append(pl.BlockSpec((1, ATT_BLOCK, c + 3 * ATT_BLOCK),
                                     lambda i, j: ((j == 0) + 2 * (j == nb - 1), 0, 0)))
        args.append(bias)
    return pl.pallas_call(
        functools.partial(_attention_kernel, local=local),
        grid=(b, nb),
        in_specs=in_specs,
        out_specs=pl.BlockSpec((1, ATT_BLOCK, ATT_HEADS * d), lambda i, j: (i, j, 0)),
        out_shape=jax.ShapeDtypeStruct((b, l, ATT_HEADS * d), BF16),
        compiler_params=_cparams("parallel", "parallel"),
        name="window_attention" if local else "context_attention",
    )(*args)


def hyena_filters(L, lp):
    t = jnp.arange(L, dtype=F32)
    tn = t / L
    bands = jnp.arange(1, HY_BANDS + 1, dtype=F32)
    ang = 2.0 * math.pi * tn[:, None] * bands[None, :]
    z = jnp.concatenate([tn[:, None], jnp.cos(ang), jnp.sin(ang)], axis=-1)
    hdn = jnp.sin(lp['hy_freq1'] * (z @ lp['hy_w1'] + lp['hy_b1']))
    hdn = jnp.sin(lp['hy_freq2'] * (hdn @ lp['hy_w2'] + lp['hy_b2']))
    filt = hdn @ lp['hy_w3']
    rate = jnp.linspace(HY_FAST_RATE, HY_SLOW_RATE, HY_WIDTH, dtype=F32)
    tw = jnp.linspace(0.0, 1.0, L, dtype=F32)
    window = jnp.exp(-tw[:, None] * rate[None, :])
    return filt[:, :HY_WIDTH] * window, filt[:, HY_WIDTH:] * window


def _short_conv_kernel(z_ref, zp_ref, zn_ref, w_ref, b_ref, x0_ref, vg_ref, vgb_ref):
    i = pl.program_id(1)
    z = z_ref[0]
    tl, w3 = z.shape
    prev_row = jnp.where(i > 0, zp_ref[0, 7:8, :], 0.0)
    next_row = jnp.where(i < pl.num_programs(1) - 1, zn_ref[0, 0:1, :], 0.0)
    row = lax.broadcasted_iota(jnp.int32, (tl, w3), 0)
    z_prev = jnp.where(row == 0, prev_row, pltpu.roll(z, 1, axis=0))
    z_next = jnp.where(row == tl - 1, next_row, pltpu.roll(z, tl - 1, axis=0))
    y = b_ref[...] + z_prev * w_ref[0:1, :] + z * w_ref[1:2, :] + z_next * w_ref[2:3, :]
    w = w3 // HY_N_PROJ
    x0_ref[0] = y[:, :w]
    vg = y[:, 2 * w:] * y[:, w:2 * w]
    vg_ref[0] = vg
    vgb_ref[0] = vg.astype(BF16)


def hyena_short_conv(z, w, b):
    bsz, l, w3 = z.shape
    tl = min(l, 256)
    nb8 = tl // 8
    last8 = l // 8 - 1
    out = jax.ShapeDtypeStruct((bsz, l, w3 // HY_N_PROJ), F32)
    ospec = pl.BlockSpec((1, tl, w3 // HY_N_PROJ), lambda i, j: (i, j, 0))
    return pl.pallas_call(
        _short_conv_kernel,
        grid=(bsz, l // tl),
        in_specs=[pl.BlockSpec((1, tl, w3), lambda i, j: (i, j, 0)),
                  pl.BlockSpec((1, 8, w3), lambda i, j: (i, jnp.maximum(j * nb8 - 1, 0), 0)),
                  pl.BlockSpec((1, 8, w3), lambda i, j: (i, jnp.minimum((j + 1) * nb8, last8), 0)),
                  pl.BlockSpec((HY_SHORT, w3), lambda i, j: (0, 0)),
                  pl.BlockSpec((1, w3), lambda i, j: (0, 0))],
        out_specs=[ospec, ospec, ospec],
        out_shape=[out, out, jax.ShapeDtypeStruct(out.shape, BF16)],
        compiler_params=_cparams("parallel", "parallel"),
        name="hyena_short_conv",
    )(z, z, z, w, b.reshape(1, w3))


def dft_tables(l):
    n = 2 * l
    split = 64
    t = jnp.arange(l, dtype=jnp.int32)[None, :]
    kh = jnp.arange(l // split, dtype=jnp.int32)[:, None]
    kl = jnp.arange(split, dtype=jnp.int32)[:, None]
    ang_h = (2.0 * math.pi / n) * (((kh * split) * t) % n).astype(F32)
    ang_l = (2.0 * math.pi / n) * ((kl * t) % n).astype(F32)
    ch, sh = jnp.cos(ang_h)[:, None, :], jnp.sin(ang_h)[:, None, :]
    cl, sl = jnp.cos(ang_l)[None, :, :], jnp.sin(ang_l)[None, :, :]
    cos_t = (ch * cl - sh * sl).reshape(l, l).astype(BF16)
    sin_t = (sh * cl + ch * sl).reshape(l, l).astype(BF16)
    return cos_t, sin_t


def _filter_spectrum_kernel(c_ref, s_ref, hp_ref, hm_ref, kre_ref, kim_ref):
    kre_ref[...] = jnp.dot(c_ref[...], hp_ref[...], preferred_element_type=F32)
    kim_ref[...] = -jnp.dot(s_ref[...], hm_ref[...], preferred_element_type=F32)


def filter_spectrum(cos_t, sin_t, h_fwd, h_bwd):
    l, w = h_fwd.shape
    hb0 = h_bwd.at[0].set(0.0)
    tk = 512
    tab = pl.BlockSpec((tk, l), lambda k: (k, 0))
    full = pl.BlockSpec((l, w), lambda k: (0, 0))
    ospec = pl.BlockSpec((tk, w), lambda k: (k, 0))
    out = jax.ShapeDtypeStruct((l, w), F32)
    kre, kim = pl.pallas_call(
        _filter_spectrum_kernel,
        grid=(l // tk,),
        in_specs=[tab, tab, full, full],
        out_specs=[ospec, ospec],
        out_shape=[out, out],
        compiler_params=_cparams("parallel"),
        name="hyena_filter_spectrum",
    )(cos_t, sin_t, (h_fwd + hb0).astype(BF16), (h_fwd - hb0).astype(BF16))
    sign = jnp.where(jnp.arange(l) % 2 == 0, 1.0, -1.0)[:, None]
    k_nyq = jnp.sum(sign * (h_fwd + hb0), axis=0, keepdims=True)
    return kre, kim, k_nyq


def _long_conv_kernel(x_ref, crow_ref, srow_ref, ccol_ref, scol_ref, kre_ref, kim_ref, knyq_ref, y_ref,
                      acc_scr, xn_scr):
    k = pl.program_id(1)
    nk = pl.num_programs(1)
    x = x_ref[0]
    l = x.shape[0]
    tk = crow_ref.shape[0]

    @pl.when(k == 0)
    def _():
        acc_scr[...] = jnp.zeros_like(acc_scr)
        t_idx = lax.broadcasted_iota(jnp.int32, (8, l), 1)
        alt = jnp.where(t_idx % 2 == 0, 1.0, -1.0).astype(BF16)
        xn_scr[...] = jnp.dot(alt, x, preferred_element_type=F32)

    xre = jnp.dot(crow_ref[...], x, preferred_element_type=F32)
    xim = -jnp.dot(srow_ref[...], x, preferred_element_type=F32)
    kre, kim = kre_ref[...], kim_ref[...]
    k_idx = lax.broadcasted_iota(jnp.int32, (tk, 1), 0) + k * tk
    weight = jnp.where(k_idx == 0, 1.0, 2.0)
    yre = ((xre * kre - xim * kim) * weight).astype(BF16)
    nyim = ((xre * kim + xim * kre) * -weight).astype(BF16)
    acc_scr[...] += (jnp.dot(ccol_ref[...], yre, preferred_element_type=F32)
                     + jnp.dot(scol_ref[...], nyim, preferred_element_type=F32))

    @pl.when(k == nk - 1)
    def _():
        t_idx = lax.broadcasted_iota(jnp.int32, (l, 1), 0)
        nyq = xn_scr[0:1, :] * knyq_ref[...]
        y_ref[0] = (acc_scr[...] + jnp.where(t_idx % 2 == 0, nyq, -nyq)) * (0.5 / l)


def long_conv(x, cos_t, sin_t, kre, kim, k_nyq):
    b, l, w = x.shape
    tk = 256
    row_tab = pl.BlockSpec((tk, l), lambda i, k: (k, 0))
    col_tab = pl.BlockSpec((l, tk), lambda i, k: (0, k))
    kspec = pl.BlockSpec((tk, w), lambda i, k: (k, 0))
    return pl.pallas_call(
        _long_conv_kernel,
        grid=(b, l // tk),
        in_specs=[pl.BlockSpec((1, l, w), lambda i, k: (i, 0, 0)), row_tab, row_tab, col_tab, col_tab,
                  kspec, kspec, pl.BlockSpec((1, w), lambda i, k: (0, 0))],
        out_specs=pl.BlockSpec((1, l, w), lambda i, k: (i, 0, 0)),
        out_shape=jax.ShapeDtypeStruct((b, l, w), F32),
        scratch_shapes=[pltpu.VMEM((l, w), F32), pltpu.VMEM((8, w), F32)],
        compiler_params=_cparams("parallel", "arbitrary"),
        name="hyena_long_conv",
    )(x, cos_t, sin_t, cos_t, sin_t, kre, kim, k_nyq)


def _hyena_gate_kernel(y_ref, vg_ref, x0_ref, bias_ref, o_ref):
    o_ref[...] = ((y_ref[...] + vg_ref[...] * bias_ref[...]) * x0_ref[...]).astype(o_ref.dtype)


def hyena_gate(y, vg, x0, bias):
    b, l, w = y.shape
    m = b * l
    tm = 1024
    row = pl.BlockSpec((tm, w), lambda i: (i, 0))
    out = pl.pallas_call(
        _hyena_gate_kernel,
        grid=(m // tm,),
        in_specs=[row, row, row, pl.BlockSpec((1, w), lambda i: (0, 0))],
        out_specs=row,
        out_shape=jax.ShapeDtypeStruct((m, w), BF16),
        compiler_params=_cparams("parallel"),
        name="hyena_gate",
    )(y.reshape(m, w), vg.reshape(m, w), x0.reshape(m, w), bias.reshape(1, w))
    return out.reshape(b, l, w)


def _dense_dft_tables(l):
    n = 2 * l
    ang = 2.0 * np.pi * np.outer(np.arange(n), np.arange(n)) / n
    fwd = np.concatenate([np.cos(ang), -np.sin(ang)], axis=0)
    inv = np.concatenate([np.cos(ang[:l]), -np.sin(ang[:l])], axis=1) / n
    return jnp.asarray(fwd, F32), jnp.asarray(inv, F32)


def _dense_conv_kernel(vg_ref, x0_ref, kern_ref, f_ref, g_ref, bias_ref, o_ref):
    hp = lax.Precision.HIGHEST
    l = vg_ref.shape[1]
    n = 2 * l
    vg = vg_ref[0]
    ks = jnp.dot(f_ref[...], kern_ref[...], preferred_element_type=F32, precision=hp)
    xs = jnp.dot(f_ref[:, :l], vg, preferred_element_type=F32, precision=hp)
    kre, kim, xre, xim = ks[:n], ks[n:], xs[:n], xs[n:]
    ys = jnp.concatenate([xre * kre - xim * kim, xre * kim + xim * kre], axis=0)
    y = jnp.dot(g_ref[...], ys, preferred_element_type=F32, precision=hp)
    o_ref[0] = ((y + vg * bias_ref[...]) * x0_ref[0]).astype(o_ref.dtype)


def hyena_dense_conv(vg, x0, kern, bias):
    bsz, l, w = vg.shape
    fwd, inv = _dense_dft_tables(l)
    row = pl.BlockSpec((1, l, w), lambda i: (i, 0, 0))
    full = lambda a: pl.BlockSpec(a.shape, lambda i: (0, 0))
    return pl.pallas_call(
        _dense_conv_kernel,
        grid=(bsz,),
        in_specs=[row, row, full(kern), full(fwd), full(inv), pl.BlockSpec((1, w), lambda i: (0, 0))],
        out_specs=row,
        out_shape=jax.ShapeDtypeStruct((bsz, l, w), BF16),
        compiler_params=_cparams("parallel"),
        name="hyena_dense_conv",
    )(vg, x0, kern, fwd, inv, bias.reshape(1, w))


HY_DENSE_MAX_LEN = 512


def hyena_mixer(z, lp, tables=None):
    l = z.shape[1]
    x0, vg, vg_bf16 = hyena_short_conv(z, lp['hy_short_w'], lp['hy_short_b'])
    h_fwd, h_bwd = hyena_filters(l, lp)
    if l <= HY_DENSE_MAX_LEN:
        kern = jnp.concatenate([h_fwd, jnp.zeros_like(h_fwd[:1]), h_bwd[1:][::-1]], axis=0)
        return hyena_dense_conv(vg, x0, kern, lp['hy_bias'])
    cos_t, sin_t = tables if tables is not None else dft_tables(l)
    kre, kim, k_nyq = filter_spectrum(cos_t, sin_t, h_fwd, h_bwd)
    y = long_conv(vg_bf16, cos_t, sin_t, kre, kim, k_nyq)
    return hyena_gate(y, vg, x0, lp['hy_bias'])


S5_CHUNK = 16
S5_CK = S5_CHUNK * S5_GROUP
S5_OCT = LANES // S5_GROUP
S5_OCT_W = S5_OCT * S5_GROUP
S5_N_OCT = S5_GROUPS // S5_OCT
S5_ROW = S5_CHUNK * S5_OCT_W


def _s5_powers(lp, d):
    lam = lax.complex(jnp.minimum(lp['s5_a_re'][d], -1e-4), lp['s5_a_im'][d])
    dt = jnp.exp(lp['s5_log_dt'][d])[:, None]
    n = jnp.arange(S5_CHUNK + 1, dtype=F32)[:, None, None]
    apow = jnp.exp((lam * dt)[None] * n)
    b = lax.complex(lp['s5_b_re'][d], lp['s5_b_im'][d])
    bbar = ((apow[1] - 1.0) / lam)[..., None] * b
    cmat = lax.complex(lp['s5_c_re'][d], lp['s5_c_im'][d])
    return apow, bbar, cmat


def _block_diag_kernel(x_ref, o_ref, *, rb, cb):
    n_g, rc, cc = x_ref.shape
    out_c = o_ref.shape[2]
    col = lax.broadcasted_iota(jnp.int32, (cc, out_c), 1)
    src = lax.broadcasted_iota(jnp.int32, (cc, out_c), 0)
    for g in range(n_g):
        dest = (src // cb) * (n_g * cb) + g * cb + src % cb
        place = jnp.where(col == dest, 1.0, 0.0).astype(BF16)
        wide = jnp.dot(x_ref[g], place, preferred_element_type=F32).astype(BF16)
        for r in range(rc // rb):
            r0 = (r * n_g + g) * rb
            o_ref[0, r0:r0 + rb, :] = wide[r * rb:(r + 1) * rb, :]


def s5_block_diag(x, rb, cb):
    g_n, rc, cc = x.shape
    out_r, out_c = rc * S5_OCT, cc * S5_OCT
    return pl.pallas_call(
        functools.partial(_block_diag_kernel, rb=rb, cb=cb),
        grid=(g_n // S5_OCT,),
        in_specs=[pl.BlockSpec((S5_OCT, rc, cc), lambda o: (o, 0, 0))],
        out_specs=pl.BlockSpec((1, out_r, out_c), lambda o: (o, 0, 0)),
        out_shape=jax.ShapeDtypeStruct((g_n // S5_OCT, out_r, out_c), BF16),
        compiler_params=_cparams("parallel"),
        name="s5_block_diag",
    )(x)


def s5_prepare(lp):
    q = S5_CHUNK
    g_n, p_n = S5_GROUPS, S5_STATE
    hp = lax.Precision.HIGHEST
    ap_f, bb_f, c_f = _s5_powers(lp, 0)
    ap_b, bb_b, c_b = _s5_powers(lp, 1)
    k_f = jnp.einsum('gop,tgp,gpi->gtio', c_f, ap_f[:q], bb_f, precision=hp).real
    k_b = jnp.einsum('gop,tgp,gpi->gtio', c_b, ap_b[:q], bb_b, precision=hp).real
    zero = jnp.zeros_like(k_f)
    fwd_ext = jnp.concatenate([zero, k_f], axis=1)
    bwd_ext = jnp.concatenate([k_b[:, ::-1], zero], axis=1)
    m_f = jnp.stack([fwd_ext[:, q - sp:2 * q - sp] for sp in range(q)], axis=1)
    m_b = jnp.stack([bwd_ext[:, q - 1 - sp:2 * q - 1 - sp] for sp in range(q)], axis=1)
    m_intra = (m_f + m_b).transpose(0, 1, 3, 2, 4).reshape(g_n, S5_CK, S5_CK)

    ws_f = ap_f[q - 1 - jnp.arange(q)][:, :, :, None] * bb_f[None]
    ws_b = ap_b[jnp.arange(q)][:, :, :, None] * bb_b[None]
    ws = jnp.stack([ws_f.real, ws_f.imag, ws_b.real, ws_b.imag], axis=0)
    ws = ws.transpose(2, 1, 4, 0, 3).reshape(g_n, S5_CK, 4, p_n)

    ca_f = c_f[None] * ap_f[1:q + 1][:, :, None, :]
    ca_b = c_b[None] * ap_b[q - jnp.arange(q)][:, :, None, :]
    wo = jnp.stack([ca_f.real, -ca_f.imag, ca_b.real, -ca_b.imag], axis=0)
    wo = wo.transpose(2, 0, 4, 1, 3).reshape(g_n, 4, p_n, S5_CK)

    h_n = S5_GROUP
    m_oct = s5_block_diag(m_intra.astype(BF16), h_n, h_n)
    w_state = s5_block_diag(ws.reshape(g_n, S5_CK, 4 * p_n).astype(BF16), h_n, p_n)
    w_out = s5_block_diag(wo.reshape(g_n, 4 * p_n, S5_CK).astype(BF16), p_n, h_n)

    def flat(z):
        return jnp.stack([z.real.reshape(-1), z.imag.reshape(-1)], axis=0)

    return m_oct, w_state, w_out, flat(ap_f[q]), flat(ap_b[q])


def _s5_chunk_rows(u_ref, n_chunks):
    steps = [u_ref[0, pl.ds(s, n_chunks, stride=S5_CHUNK), :].astype(BF16) for s in range(S5_CHUNK)]
    return jnp.concatenate(steps, axis=1)


def _s5_states_kernel(u_ref, w_ref, fre_ref, fim_ref, bre_ref, bim_ref):
    n_chunks, n = fre_ref.shape[1:]
    o = jnp.dot(_s5_chunk_rows(u_ref, n_chunks), w_ref[0], preferred_element_type=F32)
    for comp, ref in enumerate((fre_ref, fim_ref, bre_ref, bim_ref)):
        ref[0] = o[:, comp * n:(comp + 1) * n]


def s5_states(u_all, w_state):
    b, t, w = u_all.shape
    nc = t // S5_CHUNK
    pw = S5_OCT * S5_STATE
    out = jax.ShapeDtypeStruct((b, nc, S5_GROUPS * S5_STATE), F32)
    ospec = pl.BlockSpec((1, nc, pw), lambda o, i: (i, 0, o))
    return pl.pallas_call(
        _s5_states_kernel,
        grid=(S5_N_OCT, b),
        in_specs=[pl.BlockSpec((1, t, S5_OCT_W), lambda o, i: (i, 0, o)),
                  pl.BlockSpec((1, S5_ROW, 4 * pw), lambda o, i: (o, 0, 0))],
        out_specs=[ospec] * 4,
        out_shape=[out] * 4,
        compiler_params=_cparams("parallel", "parallel"),
        name="s5_states",
    )(u_all, w_state)


def _s5_scan_kernel(sre_ref, sim_ref, a_ref, hre_ref, him_ref, *, n_ctx_tiles, reverse):
    tile = 8
    n_tiles = sre_ref.shape[1] // tile
    ar = a_ref[0:1, :]
    ai = a_ref[1:2, :]
    zero = jnp.zeros((1, sre_ref.shape[2]), F32)

    def body(n, carry):
        hre, him = carry
        if reverse:
            t = jnp.where(n < n_ctx_tiles, n_ctx_tiles - 1 - n, n_tiles + n_ctx_tiles - 1 - n)
        else:
            t = n
        r0 = pl.multiple_of(t * tile, tile)
        sre = sre_ref[0, pl.ds(r0, tile), :]
        sim = sim_ref[0, pl.ds(r0, tile), :]
        in_re = [None] * tile
        in_im = [None] * tile
        for r in (range(tile - 1, -1, -1) if reverse else range(tile)):
            in_re[r], in_im[r] = hre, him
            hre, him = ar * hre - ai * him + sre[r:r + 1], ar * him + ai * hre + sim[r:r + 1]
        hre_ref[0, pl.ds(r0, tile), :] = jnp.concatenate(in_re, axis=0)
        him_ref[0, pl.ds(r0, tile), :] = jnp.concatenate(in_im, axis=0)
        return hre, him

    lax.fori_loop(0, n_tiles, body, (zero, zero))


def s5_scan(sre, sim, a_chunk, n_ctx_chunks, reverse):
    b, nc, w = sre.shape
    tl = 512
    spec = pl.BlockSpec((1, nc, tl), lambda i, j: (i, 0, j))
    out = jax.ShapeDtypeStruct((b, nc, w), F32)
    return pl.pallas_call(
        functools.partial(_s5_scan_kernel, n_ctx_tiles=n_ctx_chunks // 8, reverse=reverse),
        grid=(b, w // tl),
        in_specs=[spec, spec, pl.BlockSpec((2, tl), lambda i, j: (0, j))],
        out_specs=[spec, spec],
        out_shape=[out, out],
        compiler_params=_cparams("parallel", "parallel"),
        name="s5_scan_bwd" if reverse else "s5_scan_fwd",
    )(sre, sim, a_chunk)


def _s5_outputs_kernel(u_ref, m_ref, fre_ref, fim_ref, bre_ref, bim_ref, wo_ref, y_ref):
    n_chunks = fre_ref.shape[1]
    carried = jnp.concatenate([r[0].astype(BF16) for r in (fre_ref, fim_ref, bre_ref, bim_ref)], axis=1)
    y = (jnp.dot(_s5_chunk_rows(u_ref, n_chunks), m_ref[0], preferred_element_type=F32)
         + jnp.dot(carried, wo_ref[0], preferred_element_type=F32))
    for s in range(S5_CHUNK):
        y_ref[0, pl.ds(s, n_chunks, stride=S5_CHUNK), :] = y[:, s * S5_OCT_W:(s + 1) * S5_OCT_W]


def s5_outputs(u_all, m_oct, carried, w_out):
    b, t, w = u_all.shape
    nc = t // S5_CHUNK
    pw = S5_OCT * S5_STATE
    hspec = pl.BlockSpec((1, nc, pw), lambda o, i: (i, 0, o))
    lane_blk = pl.BlockSpec((1, t, S5_OCT_W), lambda o, i: (i, 0, o))
    mat = pl.BlockSpec((1, S5_ROW, S5_ROW), lambda o, i: (o, 0, 0))
    return pl.pallas_call(
        _s5_outputs_kernel,
        grid=(S5_N_OCT, b),
        in_specs=[lane_blk, mat, hspec, hspec, hspec, hspec, mat],
        out_specs=lane_blk,
        out_shape=jax.ShapeDtypeStruct((b, t, w), F32),
        compiler_params=_cparams("parallel", "parallel"),
        name="s5_outputs",
    )(u_all, m_oct, *carried, w_out)


def _s5_readout_kernel(y_ref, u_ref, d_ref, w_ref, b_ref, o_ref):
    y = _gelu_tanh(y_ref[...] + d_ref[...] * u_ref[...])
    z = jnp.dot(y.astype(BF16), w_ref[...], preferred_element_type=F32) + b_ref[...]
    o_ref[...] = (y * jax.nn.sigmoid(z)).astype(o_ref.dtype)


def s5_readout(y, u, d, glu_w, glu_b):
    m, w = y.shape
    tm = min(m, 512)
    row = pl.BlockSpec((tm, w), lambda i: (i, 0))
    vec = pl.BlockSpec((1, w), lambda i: (0, 0))
    return pl.pallas_call(
        _s5_readout_kernel,
        grid=(m // tm,),
        in_specs=[row, row, vec, pl.BlockSpec((w, w), lambda i: (0, 0)), vec],
        out_specs=row,
        out_shape=jax.ShapeDtypeStruct((m, w), BF16),
        compiler_params=_cparams("parallel"),
        name="s5_readout",
    )(y, u, d.reshape(1, w), glu_w, glu_b.reshape(1, w))


def s5_mixer(u, uc, lp, glu_w):
    b, l, w = u.shape
    c = uc.shape[1]
    m_oct, w_state, w_out, a_f, a_b = s5_prepare(lp)
    u_all = jnp.concatenate([uc, u], axis=1)
    fre, fim, bre, bim = s5_states(u_all, w_state)
    hf = s5_scan(fre, fim, a_f, c // S5_CHUNK, False)
    hb = s5_scan(bre, bim, a_b, c // S5_CHUNK, True)
    y_all = s5_outputs(u_all, m_oct, (*hf, *hb), w_out)
    out = s5_readout(y_all.reshape(b * (c + l), w), u_all.reshape(b * (c + l), w), lp['s5_d'], glu_w,
                     lp['s5_glu_b'])
    out = out.reshape(b, c + l, w)
    return out[:, c:], out[:, :c]


def kernel(x, c, ctx, c_ctx, mod_w, mod_b, norm1_g, norm2_g, in_w, gate_b, attn_sink, hy_short_w, hy_short_b,
           hy_w1, hy_b1, hy_freq1, hy_w2, hy_b2, hy_freq2, hy_w3, hy_bias, s5_a_re, s5_a_im, s5_log_dt,
           s5_b_re, s5_b_im, s5_c_re, s5_c_im, s5_d, s5_glu_w, s5_glu_b, br_attn_w, br_hyena_w, br_s5_w,
           out_w, peer_wq, peer_keys, peer_u, peer_v, final_g):
    B, L, D = x.shape
    C = ctx.shape[1]
    depth = mod_w.shape[0]
    rope_cos, rope_sin = rope_tables(L)
    conv_tables = dft_tables(L) if L > HY_DENSE_MAX_LEN else None

    cvec = jnp.zeros((8, D), F32).at[:B].set(c).at[B].set(c_ctx)
    mod_all = modulation(cvec, mod_w, mod_b)

    xc = ctx
    for layer in range(depth):
        need_ctx = layer < depth - 1
        lp = dict(attn_sink=attn_sink[layer],
                  hy_short_w=hy_short_w[layer], hy_short_b=hy_short_b[layer],
                  hy_w1=hy_w1[layer], hy_b1=hy_b1[layer], hy_freq1=hy_freq1[layer],
                  hy_w2=hy_w2[layer], hy_b2=hy_b2[layer], hy_freq2=hy_freq2[layer],
                  hy_w3=hy_w3[layer], hy_bias=hy_bias[layer],
                  s5_a_re=s5_a_re[layer], s5_a_im=s5_a_im[layer], s5_log_dt=s5_log_dt[layer],
                  s5_b_re=s5_b_re[layer], s5_b_im=s5_b_im[layer],
                  s5_c_re=s5_c_re[layer], s5_c_im=s5_c_im[layer], s5_d=s5_d[layer],
                  s5_glu_w=s5_glu_w[layer], s5_glu_b=s5_glu_b[layer])
        mod = mod_all[layer]
        ml = [mod[:B, i * D:(i + 1) * D] for i in range(N_MOD)]
        mc = [mod[B:B + 1, i * D:(i + 1) * D] for i in range(N_MOD)]

        w_in = in_w[layer].astype(BF16)
        w_parts = [w_in[:, COL_Q:COL_S5], w_in[:, COL_S5:COL_HY], w_in[:, COL_HY:COL_GATE], w_in[:, COL_GATE:]]
        wa = br_attn_w[layer].astype(BF16)
        wh = br_hyena_w[layer].astype(BF16)
        ws = br_s5_w[layer].astype(BF16)
        wo = out_w[layer].astype(BF16)
        wqt = peer_wq[layer].T.astype(BF16)
        keys = peer_keys[layer].reshape(2 * PEER_HEADS, PEER_KEYS, PEER_HALF).astype(BF16)
        u_tab = peer_u[layer].astype(BF16)
        vt_tab = peer_v[layer].T.astype(BF16)

        h = norm_modulate(x, norm1_g[layer], ml[0], ml[1])
        hc = norm_modulate(xc, norm1_g[layer], mc[0], mc[1])
        qkv, u_s5, z_hy, gates = in_projection(h.reshape(B * L, D), w_parts, gate_b[layer])
        qkv_c, uc_s5, zc_hy, gates_c = in_projection(hc.reshape(B * C, D), w_parts, gate_b[layer])
        qkv = qkv.reshape(B, L, -1)
        qkv_c = qkv_c.reshape(B, C, -1)
        u_s5 = u_s5.reshape(B, L, -1)
        uc_s5 = uc_s5.reshape(B, C, -1)

        q, k, v = qkv_heads(qkv, rope_cos, rope_sin, True)
        qc, kc, vc = qkv_heads(qkv_c, rope_cos, rope_sin, False)
        att = attention(q, k, v, kc, vc, lp['attn_sink'], True)
        hy = hyena_mixer(z_hy.reshape(B, L, -1), lp, conv_tables)
        s5, s5c = s5_mixer(u_s5, uc_s5, lp, s5_glu_w[layer].astype(BF16))

        x = branch_merge(att, hy, s5, gates.reshape(B, L, -1), wa, wh, ws, wo, x, ml[2])
        x = peer_block(x, norm2_g[layer], ml[3], ml[4], ml[5], wqt, keys, u_tab, vt_tab)

        if need_ctx:
            att_c = attention(qc, None, None, kc, vc, lp['attn_sink'], False)
            hy_c = hyena_mixer(zc_hy.reshape(B, C, -1), lp)
            xc = branch_merge(att_c, hy_c, s5c, gates_c.reshape(B, C, -1), wa, wh, ws, wo, xc, mc[2])
            xc = peer_block(xc, norm2_g[layer], mc[3], mc[4], mc[5], wqt, keys, u_tab, vt_tab)
    return final_norm(x, final_g)
```

```python
import functools
import math

import jax
import jax.numpy as jnp
import numpy as np
from jax import lax
from jax.experimental import pallas as pl
from jax.experimental.pallas import tpu as pltpu

F32 = jnp.float32
BF16 = jnp.bfloat16

NORM_EPS = 1e-6
NEG_INF = -1e30
N_MOD = 6

GRID_W = 64
ATT_HEADS = 8
ATT_KV_HEADS = 2
ATT_GROUP = ATT_HEADS // ATT_KV_HEADS
HEAD_DIM = 64
ATT_WIDTH = ATT_HEADS * HEAD_DIM
KV_WIDTH = ATT_KV_HEADS * HEAD_DIM
WINDOW = 128
ATT_BLOCK = 128
ROPE_BASE = 10000.0
ROPE_PAIRS_PER_AXIS = HEAD_DIM // 4

HY_WIDTH = 512
HY_N_PROJ = 3
HY_SHORT = 3
HY_BANDS = 16
HY_DECAY_TARGET = 1e-2
HY_FAST_RATE = -math.log(HY_DECAY_TARGET) / 0.3
HY_SLOW_RATE = -math.log(HY_DECAY_TARGET) / 1.5

S5_WIDTH = 512
S5_GROUP = 16
S5_GROUPS = S5_WIDTH // S5_GROUP
S5_STATE = 64

N_BRANCH = 3

PEER_HEADS = 8
PEER_KEYS = 128
PEER_HALF = 128
PEER_TOPK = 16

COL_Q = 0
COL_K = COL_Q + ATT_WIDTH
COL_V = COL_K + KV_WIDTH
COL_S5 = COL_V + KV_WIDTH
COL_HY = COL_S5 + S5_WIDTH
COL_GATE = COL_HY + HY_N_PROJ * HY_WIDTH

VMEM_LIMIT_V7X = 56 * 1024 * 1024


def _cparams(*sem):
    return pltpu.CompilerParams(dimension_semantics=sem, vmem_limit_bytes=VMEM_LIMIT_V7X)


def _gelu_tanh(x):
    return 0.5 * x * (1.0 + jnp.tanh(math.sqrt(2.0 / math.pi) * (x + 0.044715 * (x * x * x))))


def _mod_kernel(s_ref, w_ref, b_ref, o_ref):
    s = s_ref[...]
    s = (s * jax.nn.sigmoid(s)).astype(BF16)
    o_ref[0] = jnp.dot(s, w_ref[0].astype(BF16), preferred_element_type=F32) + b_ref[0]


def modulation(cvec, mod_w, mod_b):
    depth, d, n = mod_w.shape
    tn = 1024
    return pl.pallas_call(
        _mod_kernel,
        grid=(depth, n // tn),
        in_specs=[pl.BlockSpec((8, d), lambda l, j: (0, 0)),
                  pl.BlockSpec((1, d, tn), lambda l, j: (l, 0, j)),
                  pl.BlockSpec((1, 1, tn), lambda l, j: (l, 0, j))],
        out_specs=pl.BlockSpec((1, 8, tn), lambda l, j: (l, 0, j)),
        out_shape=jax.ShapeDtypeStruct((depth, 8, n), F32),
        compiler_params=_cparams("parallel", "parallel"),
        name="adaln_mod",
    )(cvec, mod_w, mod_b.reshape(depth, 1, n))


def _norm_mod_kernel(x_ref, g_ref, sh_ref, sc_ref, o_ref, *, transposed):
    x = x_ref[0]
    y = x * lax.rsqrt(jnp.mean(x * x, axis=-1, keepdims=True) + NORM_EPS)
    y = y * g_ref[...]
    y = y * (1.0 + sc_ref[0]) + sh_ref[0]
    if transposed:
        o_ref[...] = y.T.astype(o_ref.dtype)
    else:
        o_ref[0] = y.astype(o_ref.dtype)


def norm_modulate(x, g, shift, scale, transposed=False):
    b, l, d = x.shape
    tl = min(l, 512)
    nt = l // tl
    per_batch = shift.shape[0] == b and b > 1
    mod_map = (lambda i, j: (i, 0, 0)) if per_batch else (lambda i, j: (0, 0, 0))
    if transposed:
        out_spec = pl.BlockSpec((d, tl), lambda i, j: (0, i * nt + j))
        out_shape = jax.ShapeDtypeStruct((d, b * l), BF16)
    else:
        out_spec = pl.BlockSpec((1, tl, d), lambda i, j: (i, j, 0))
        out_shape = jax.ShapeDtypeStruct((b, l, d), BF16)
    return pl.pallas_call(
        functools.partial(_norm_mod_kernel, transposed=transposed),
        grid=(b, nt),
        in_specs=[pl.BlockSpec((1, tl, d), lambda i, j: (i, j, 0)),
                  pl.BlockSpec((1, d), lambda i, j: (0, 0)),
                  pl.BlockSpec((1, 1, d), mod_map),
                  pl.BlockSpec((1, 1, d), mod_map)],
        out_specs=out_spec,
        out_shape=out_shape,
        compiler_params=_cparams("parallel", "parallel"),
        name="norm_modulate_t" if transposed else "norm_modulate",
    )(x, g.reshape(1, d), shift.reshape(-1, 1, d), scale.reshape(-1, 1, d))


def _final_norm_kernel(x_ref, g_ref, o_ref):
    x = x_ref[...]
    y = x * lax.rsqrt(jnp.mean(x * x, axis=-1, keepdims=True) + NORM_EPS)
    o_ref[...] = y * g_ref[...]


def final_norm(x, g):
    b, l, d = x.shape
    x2 = x.reshape(b * l, d)
    tm = 512
    out = pl.pallas_call(
        _final_norm_kernel,
        grid=(b * l // tm,),
        in_specs=[pl.BlockSpec((tm, d), lambda i: (i, 0)), pl.BlockSpec((1, d), lambda i: (0, 0))],
        out_specs=pl.BlockSpec((tm, d), lambda i: (i, 0)),
        out_shape=jax.ShapeDtypeStruct((b * l, d), F32),
        compiler_params=_cparams("parallel"),
        name="final_norm",
    )(x2, g.reshape(1, d))
    return out.reshape(b, l, d)


def _inproj_kernel(h_ref, wqkv_ref, ws5_ref, why_ref, wg_ref, gb_ref, oqkv_ref, os5_ref, ohy_ref, og_ref):
    h = h_ref[...]
    oqkv_ref[...] = jnp.dot(h, wqkv_ref[...], preferred_element_type=F32)
    os5_ref[...] = jnp.dot(h, ws5_ref[...], preferred_element_type=F32)
    ohy_ref[...] = jnp.dot(h, why_ref[...], preferred_element_type=F32)
    og_ref[...] = (jnp.dot(h, wg_ref[...], preferred_element_type=F32) + gb_ref[...]).astype(og_ref.dtype)


def in_projection(h, w_parts, gate_b):
    m, d = h.shape
    tm = 256
    widths = [w.shape[1] for w in w_parts]
    w_specs = [pl.BlockSpec((d, n), lambda i: (0, 0)) for n in widths]
    return pl.pallas_call(
        _inproj_kernel,
        grid=(m // tm,),
        in_specs=[pl.BlockSpec((tm, d), lambda i: (i, 0))] + w_specs
                 + [pl.BlockSpec((1, widths[3]), lambda i: (0, 0))],
        out_specs=[pl.BlockSpec((tm, n), lambda i: (i, 0)) for n in widths],
        out_shape=[jax.ShapeDtypeStruct((m, n), F32) for n in widths[:3]]
                  + [jax.ShapeDtypeStruct((m, widths[3]), BF16)],
        compiler_params=_cparams("parallel"),
        name="in_projection",
    )(h, *w_parts, gate_b.reshape(1, -1))


def _merge_kernel(att_ref, hy_ref, s5_ref, gl_ref, wa_ref, wh_ref, ws_ref, wo_ref, x_ref, g1_ref, o_ref):
    d = wo_ref.shape[0]
    g = jax.nn.sigmoid(gl_ref[0].astype(F32))
    m = g[:, :d] * jnp.dot(att_ref[0], wa_ref[...], preferred_element_type=F32)
    m = m + g[:, d:2 * d] * jnp.dot(hy_ref[0], wh_ref[...], preferred_element_type=F32)
    m = m + g[:, 2 * d:] * jnp.dot(s5_ref[0], ws_ref[...], preferred_element_type=F32)
    y = jnp.dot(m.astype(BF16), wo_ref[...], preferred_element_type=F32)
    o_ref[0] = x_ref[0] + g1_ref[0] * y


def branch_merge(att, hy, s5, gate_logits, wa, wh, ws, wo, x, g1):
    b, l, d = x.shape
    tl = min(l, 256)
    wd = att.shape[-1]
    per_batch = g1.shape[0] == b and b > 1
    g_map = (lambda i, j: (i, 0, 0)) if per_batch else (lambda i, j: (0, 0, 0))
    row = lambda n: pl.BlockSpec((1, tl, n), lambda i, j: (i, j, 0))
    full = lambda a: pl.BlockSpec(a.shape, lambda i, j: (0, 0))
    return pl.pallas_call(
        _merge_kernel,
        grid=(b, l // tl),
        in_specs=[row(wd), row(wd), row(wd), row(N_BRANCH * d), full(wa), full(wh), full(ws), full(wo),
                  row(d), pl.BlockSpec((1, 1, d), g_map)],
        out_specs=row(d),
        out_shape=jax.ShapeDtypeStruct((b, l, d), F32),
        compiler_params=_cparams("parallel", "parallel"),
        name="branch_merge",
    )(att, hy, s5, gate_logits, wa, wh, ws, wo, x, g1.reshape(-1, 1, d))


def _topk_rows(s, k):
    rows = []
    for _ in range(k):
        m = jnp.max(s, axis=0, keepdims=True)
        rows.append(m)
        s = jnp.where(s >= m, NEG_INF, s)
    return rows


LANES = 128


def _peer_select_kernel(ht_ref, wqt_ref, keys_ref, rank_ref, cnt_ref, e1_ref, e2_ref, qt_scr, cand_scr):
    nh = rank_ref.shape[0]
    nk = PEER_KEYS
    pairs = [(p, q) for p in range(PEER_TOPK) for q in range(PEER_TOPK) if (p + 1) * (q + 1) <= PEER_TOPK]

    def project(hh):
        r0 = pl.multiple_of(hh * 2 * nk, 2 * nk)
        qt_scr[hh % 2] = jnp.dot(wqt_ref[pl.ds(r0, 2 * nk), :], ht_ref[...],
                                 preferred_element_type=F32).astype(BF16)

    project(0)

    def head(hh, carry):
        qt = qt_scr[hh % 2]
        s1 = jnp.dot(keys_ref[2 * hh], qt[:nk], preferred_element_type=F32)
        s2 = jnp.dot(keys_ref[2 * hh + 1], qt[nk:], preferred_element_type=F32)
        project(jnp.minimum(hh + 1, nh - 1))
        for tb in range(s1.shape[1] // LANES):
            lanes = slice(tb * LANES, (tb + 1) * LANES)
            select_block(hh, tb, lanes, s1[:, lanes], s2[:, lanes])
        return carry

    def select_block(hh, tb, lanes, s1, s2):
        a = _topk_rows(s1, PEER_TOPK)
        b = []
        rest = s2
        rank = jnp.full(s2.shape, float(PEER_TOPK), F32)
        for r in range(PEER_TOPK):
            m = jnp.max(rest, axis=0, keepdims=True)
            b.append(m)
            hit = rest >= m
            rank = jnp.where(hit, float(r), rank)
            rest = jnp.where(hit, NEG_INF, rest)
        cand_scr[:, lanes] = jnp.full((cand_scr.shape[0], LANES), NEG_INF, F32)
        for r, (p, q) in enumerate(pairs):
            cand_scr[r:r + 1, lanes] = a[p] + b[q]
        cand = cand_scr[:, lanes]
        best = _topk_rows(cand, PEER_TOPK)
        top = a[0] + b[0]
        z = jnp.exp(best[0] - top)
        for r in range(1, PEER_TOPK):
            z = z + jnp.exp(best[r] - top)
        tau = best[PEER_TOPK - 1]
        reached = jnp.where(cand >= tau, 1.0, 0.0)
        cnt = jnp.zeros(s1.shape, F32)
        r = 0
        for p in range(PEER_TOPK):
            n_q = sum(1 for pp, _ in pairs if pp == p)
            cnt_p = reached[r:r + 1]
            for rr in range(r + 1, r + n_q):
                cnt_p = cnt_p + reached[rr:rr + 1]
            r += n_q
            cnt = jnp.where(s1 == a[p], cnt_p, cnt)
        rank_ref[hh, :, lanes] = rank.astype(BF16)
        cnt_ref[hh, tb] = cnt
        e1_ref[hh, tb] = 0.5 * jnp.exp(s1 - a[0])
        e2_ref[hh, :, lanes] = (jnp.exp(s2 - b[0]) / z).astype(BF16)

    lax.fori_loop(0, nh, head, 0)


def peer_select(ht, wqt, keys, tt):
    d, m = ht.shape
    nh = keys.shape[0] // 2
    out = jax.ShapeDtypeStruct((nh, PEER_KEYS, m), BF16)
    out32 = jax.ShapeDtypeStruct((nh, m // LANES, PEER_KEYS, LANES), F32)
    ospec = pl.BlockSpec((nh, PEER_KEYS, tt), lambda i: (0, 0, i))
    ospec32 = pl.BlockSpec((nh, tt // LANES, PEER_KEYS, LANES), lambda i: (0, i, 0, 0))
    return pl.pallas_call(
        _peer_select_kernel,
        grid=(m // tt,),
        in_specs=[pl.BlockSpec((d, tt), lambda i: (0, i)),
                  pl.BlockSpec(wqt.shape, lambda i: (0, 0)),
                  pl.BlockSpec(keys.shape, lambda i: (0, 0, 0))],
        out_specs=[ospec, ospec32, ospec32, ospec],
        out_shape=[out, out32, out32, out],
        scratch_shapes=[pltpu.VMEM((2, 2 * PEER_KEYS, tt), BF16), pltpu.VMEM((56, tt), F32)],
        compiler_params=_cparams("parallel"),
        name="peer_select",
    )(ht, wqt, keys)


PEER_PAIR = 2 * PEER_KEYS
PEER_LOOKAHEAD = 2


def _half_gate_gelu(half_w, x):
    c = math.sqrt(2.0 / math.pi)
    p = half_w * x
    return p + p * jnp.tanh(x * (c + (c * 0.044715) * (x * x)))


def _key_row(ref, hh, ii):
    blocks = [jnp.broadcast_to(ref[hh, tb, ii:ii + 1, :], (PEER_KEYS, LANES)) for tb in range(ref.shape[1])]
    return jnp.concatenate(blocks, axis=1).astype(BF16)


def _peer_expert_kernel(ht_ref, u_ref, vt_ref, rank_ref, cnt_ref, e1_ref, e2_ref, x_ref, g2_ref, fg_ref, o_ref,
                        acc_scr, wa_scr, *, out_norm):
    k = pl.program_id(1)
    nh = rank_ref.shape[0]

    @pl.when(k == 0)
    def _():
        acc_scr[...] = jnp.zeros_like(acc_scr)

    ht = ht_ref[...]
    n_pairs = u_ref.shape[0] // PEER_PAIR

    def cols(p):
        return slice(p * PEER_PAIR, (p + 1) * PEER_PAIR)

    def first_matmul(p):
        return jnp.dot(u_ref[cols(p), :], ht, preferred_element_type=F32)

    def gated(p, act):
        w_rows = []
        for r in range(2):
            ii = 2 * p + r
            w = None
            for hh in range(nh):
                selected = rank_ref[hh] < _key_row(cnt_ref, hh, ii)
                gate = jnp.where(selected, e2_ref[hh], jnp.zeros((), BF16)) * _key_row(e1_ref, hh, ii)
                w = gate if w is None else w + gate
            w_rows.append(w)
        half_w = jnp.concatenate(w_rows, axis=0)
        return _half_gate_gelu(half_w, act.astype(BF16))

    acts = {q: first_matmul(q) for q in range(min(PEER_LOOKAHEAD, n_pairs))}
    part = None
    for p in range(n_pairs + 1):
        if p + PEER_LOOKAHEAD < n_pairs:
            acts[p + PEER_LOOKAHEAD] = first_matmul(p + PEER_LOOKAHEAD)
        if p < n_pairs:
            wa_scr[p] = gated(p, acts.pop(p))
        if p >= 1:
            d = jnp.dot(vt_ref[:, cols(p - 1)], wa_scr[p - 1], preferred_element_type=F32)
            part = d if part is None else part + d
    acc_scr[...] += part

    @pl.when(k == pl.num_programs(1) - 1)
    def _():
        y = x_ref[...] + g2_ref[0] * acc_scr[...].T
        if out_norm:
            y = y * lax.rsqrt(jnp.mean(y * y, axis=-1, keepdims=True) + NORM_EPS) * fg_ref[...]
        o_ref[...] = y


def peer_experts(ht, u_tab, vt_tab, sel, x, g2, tokens_per_batch, tt, ec, out_g, out_norm):
    d, m = ht.shape
    e = u_tab.shape[0]
    nh = sel[0].shape[0]
    rows_i = ec // PEER_KEYS
    full_keys = pl.BlockSpec((nh, PEER_KEYS, tt), lambda i, k: (0, 0, i))
    chunk_keys = pl.BlockSpec((nh, tt // LANES, rows_i, LANES), lambda i, k: (0, i, k, 0))
    tiles_per_batch = tokens_per_batch // tt
    if g2.shape[0] > 1:
        g_map = lambda i, k: (i // tiles_per_batch, 0, 0)
    else:
        g_map = lambda i, k: (0, 0, 0)
    rank, cnt, e1, e2 = sel
    return pl.pallas_call(
        functools.partial(_peer_expert_kernel, out_norm=out_norm),
        grid=(m // tt, e // ec),
        in_specs=[pl.BlockSpec((d, tt), lambda i, k: (0, i)),
                  pl.BlockSpec((ec, d), lambda i, k: (k, 0)),
                  pl.BlockSpec((d, ec), lambda i, k: (0, k)),
                  full_keys, chunk_keys, chunk_keys, full_keys,
                  pl.BlockSpec((tt, d), lambda i, k: (i, 0)),
                  pl.BlockSpec((1, 1, d), g_map),
                  pl.BlockSpec((1, d), lambda i, k: (0, 0))],
        out_specs=pl.BlockSpec((tt, d), lambda i, k: (i, 0)),
        out_shape=jax.ShapeDtypeStruct((m, d), F32),
        scratch_shapes=[pltpu.VMEM((d, tt), F32), pltpu.VMEM((ec // PEER_PAIR, PEER_PAIR, tt), BF16)],
        compiler_params=_cparams("parallel", "arbitrary"),
        name="peer_experts",
    )(ht, u_tab, vt_tab, rank, cnt, e1, e2, x, g2.reshape(-1, 1, d), out_g.reshape(1, d))


def peer_block(x, g_norm, shift, scale, gate, wqt, keys, u_tab, vt_tab, out_g, out_norm=False):
    b, l, d = x.shape
    ht = norm_modulate(x, g_norm, shift, scale, transposed=True)
    tt = min(l, 512)
    sel = peer_select(ht, wqt, keys, tt)
    out = peer_experts(ht, u_tab, vt_tab, sel, x.reshape(b * l, d), gate, l, tt, 2048, out_g, out_norm)
    return out.reshape(b, l, d)


def rope_tables(l):
    rows_n = l // GRID_W
    row = jnp.repeat(jnp.arange(rows_n), GRID_W).astype(F32)
    col = jnp.tile(jnp.arange(GRID_W), rows_n).astype(F32)
    inv = jnp.power(ROPE_BASE, -jnp.arange(ROPE_PAIRS_PER_AXIS, dtype=F32) / ROPE_PAIRS_PER_AXIS)
    ang = jnp.concatenate([row[:, None] * inv, col[:, None] * inv], axis=-1)
    c, s = jnp.cos(ang), jnp.sin(ang)
    return jnp.concatenate([c, c, c, c], axis=-1), jnp.concatenate([-s, s, -s, s], axis=-1)


def _rope_pair(x, cos2, sin2):
    half = HEAD_DIM // 2
    lane = lax.broadcasted_iota(jnp.int32, x.shape, 1)
    swapped = jnp.where(lane % HEAD_DIM < half, pltpu.roll(x, 2 * HEAD_DIM - half, axis=1),
                        pltpu.roll(x, half, axis=1))
    return x * cos2 + swapped * sin2


def _qkv_heads_kernel(qkv_ref, cos_ref, sin_ref, q_ref, k_ref, v_ref, *, rope):
    scale = HEAD_DIM ** -0.5
    x = qkv_ref[0]
    pair = 2 * HEAD_DIM
    for j in range((ATT_WIDTH + KV_WIDTH) // pair):
        blk = x[:, j * pair:(j + 1) * pair]
        if rope:
            blk = _rope_pair(blk, cos_ref[...], sin_ref[...])
        for t in range(2):
            head = blk[:, t * HEAD_DIM:(t + 1) * HEAD_DIM]
            h = 2 * j + t
            if h < ATT_HEADS:
                q_ref[0, h] = (head * scale).astype(q_ref.dtype)
            else:
                k_ref[0, h - ATT_HEADS] = head.astype(k_ref.dtype)
    for t in range(ATT_KV_HEADS):
        v_ref[0, t] = x[:, COL_V + t * HEAD_DIM:COL_V + (t + 1) * HEAD_DIM].astype(v_ref.dtype)


def qkv_heads(qkv, cos2, sin2, rope):
    b, l, w = qkv.shape
    tl = min(l, 256)
    tab = pl.BlockSpec((tl, 2 * HEAD_DIM), lambda i, j: (j, 0))
    hspec = lambda n: pl.BlockSpec((1, n, tl, HEAD_DIM), lambda i, j: (i, 0, j, 0))
    return pl.pallas_call(
        functools.partial(_qkv_heads_kernel, rope=rope),
        grid=(b, l // tl),
        in_specs=[pl.BlockSpec((1, tl, w), lambda i, j: (i, j, 0)), tab, tab],
        out_specs=[hspec(ATT_HEADS), hspec(ATT_KV_HEADS), hspec(ATT_KV_HEADS)],
        out_shape=[jax.ShapeDtypeStruct((b, ATT_HEADS, l, HEAD_DIM), BF16),
                   jax.ShapeDtypeStruct((b, ATT_KV_HEADS, l, HEAD_DIM), BF16),
                   jax.ShapeDtypeStruct((b, ATT_KV_HEADS, l, HEAD_DIM), BF16)],
        compiler_params=_cparams("parallel", "parallel"),
        name="qkv_heads",
    )(qkv, cos2, sin2)


def _attention_kernel(sink_ref, q_ref, kc_ref, vc_ref, *rest, local):
    if local:
        kp_ref, ko_ref, kn_ref, vp_ref, vo_ref, vn_ref, bias_ref, o_ref = rest
    else:
        (o_ref,) = rest
    blk = ATT_BLOCK
    rows = ATT_GROUP * blk
    outs = []
    for g in range(ATT_KV_HEADS):
        q = q_ref[0, g * ATT_GROUP:(g + 1) * ATT_GROUP].reshape(rows, HEAD_DIM)
        if local:
            keys = jnp.concatenate([kc_ref[0, g], kp_ref[0, g], ko_ref[0, g], kn_ref[0, g]], axis=0)
            vals = jnp.concatenate([vc_ref[0, g], vp_ref[0, g], vo_ref[0, g], vn_ref[0, g]], axis=0)
        else:
            keys, vals = kc_ref[0, g], vc_ref[0, g]
        s = lax.dot_general(q, keys, (((1,), (1,)), ((), ())), preferred_element_type=F32)
        r = lax.broadcasted_iota(jnp.int32, (rows, 1), 0)
        if local:
            s = s + jnp.concatenate([bias_ref[0]] * ATT_GROUP, axis=0)
        sink = jnp.zeros((rows, 1), F32)
        for t in range(ATT_GROUP):
            sink = jnp.where(r // blk == t, sink_ref[g * ATT_GROUP + t], sink)
        m = jnp.maximum(jnp.max(s, axis=-1, keepdims=True), sink)
        p = jnp.exp(s - m)
        denom = jnp.sum(p, axis=-1, keepdims=True) + jnp.exp(sink - m)
        o = jnp.dot(p.astype(BF16), vals, preferred_element_type=F32) / denom
        outs += [o[t * blk:(t + 1) * blk] for t in range(ATT_GROUP)]
    o_ref[0] = jnp.concatenate(outs, axis=1).astype(o_ref.dtype)


def attention(q, k, v, kc, vc, sink, local):
    b, _, l, d = q.shape
    c = kc.shape[2]
    nb = l // ATT_BLOCK
    qspec = pl.BlockSpec((1, ATT_HEADS, ATT_BLOCK, d), lambda i, j: (i, 0, j, 0))
    cspec = pl.BlockSpec((1, ATT_KV_HEADS, c, d), lambda i, j: (i, 0, 0, 0))
    in_specs = [pl.BlockSpec(memory_space=pltpu.SMEM), qspec, cspec, cspec]
    args = [sink, q, kc, vc]
    if local:
        kv = lambda f: pl.BlockSpec((1, ATT_KV_HEADS, ATT_BLOCK, d), lambda i, j: (i, 0, f(j), 0))
        band = [kv(lambda j: jnp.maximum(j - 1, 0)), kv(lambda j: j), kv(lambda j: jnp.minimum(j + 1, nb - 1))]
        in_specs += band + band
        args += [k, k, k, v, v, v]
        qpos = jnp.arange(ATT_BLOCK)[:, None]
        kpos = jnp.arange(-ATT_BLOCK, 2 * ATT_BLOCK)[None, :]
        near = jnp.abs(kpos - qpos) <= WINDOW
        variants = [near, near & (kpos >= 0), near & (kpos < ATT_BLOCK), near & (kpos >= 0) & (kpos < ATT_BLOCK)]
        bias = jnp.stack([jnp.concatenate([jnp.zeros((ATT_BLOCK, c), F32), jnp.where(ok, 0.0, NEG_INF)], axis=1)
                          for ok in variants])
        in_specs.append(pl.BlockSpec((1, ATT_BLOCK, c + 3 * ATT_BLOCK),
                                     lambda i, j: ((j == 0) + 2 * (j == nb - 1), 0, 0)))
        args.append(bias)
    return pl.pallas_call(
        functools.partial(_attention_kernel, local=local),
        grid=(b, nb),
        in_specs=in_specs,
        out_specs=pl.BlockSpec((1, ATT_BLOCK, ATT_HEADS * d), lambda i, j: (i, j, 0)),
        out_shape=jax.ShapeDtypeStruct((b, l, ATT_HEADS * d), BF16),
        compiler_params=_cparams("parallel", "parallel"),
        name="window_attention" if local else "context_attention",
    )(*args)


def hyena_filters(L, lp):
    t = jnp.arange(L, dtype=F32)
    tn = t / L
    bands = jnp.arange(1, HY_BANDS + 1, dtype=F32)
    ang = 2.0 * math.pi * tn[:, None] * bands[None, :]
    z = jnp.concatenate([tn[:, None], jnp.cos(ang), jnp.sin(ang)], axis=-1)
    hdn = jnp.sin(lp['hy_freq1'] * (z @ lp['hy_w1'] + lp['hy_b1']))
    hdn = jnp.sin(lp['hy_freq2'] * (hdn @ lp['hy_w2'] + lp['hy_b2']))
    filt = hdn @ lp['hy_w3']
    rate = jnp.linspace(HY_FAST_RATE, HY_SLOW_RATE, HY_WIDTH, dtype=F32)
    tw = jnp.linspace(0.0, 1.0, L, dtype=F32)
    window = jnp.exp(-tw[:, None] * rate[None, :])
    return filt[:, :HY_WIDTH] * window, filt[:, HY_WIDTH:] * window


def _short_conv_kernel(z_ref, zp_ref, zn_ref, w_ref, b_ref, x0_ref, vg_ref, vgb_ref):
    i = pl.program_id(1)
    z = z_ref[0]
    tl, w3 = z.shape
    prev_row = jnp.where(i > 0, zp_ref[0, 7:8, :], 0.0)
    next_row = jnp.where(i < pl.num_programs(1) - 1, zn_ref[0, 0:1, :], 0.0)
    row = lax.broadcasted_iota(jnp.int32, (tl, w3), 0)
    z_prev = jnp.where(row == 0, prev_row, pltpu.roll(z, 1, axis=0))
    z_next = jnp.where(row == tl - 1, next_row, pltpu.roll(z, tl - 1, axis=0))
    y = b_ref[...] + z_prev * w_ref[0:1, :] + z * w_ref[1:2, :] + z_next * w_ref[2:3, :]
    w = w3 // HY_N_PROJ
    x0_ref[0] = y[:, :w]
    vg = y[:, 2 * w:] * y[:, w:2 * w]
    vg_ref[0] = vg
    vgb_ref[0] = vg.astype(BF16)


def hyena_short_conv(z, w, b):
    bsz, l, w3 = z.shape
    tl = min(l, 256)
    nb8 = tl // 8
    last8 = l // 8 - 1
    out = jax.ShapeDtypeStruct((bsz, l, w3 // HY_N_PROJ), F32)
    ospec = pl.BlockSpec((1, tl, w3 // HY_N_PROJ), lambda i, j: (i, j, 0))
    return pl.pallas_call(
        _short_conv_kernel,
        grid=(bsz, l // tl),
        in_specs=[pl.BlockSpec((1, tl, w3), lambda i, j: (i, j, 0)),
                  pl.BlockSpec((1, 8, w3), lambda i, j: (i, jnp.maximum(j * nb8 - 1, 0), 0)),
                  pl.BlockSpec((1, 8, w3), lambda i, j: (i, jnp.minimum((j + 1) * nb8, last8), 0)),
                  pl.BlockSpec((HY_SHORT, w3), lambda i, j: (0, 0)),
                  pl.BlockSpec((1, w3), lambda i, j: (0, 0))],
        out_specs=[ospec, ospec, ospec],
        out_shape=[out, out, jax.ShapeDtypeStruct(out.shape, BF16)],
        compiler_params=_cparams("parallel", "parallel"),
        name="hyena_short_conv",
    )(z, z, z, w, b.reshape(1, w3))


def dft_tables(l):
    n = 2 * l
    split = 64
    t = jnp.arange(l, dtype=jnp.int32)[None, :]
    kh = jnp.arange(l // split, dtype=jnp.int32)[:, None]
    kl = jnp.arange(split, dtype=jnp.int32)[:, None]
    ang_h = (2.0 * math.pi / n) * (((kh * split) * t) % n).astype(F32)
    ang_l = (2.0 * math.pi / n) * ((kl * t) % n).astype(F32)
    ch, sh = jnp.cos(ang_h)[:, None, :], jnp.sin(ang_h)[:, None, :]
    cl, sl = jnp.cos(ang_l)[None, :, :], jnp.sin(ang_l)[None, :, :]
    cos_t = (ch * cl - sh * sl).reshape(l, l).astype(BF16)
    sin_t = (sh * cl + ch * sl).reshape(l, l).astype(BF16)
    return cos_t, sin_t


def _filter_spectrum_kernel(c_ref, s_ref, hp_ref, hm_ref, kre_ref, kim_ref):
    kre_ref[...] = jnp.dot(c_ref[...], hp_ref[...], preferred_element_type=F32)
    kim_ref[...] = -jnp.dot(s_ref[...], hm_ref[...], preferred_element_type=F32)


def filter_spectrum(cos_t, sin_t, h_fwd, h_bwd):
    l, w = h_fwd.shape
    hb0 = h_bwd.at[0].set(0.0)
    tk = 512
    tab = pl.BlockSpec((tk, l), lambda k: (k, 0))
    full = pl.BlockSpec((l, w), lambda k: (0, 0))
    ospec = pl.BlockSpec((tk, w), lambda k: (k, 0))
    out = jax.ShapeDtypeStruct((l, w), F32)
    kre, kim = pl.pallas_call(
        _filter_spectrum_kernel,
        grid=(l // tk,),
        in_specs=[tab, tab, full, full],
        out_specs=[ospec, ospec],
        out_shape=[out, out],
        compiler_params=_cparams("parallel"),
        name="hyena_filter_spectrum",
    )(cos_t, sin_t, (h_fwd + hb0).astype(BF16), (h_fwd - hb0).astype(BF16))
    sign = jnp.where(jnp.arange(l) % 2 == 0, 1.0, -1.0)[:, None]
    k_nyq = jnp.sum(sign * (h_fwd + hb0), axis=0, keepdims=True)
    return kre, kim, k_nyq


def _long_conv_kernel(x_ref, crow_ref, srow_ref, ccol_ref, scol_ref, kre_ref, kim_ref, knyq_ref, y_ref,
                      acc_scr, xn_scr):
    k = pl.program_id(1)
    nk = pl.num_programs(1)
    x = x_ref[0]
    l = x.shape[0]
    tk = crow_ref.shape[0]

    @pl.when(k == 0)
    def _():
        acc_scr[...] = jnp.zeros_like(acc_scr)
        t_idx = lax.broadcasted_iota(jnp.int32, (8, l), 1)
        alt = jnp.where(t_idx % 2 == 0, 1.0, -1.0).astype(BF16)
        xn_scr[...] = jnp.dot(alt, x, preferred_element_type=F32)

    xre = jnp.dot(crow_ref[...], x, preferred_element_type=F32)
    xim = -jnp.dot(srow_ref[...], x, preferred_element_type=F32)
    kre, kim = kre_ref[...], kim_ref[...]
    k_idx = lax.broadcasted_iota(jnp.int32, (tk, 1), 0) + k * tk
    weight = jnp.where(k_idx == 0, 1.0, 2.0)
    yre = ((xre * kre - xim * kim) * weight).astype(BF16)
    nyim = ((xre * kim + xim * kre) * -weight).astype(BF16)
    acc_scr[...] += (jnp.dot(ccol_ref[...], yre, preferred_element_type=F32)
                     + jnp.dot(scol_ref[...], nyim, preferred_element_type=F32))

    @pl.when(k == nk - 1)
    def _():
        t_idx = lax.broadcasted_iota(jnp.int32, (l, 1), 0)
        nyq = xn_scr[0:1, :] * knyq_ref[...]
        y_ref[0] = (acc_scr[...] + jnp.where(t_idx % 2 == 0, nyq, -nyq)) * (0.5 / l)


def long_conv(x, cos_t, sin_t, kre, kim, k_nyq):
    b, l, w = x.shape
    tk = 256
    row_tab = pl.BlockSpec((tk, l), lambda i, k: (k, 0))
    col_tab = pl.BlockSpec((l, tk), lambda i, k: (0, k))
    kspec = pl.BlockSpec((tk, w), lambda i, k: (k, 0))
    return pl.pallas_call(
        _long_conv_kernel,
        grid=(b, l // tk),
        in_specs=[pl.BlockSpec((1, l, w), lambda i, k: (i, 0, 0)), row_tab, row_tab, col_tab, col_tab,
                  kspec, kspec, pl.BlockSpec((1, w), lambda i, k: (0, 0))],
        out_specs=pl.BlockSpec((1, l, w), lambda i, k: (i, 0, 0)),
        out_shape=jax.ShapeDtypeStruct((b, l, w), F32),
        scratch_shapes=[pltpu.VMEM((l, w), F32), pltpu.VMEM((8, w), F32)],
        compiler_params=_cparams("parallel", "arbitrary"),
        name="hyena_long_conv",
    )(x, cos_t, sin_t, cos_t, sin_t, kre, kim, k_nyq)


def _hyena_gate_kernel(y_ref, vg_ref, x0_ref, bias_ref, o_ref):
    o_ref[...] = ((y_ref[...] + vg_ref[...] * bias_ref[...]) * x0_ref[...]).astype(o_ref.dtype)


def hyena_gate(y, vg, x0, bias):
    b, l, w = y.shape
    m = b * l
    tm = 1024
    row = pl.BlockSpec((tm, w), lambda i: (i, 0))
    out = pl.pallas_call(
        _hyena_gate_kernel,
        grid=(m // tm,),
        in_specs=[row, row, row, pl.BlockSpec((1, w), lambda i: (0, 0))],
        out_specs=row,
        out_shape=jax.ShapeDtypeStruct((m, w), BF16),
        compiler_params=_cparams("parallel"),
        name="hyena_gate",
    )(y.reshape(m, w), vg.reshape(m, w), x0.reshape(m, w), bias.reshape(1, w))
    return out.reshape(b, l, w)


def _dense_dft_tables(l):
    n = 2 * l
    ang = 2.0 * np.pi * np.outer(np.arange(n), np.arange(n)) / n
    fwd = np.concatenate([np.cos(ang), -np.sin(ang)], axis=0)
    inv = np.concatenate([np.cos(ang[:l]), -np.sin(ang[:l])], axis=1) / n
    return jnp.asarray(fwd, F32), jnp.asarray(inv, F32)


def _dense_conv_kernel(vg_ref, x0_ref, kern_ref, f_ref, g_ref, bias_ref, o_ref):
    hp = lax.Precision.HIGHEST
    l = vg_ref.shape[1]
    n = 2 * l
    vg = vg_ref[0]
    ks = jnp.dot(f_ref[...], kern_ref[...], preferred_element_type=F32, precision=hp)
    xs = jnp.dot(f_ref[:, :l], vg, preferred_element_type=F32, precision=hp)
    kre, kim, xre, xim = ks[:n], ks[n:], xs[:n], xs[n:]
    ys = jnp.concatenate([xre * kre - xim * kim, xre * kim + xim * kre], axis=0)
    y = jnp.dot(g_ref[...], ys, preferred_element_type=F32, precision=hp)
    o_ref[0] = ((y + vg * bias_ref[...]) * x0_ref[0]).astype(o_ref.dtype)


def hyena_dense_conv(vg, x0, kern, bias):
    bsz, l, w = vg.shape
    fwd, inv = _dense_dft_tables(l)
    row = pl.BlockSpec((1, l, w), lambda i: (i, 0, 0))
    full = lambda a: pl.BlockSpec(a.shape, lambda i: (0, 0))
    return pl.pallas_call(
        _dense_conv_kernel,
        grid=(bsz,),
        in_specs=[row, row, full(kern), full(fwd), full(inv), pl.BlockSpec((1, w), lambda i: (0, 0))],
        out_specs=row,
        out_shape=jax.ShapeDtypeStruct((bsz, l, w), BF16),
        compiler_params=_cparams("parallel"),
        name="hyena_dense_conv",
    )(vg, x0, kern, fwd, inv, bias.reshape(1, w))


HY_DENSE_MAX_LEN = 512


def hyena_mixer(z, lp, tables=None):
    l = z.shape[1]
    x0, vg, vg_bf16 = hyena_short_conv(z, lp['hy_short_w'], lp['hy_short_b'])
    h_fwd, h_bwd = hyena_filters(l, lp)
    if l <= HY_DENSE_MAX_LEN:
        kern = jnp.concatenate([h_fwd, jnp.zeros_like(h_fwd[:1]), h_bwd[1:][::-1]], axis=0)
        return hyena_dense_conv(vg, x0, kern, lp['hy_bias'])
    cos_t, sin_t = tables if tables is not None else dft_tables(l)
    kre, kim, k_nyq = filter_spectrum(cos_t, sin_t, h_fwd, h_bwd)
    y = long_conv(vg_bf16, cos_t, sin_t, kre, kim, k_nyq)
    return hyena_gate(y, vg, x0, lp['hy_bias'])


S5_CHUNK = 16
S5_CK = S5_CHUNK * S5_GROUP
S5_OCT = LANES // S5_GROUP
S5_OCT_W = S5_OCT * S5_GROUP
S5_N_OCT = S5_GROUPS // S5_OCT
S5_ROW = S5_CHUNK * S5_OCT_W


def _s5_powers(lp, d):
    lam = lax.complex(jnp.minimum(lp['s5_a_re'][d], -1e-4), lp['s5_a_im'][d])
    dt = jnp.exp(lp['s5_log_dt'][d])[:, None]
    n = jnp.arange(S5_CHUNK + 1, dtype=F32)[:, None, None]
    apow = jnp.exp((lam * dt)[None] * n)
    b = lax.complex(lp['s5_b_re'][d], lp['s5_b_im'][d])
    bbar = ((apow[1] - 1.0) / lam)[..., None] * b
    cmat = lax.complex(lp['s5_c_re'][d], lp['s5_c_im'][d])
    return apow, bbar, cmat


def _block_diag_kernel(x_ref, o_ref, *, rb, cb):
    n_g, rc, cc = x_ref.shape
    out_c = o_ref.shape[2]
    col = lax.broadcasted_iota(jnp.int32, (cc, out_c), 1)
    src = lax.broadcasted_iota(jnp.int32, (cc, out_c), 0)
    for g in range(n_g):
        dest = (src // cb) * (n_g * cb) + g * cb + src % cb
        place = jnp.where(col == dest, 1.0, 0.0).astype(BF16)
        wide = jnp.dot(x_ref[g], place, preferred_element_type=F32).astype(BF16)
        for r in range(rc // rb):
            r0 = (r * n_g + g) * rb
            o_ref[0, r0:r0 + rb, :] = wide[r * rb:(r + 1) * rb, :]


def s5_block_diag(x, rb, cb):
    g_n, rc, cc = x.shape
    out_r, out_c = rc * S5_OCT, cc * S5_OCT
    return pl.pallas_call(
        functools.partial(_block_diag_kernel, rb=rb, cb=cb),
        grid=(g_n // S5_OCT,),
        in_specs=[pl.BlockSpec((S5_OCT, rc, cc), lambda o: (o, 0, 0))],
        out_specs=pl.BlockSpec((1, out_r, out_c), lambda o: (o, 0, 0)),
        out_shape=jax.ShapeDtypeStruct((g_n // S5_OCT, out_r, out_c), BF16),
        compiler_params=_cparams("parallel"),
        name="s5_block_diag",
    )(x)


def s5_prepare(lp):
    q = S5_CHUNK
    g_n, p_n = S5_GROUPS, S5_STATE
    hp = lax.Precision.HIGHEST
    ap_f, bb_f, c_f = _s5_powers(lp, 0)
    ap_b, bb_b, c_b = _s5_powers(lp, 1)
    k_f = jnp.einsum('gop,tgp,gpi->gtio', c_f, ap_f[:q], bb_f, precision=hp).real
    k_b = jnp.einsum('gop,tgp,gpi->gtio', c_b, ap_b[:q], bb_b, precision=hp).real
    zero = jnp.zeros_like(k_f)
    fwd_ext = jnp.concatenate([zero, k_f], axis=1)
    bwd_ext = jnp.concatenate([k_b[:, ::-1], zero], axis=1)
    m_f = jnp.stack([fwd_ext[:, q - sp:2 * q - sp] for sp in range(q)], axis=1)
    m_b = jnp.stack([bwd_ext[:, q - 1 - sp:2 * q - 1 - sp] for sp in range(q)], axis=1)
    m_intra = (m_f + m_b).transpose(0, 1, 3, 2, 4).reshape(g_n, S5_CK, S5_CK)

    ws_f = ap_f[q - 1 - jnp.arange(q)][:, :, :, None] * bb_f[None]
    ws_b = ap_b[jnp.arange(q)][:, :, :, None] * bb_b[None]
    ws = jnp.stack([ws_f.real, ws_f.imag, ws_b.real, ws_b.imag], axis=0)
    ws = ws.transpose(2, 1, 4, 0, 3).reshape(g_n, S5_CK, 4, p_n)

    ca_f = c_f[None] * ap_f[1:q + 1][:, :, None, :]
    ca_b = c_b[None] * ap_b[q - jnp.arange(q)][:, :, None, :]
    wo = jnp.stack([ca_f.real, -ca_f.imag, ca_b.real, -ca_b.imag], axis=0)
    wo = wo.transpose(2, 0, 4, 1, 3).reshape(g_n, 4, p_n, S5_CK)

    h_n = S5_GROUP
    m_oct = s5_block_diag(m_intra.astype(BF16), h_n, h_n)
    w_state = s5_block_diag(ws.reshape(g_n, S5_CK, 4 * p_n).astype(BF16), h_n, p_n)
    w_out = s5_block_diag(wo.reshape(g_n, 4 * p_n, S5_CK).astype(BF16), p_n, h_n)

    def flat(z):
        return jnp.stack([z.real.reshape(-1), z.imag.reshape(-1)], axis=0)

    return m_oct, w_state, w_out, flat(ap_f[q]), flat(ap_b[q])


def _s5_chunk_rows(u_ref, n_chunks):
    steps = [u_ref[0, pl.ds(s, n_chunks, stride=S5_CHUNK), :].astype(BF16) for s in range(S5_CHUNK)]
    return jnp.concatenate(steps, axis=1)


def _s5_states_kernel(u_ref, w_ref, fre_ref, fim_ref, bre_ref, bim_ref):
    n_chunks, n = fre_ref.shape[1:]
    o = jnp.dot(_s5_chunk_rows(u_ref, n_chunks), w_ref[0], preferred_element_type=F32)
    for comp, ref in enumerate((fre_ref, fim_ref, bre_ref, bim_ref)):
        ref[0] = o[:, comp * n:(comp + 1) * n]


def s5_states(u_all, w_state):
    b, t, w = u_all.shape
    nc = t // S5_CHUNK
    pw = S5_OCT * S5_STATE
    out = jax.ShapeDtypeStruct((b, nc, S5_GROUPS * S5_STATE), F32)
    ospec = pl.BlockSpec((1, nc, pw), lambda o, i: (i, 0, o))
    return pl.pallas_call(
        _s5_states_kernel,
        grid=(S5_N_OCT, b),
        in_specs=[pl.BlockSpec((1, t, S5_OCT_W), lambda o, i: (i, 0, o)),
                  pl.BlockSpec((1, S5_ROW, 4 * pw), lambda o, i: (o, 0, 0))],
        out_specs=[ospec] * 4,
        out_shape=[out] * 4,
        compiler_params=_cparams("parallel", "parallel"),
        name="s5_states",
    )(u_all, w_state)


def _s5_scan_kernel(sre_ref, sim_ref, a_ref, hre_ref, him_ref, *, n_ctx_tiles, reverse):
    tile = 8
    n_tiles = sre_ref.shape[1] // tile
    ar = a_ref[0:1, :]
    ai = a_ref[1:2, :]
    zero = jnp.zeros((1, sre_ref.shape[2]), F32)

    def body(n, carry):
        hre, him = carry
        if reverse:
            t = jnp.where(n < n_ctx_tiles, n_ctx_tiles - 1 - n, n_tiles + n_ctx_tiles - 1 - n)
        else:
            t = n
        r0 = pl.multiple_of(t * tile, tile)
        sre = sre_ref[0, pl.ds(r0, tile), :]
        sim = sim_ref[0, pl.ds(r0, tile), :]
        in_re = [None] * tile
        in_im = [None] * tile
        for r in (range(tile - 1, -1, -1) if reverse else range(tile)):
            in_re[r], in_im[r] = hre, him
            hre, him = ar * hre - ai * him + sre[r:r + 1], ar * him + ai * hre + sim[r:r + 1]
        hre_ref[0, pl.ds(r0, tile), :] = jnp.concatenate(in_re, axis=0)
        him_ref[0, pl.ds(r0, tile), :] = jnp.concatenate(in_im, axis=0)
        return hre, him

    lax.fori_loop(0, n_tiles, body, (zero, zero))


def s5_scan(sre, sim, a_chunk, n_ctx_chunks, reverse):
    b, nc, w = sre.shape
    tl = 512
    spec = pl.BlockSpec((1, nc, tl), lambda i, j: (i, 0, j))
    out = jax.ShapeDtypeStruct((b, nc, w), F32)
    return pl.pallas_call(
        functools.partial(_s5_scan_kernel, n_ctx_tiles=n_ctx_chunks // 8, reverse=reverse),
        grid=(b, w // tl),
        in_specs=[spec, spec, pl.BlockSpec((2, tl), lambda i, j: (0, j))],
        out_specs=[spec, spec],
        out_shape=[out, out],
        compiler_params=_cparams("parallel", "parallel"),
        name="s5_scan_bwd" if reverse else "s5_scan_fwd",
    )(sre, sim, a_chunk)


def _s5_outputs_kernel(u_ref, m_ref, fre_ref, fim_ref, bre_ref, bim_ref, wo_ref, y_ref):
    n_chunks = fre_ref.shape[1]
    carried = jnp.concatenate([r[0].astype(BF16) for r in (fre_ref, fim_ref, bre_ref, bim_ref)], axis=1)
    y = (jnp.dot(_s5_chunk_rows(u_ref, n_chunks), m_ref[0], preferred_element_type=F32)
         + jnp.dot(carried, wo_ref[0], preferred_element_type=F32))
    for s in range(S5_CHUNK):
        y_ref[0, pl.ds(s, n_chunks, stride=S5_CHUNK), :] = y[:, s * S5_OCT_W:(s + 1) * S5_OCT_W]


def s5_outputs(u_all, m_oct, carried, w_out):
    b, t, w = u_all.shape
    nc = t // S5_CHUNK
    pw = S5_OCT * S5_STATE
    hspec = pl.BlockSpec((1, nc, pw), lambda o, i: (i, 0, o))
    lane_blk = pl.BlockSpec((1, t, S5_OCT_W), lambda o, i: (i, 0, o))
    mat = pl.BlockSpec((1, S5_ROW, S5_ROW), lambda o, i: (o, 0, 0))
    return pl.pallas_call(
        _s5_outputs_kernel,
        grid=(S5_N_OCT, b),
        in_specs=[lane_blk, mat, hspec, hspec, hspec, hspec, mat],
        out_specs=lane_blk,
        out_shape=jax.ShapeDtypeStruct((b, t, w), F32),
        compiler_params=_cparams("parallel", "parallel"),
        name="s5_outputs",
    )(u_all, m_oct, *carried, w_out)


def _s5_readout_kernel(y_ref, u_ref, d_ref, w_ref, b_ref, o_ref):
    y = _gelu_tanh(y_ref[...] + d_ref[...] * u_ref[...])
    z = jnp.dot(y.astype(BF16), w_ref[...], preferred_element_type=F32) + b_ref[...]
    o_ref[...] = (y * jax.nn.sigmoid(z)).astype(o_ref.dtype)


def s5_readout(y, u, d, glu_w, glu_b):
    m, w = y.shape
    tm = min(m, 512)
    row = pl.BlockSpec((tm, w), lambda i: (i, 0))
    vec = pl.BlockSpec((1, w), lambda i: (0, 0))
    return pl.pallas_call(
        _s5_readout_kernel,
        grid=(m // tm,),
        in_specs=[row, row, vec, pl.BlockSpec((w, w), lambda i: (0, 0)), vec],
        out_specs=row,
        out_shape=jax.ShapeDtypeStruct((m, w), BF16),
        compiler_params=_cparams("parallel"),
        name="s5_readout",
    )(y, u, d.reshape(1, w), glu_w, glu_b.reshape(1, w))


def s5_mixer(u, uc, lp, glu_w):
    b, l, w = u.shape
    c = uc.shape[1]
    m_oct, w_state, w_out, a_f, a_b = s5_prepare(lp)
    u_all = jnp.concatenate([uc, u], axis=1)
    fre, fim, bre, bim = s5_states(u_all, w_state)
    hf = s5_scan(fre, fim, a_f, c // S5_CHUNK, False)
    hb = s5_scan(bre, bim, a_b, c // S5_CHUNK, True)
    y_all = s5_outputs(u_all, m_oct, (*hf, *hb), w_out)
    out = s5_readout(y_all.reshape(b * (c + l), w), u_all.reshape(b * (c + l), w), lp['s5_d'], glu_w,
                     lp['s5_glu_b'])
    out = out.reshape(b, c + l, w)
    return out[:, c:], out[:, :c]


def kernel(x, c, ctx, c_ctx, mod_w, mod_b, norm1_g, norm2_g, in_w, gate_b, attn_sink, hy_short_w, hy_short_b,
           hy_w1, hy_b1, hy_freq1, hy_w2, hy_b2, hy_freq2, hy_w3, hy_bias, s5_a_re, s5_a_im, s5_log_dt,
           s5_b_re, s5_b_im, s5_c_re, s5_c_im, s5_d, s5_glu_w, s5_glu_b, br_attn_w, br_hyena_w, br_s5_w,
           out_w, peer_wq, peer_keys, peer_u, peer_v, final_g):
    B, L, D = x.shape
    C = ctx.shape[1]
    depth = mod_w.shape[0]
    rope_cos, rope_sin = rope_tables(L)
    conv_tables = dft_tables(L) if L > HY_DENSE_MAX_LEN else None

    cvec = jnp.zeros((8, D), F32).at[:B].set(c).at[B].set(c_ctx)
    mod_all = modulation(cvec, mod_w, mod_b)

    xc = ctx
    for layer in range(depth):
        need_ctx = layer < depth - 1
        lp = dict(attn_sink=attn_sink[layer],
                  hy_short_w=hy_short_w[layer], hy_short_b=hy_short_b[layer],
                  hy_w1=hy_w1[layer], hy_b1=hy_b1[layer], hy_freq1=hy_freq1[layer],
                  hy_w2=hy_w2[layer], hy_b2=hy_b2[layer], hy_freq2=hy_freq2[layer],
                  hy_w3=hy_w3[layer], hy_bias=hy_bias[layer],
                  s5_a_re=s5_a_re[layer], s5_a_im=s5_a_im[layer], s5_log_dt=s5_log_dt[layer],
                  s5_b_re=s5_b_re[layer], s5_b_im=s5_b_im[layer],
                  s5_c_re=s5_c_re[layer], s5_c_im=s5_c_im[layer], s5_d=s5_d[layer],
                  s5_glu_w=s5_glu_w[layer], s5_glu_b=s5_glu_b[layer])
        mod = mod_all[layer]
        ml = [mod[:B, i * D:(i + 1) * D] for i in range(N_MOD)]
        mc = [mod[B:B + 1, i * D:(i + 1) * D] for i in range(N_MOD)]

        w_in = in_w[layer].astype(BF16)
        w_parts = [w_in[:, COL_Q:COL_S5], w_in[:, COL_S5:COL_HY], w_in[:, COL_HY:COL_GATE], w_in[:, COL_GATE:]]
        wa = br_attn_w[layer].astype(BF16)
        wh = br_hyena_w[layer].astype(BF16)
        ws = br_s5_w[layer].astype(BF16)
        wo = out_w[layer].astype(BF16)
        wqt = peer_wq[layer].T.astype(BF16)
        keys = peer_keys[layer].reshape(2 * PEER_HEADS, PEER_KEYS, PEER_HALF).astype(BF16)
        u_tab = peer_u[layer].astype(BF16)
        vt_tab = peer_v[layer].T.astype(BF16)

        h = norm_modulate(x, norm1_g[layer], ml[0], ml[1])
        hc = norm_modulate(xc, norm1_g[layer], mc[0], mc[1])
        qkv, u_s5, z_hy, gates = in_projection(h.reshape(B * L, D), w_parts, gate_b[layer])
        qkv_c, uc_s5, zc_hy, gates_c = in_projection(hc.reshape(B * C, D), w_parts, gate_b[layer])
        qkv = qkv.reshape(B, L, -1)
        qkv_c = qkv_c.reshape(B, C, -1)
        u_s5 = u_s5.reshape(B, L, -1)
        uc_s5 = uc_s5.reshape(B, C, -1)

        q, k, v = qkv_heads(qkv, rope_cos, rope_sin, True)
        qc, kc, vc = qkv_heads(qkv_c, rope_cos, rope_sin, False)
        att = attention(q, k, v, kc, vc, lp['attn_sink'], True)
        hy = hyena_mixer(z_hy.reshape(B, L, -1), lp, conv_tables)
        s5, s5c = s5_mixer(u_s5, uc_s5, lp, s5_glu_w[layer].astype(BF16))

        x = branch_merge(att, hy, s5, gates.reshape(B, L, -1), wa, wh, ws, wo, x, ml[2])
        x = peer_block(x, norm2_g[layer], ml[3], ml[4], ml[5], wqt, keys, u_tab, vt_tab, final_g,
                       out_norm=not need_ctx)

        if need_ctx:
            att_c = attention(qc, None, None, kc, vc, lp['attn_sink'], False)
            hy_c = hyena_mixer(zc_hy.reshape(B, C, -1), lp)
            xc = branch_merge(att_c, hy_c, s5c, gates_c.reshape(B, C, -1), wa, wh, ws, wo, xc, mc[2])
            xc = peer_block(xc, norm2_g[layer], mc[3], mc[4], mc[5], wqt, keys, u_tab, vt_tab, final_g)
    return x
```
